```python
import math
import jax
import jax.numpy as jnp
from jax import lax
import numpy as np

D_MODEL = 1024
BATCH = 8
SEQ = 2048
DEPTH = 4

SSD_HEADS = 16
SSD_HEAD_DIM = 64
SSD_INNER = SSD_HEADS * SSD_HEAD_DIM
SSD_GROUPS = 2
SSD_STATE = 128
SSD_CONV = 4
SSD_CHUNK = 128
S5_GROUPS = 32
S5_GROUP_CH = 16
S5_WIDTH = S5_GROUPS * S5_GROUP_CH
S5_STATE = 64
MIX_WIDTH = SSD_INNER + S5_WIDTH
CONV_CH = SSD_INNER + 2 * SSD_GROUPS * SSD_STATE
IN_COLS = SSD_INNER + CONV_CH + SSD_HEADS + S5_WIDTH
N_EXPERT_GROUPS = 4
EXPERTS_PER_GROUP = 8
N_EXPERTS = N_EXPERT_GROUPS * EXPERTS_PER_GROUP
TOP_K = 2
EXPERT_FF = 512
MOE_BLOCK = 128
EPS = 1e-6

kernel_name = 'hybrid_ssd_s5_hmoe_block'


def rmsnorm(x, w):
    xf = x.astype(jnp.float32)
    y = xf * lax.rsqrt(jnp.mean(xf * xf, axis=-1, keepdims=True) + EPS)
    return y.astype(x.dtype) * w


def modulate(h, shift, scale):
    return h * (1.0 + scale[:, None, :]) + shift[:, None, :]


def causal_depthwise_conv(u, w, b):
    k, ch = w.shape
    y = lax.conv_general_dilated(u, w[:, None, :], (1,), [(k - 1, 0)],
                                 dimension_numbers=('NWC', 'WIO', 'NWC'),
                                 feature_group_count=ch)
    return y + b


def ssd_chunked_scan(x, dt, a, bm, cm):
    b, s, h, p = x.shape
    g, n = bm.shape[2], bm.shape[3]
    r = h // g
    nc = s // SSD_CHUNK
    xd = (x * dt[..., None]).reshape(b, nc, SSD_CHUNK, g, r, p)
    ad = (dt * a).reshape(b, nc, SSD_CHUNK, g, r)
    bc = bm.reshape(b, nc, SSD_CHUNK, g, n)
    cc = cm.reshape(b, nc, SSD_CHUNK, g, n)
    a_cs = jnp.cumsum(ad, axis=2)
    seg = a_cs[:, :, :, None] - a_cs[:, :, None, :]
    causal = jnp.tril(jnp.ones((SSD_CHUNK, SSD_CHUNK), bool))[None, None, :, :, None, None]
    decay = jnp.exp(jnp.where(causal, seg, -jnp.inf))
    cb = jnp.einsum('bclgn,bcsgn->bclsg', cc, bc)
    y_diag = jnp.einsum('bclsgr,bcsgrp->bclgrp', decay * cb[..., None], xd)
    decay_states = jnp.exp(a_cs[:, :, -1:] - a_cs)
    states = jnp.einsum('bclgn,bclgrp->bcgrpn', bc, xd * decay_states[..., None])
    chunk_decay = jnp.exp(a_cs[:, :, -1])

    def step(hc, inp):
        st, dc = inp
        return hc * dc[..., None, None] + st, hc

    h0 = jnp.zeros((b, g, r, p, n), states.dtype)
    _, h_prev = lax.scan(step, h0, (jnp.moveaxis(states, 1, 0), jnp.moveaxis(chunk_decay, 1, 0)))
    h_prev = jnp.moveaxis(h_prev, 0, 1)
    y_off = jnp.einsum('bclgn,bcgrpn->bclgrp', cc, h_prev) * jnp.exp(a_cs)[..., None]
    return (y_diag + y_off).reshape(b, s, h, p)


def ssd_mixer(z, xbc, dt_raw, conv_w, conv_b, dt_bias, a_log, d_skip, norm_w):
    bsz, seqlen, _ = z.shape
    f32 = jnp.float32
    xbc = jax.nn.silu(causal_depthwise_conv(xbc, conv_w, conv_b))
    nb = SSD_GROUPS * SSD_STATE
    xs = xbc[..., :SSD_INNER].reshape(bsz, seqlen, SSD_HEADS, SSD_HEAD_DIM).astype(f32)
    bm = xbc[..., SSD_INNER:SSD_INNER + nb].reshape(bsz, seqlen, SSD_GROUPS, SSD_STATE).astype(f32)
    cm = xbc[..., SSD_INNER + nb:].reshape(bsz, seqlen, SSD_GROUPS, SSD_STATE).astype(f32)
    dt = jax.nn.softplus((dt_raw + dt_bias).astype(f32))
    a = -jnp.exp(a_log.astype(f32))
    y = ssd_chunked_scan(xs, dt, a, bm, cm) + d_skip.astype(f32)[:, None] * xs
    y = y.reshape(bsz, seqlen, SSD_INNER) * jax.nn.silu(z.astype(f32))
    yg = y.reshape(bsz, seqlen, SSD_GROUPS, SSD_INNER // SSD_GROUPS)
    yg = yg * lax.rsqrt(jnp.mean(yg * yg, axis=-1, keepdims=True) + EPS)
    return yg.reshape(bsz, seqlen, SSD_INNER).astype(z.dtype) * norm_w


def complex_linear_combine(e1, e2):
    a1r, a1i, b1r, b1i = e1
    a2r, a2i, b2r, b2i = e2
    return (a2r * a1r - a2i * a1i, a2r * a1i + a2i * a1r,
            a2r * b1r - a2i * b1i + b2r, a2r * b1i + a2i * b1r + b2i)


def s5_mixer(u, lam_re, lam_im, log_dt, b_re, b_im, c_re, c_im, d_skip, w_glu, b_glu, norm_w):
    bsz, seqlen, _ = u.shape
    f32 = jnp.float32
    uf = u.astype(f32)
    ug = uf.reshape(bsz, seqlen, S5_GROUPS, S5_GROUP_CH)
    lr, li = lam_re.astype(f32), lam_im.astype(f32)
    step = jnp.exp(log_dt.astype(f32))[:, None]
    mag = jnp.exp(lr * step)
    ab_re, ab_im = mag * jnp.cos(li * step), mag * jnp.sin(li * step)
    den = lr * lr + li * li
    nr = ab_re - 1.0
    coef_re = (nr * lr + ab_im * li) / den
    coef_im = (ab_im * lr - nr * li) / den
    br, bi = b_re.astype(f32), b_im.astype(f32)
    bb_re = coef_re[..., None] * br - coef_im[..., None] * bi
    bb_im = coef_re[..., None] * bi + coef_im[..., None] * br
    bu_re = jnp.einsum('bsgh,gph->bsgp', ug, bb_re)
    bu_im = jnp.einsum('bsgh,gph->bsgp', ug, bb_im)
    a_re = jnp.broadcast_to(ab_re, (1, seqlen, S5_GROUPS, S5_STATE))
    a_im = jnp.broadcast_to(ab_im, (1, seqlen, S5_GROUPS, S5_STATE))
    _, _, s_re, s_im = lax.associative_scan(complex_linear_combine, (a_re, a_im, bu_re, bu_im), axis=1)
    y = (jnp.einsum('ghp,bsgp->bsgh', c_re.astype(f32), s_re)
         - jnp.einsum('ghp,bsgp->bsgh', c_im.astype(f32), s_im))
    y = y.reshape(bsz, seqlen, S5_WIDTH) + d_skip.astype(f32) * uf
    y = jax.nn.gelu(y)
    y = y * jax.nn.sigmoid(y @ w_glu.astype(f32) + b_glu.astype(f32))
    return rmsnorm(y.astype(u.dtype), norm_w)


def hybrid_mixer(h, w_in, conv_w, conv_b, dt_bias, a_log, d_ssd, ssd_norm_w,
                 lam_re, lam_im, log_dt, b_re, b_im, c_re, c_im, s5_d, w_glu, b_glu, s5_norm_w, w_out):
    proj = h @ w_in
    i0 = SSD_INNER
    i1 = i0 + CONV_CH
    i2 = i1 + SSD_HEADS
    z, xbc, dt_raw, u5 = proj[..., :i0], proj[..., i0:i1], proj[..., i1:i2], proj[..., i2:]
    y_ssd = ssd_mixer(z, xbc, dt_raw, conv_w, conv_b, dt_bias, a_log, d_ssd, ssd_norm_w)
    y_s5 = s5_mixer(u5, lam_re, lam_im, log_dt, b_re, b_im, c_re, c_im, s5_d, w_glu, b_glu, s5_norm_w)
    return jnp.concatenate([y_ssd, y_s5], axis=-1) @ w_out


def hier_moe(h, w_rg, b_rg, w_re, b_re, w_eg, w_eu, w_ed):
    bsz, seqlen, d = h.shape
    t = bsz * seqlen
    hf = h.reshape(t, d)
    g_logits = (hf @ w_rg + b_rg).astype(jnp.float32)
    g_prob = jax.nn.softmax(g_logits, axis=-1)
    g_idx = jnp.argmax(g_logits, axis=-1).astype(jnp.int32)
    g_w = jnp.take_along_axis(g_prob, g_idx[:, None], axis=1)
    e_logits = (hf @ w_re + b_re).astype(jnp.float32).reshape(t, N_EXPERT_GROUPS, EXPERTS_PER_GROUP)
    e_logits = jnp.take_along_axis(e_logits, g_idx[:, None, None], axis=1)[:, 0]
    e_prob = jax.nn.softmax(e_logits, axis=-1)
    top_p, top_i = lax.top_k(e_prob, TOP_K)
    gate = g_w * top_p / jnp.sum(top_p, axis=-1, keepdims=True)
    expert_id = (g_idx[:, None] * EXPERTS_PER_GROUP + top_i).reshape(-1).astype(jnp.int32)
    n = t * TOP_K
    token_id = jnp.arange(n, dtype=jnp.int32) // TOP_K
    order = jnp.argsort(expert_id)
    sorted_e = expert_id[order]
    counts = jnp.bincount(expert_id, length=N_EXPERTS).astype(jnp.int32)
    starts = jnp.cumsum(counts) - counts
    padded = (counts + MOE_BLOCK - 1) // MOE_BLOCK * MOE_BLOCK
    padded_end = jnp.cumsum(padded)
    padded_start = padded_end - padded
    dest = padded_start[sorted_e] + jnp.arange(n, dtype=jnp.int32) - starts[sorted_e]
    n_pad = n + N_EXPERTS * MOE_BLOCK
    n_blocks = n_pad // MOE_BLOCK
    row_token = jnp.zeros((n_pad,), jnp.int32).at[dest].set(token_id[order])
    block_start = jnp.arange(n_blocks, dtype=jnp.int32) * MOE_BLOCK
    block_expert = jnp.minimum(jnp.searchsorted(padded_end, block_start, side='right'),
                               N_EXPERTS - 1).astype(jnp.int32)
    x_rows = hf[row_token].reshape(n_blocks, MOE_BLOCK, d)

    def expert_block(args):
        xb, e = args
        return (jax.nn.silu(xb @ w_eg[e]) * (xb @ w_eu[e])) @ w_ed[e]

    y_rows = lax.map(expert_block, (x_rows, block_expert)).reshape(n_pad, d)
    slot_dest = jnp.zeros((n,), jnp.int32).at[order].set(dest)
    y_slots = y_rows[slot_dest].reshape(t, TOP_K, d)
    out = jnp.einsum('tkd,tk->td', y_slots, gate.astype(h.dtype))
    return out.reshape(bsz, seqlen, d)


def setup_inputs(seed: int = 0) -> dict:
    key = jax.random.key(seed)
    ks = jax.random.split(key, 40)
    f32 = jnp.float32
    L = DEPTH

    def nrm(k, shape, std):
        return std * jax.random.normal(k, shape, f32)

    x = nrm(ks[0], (BATCH, SEQ, D_MODEL), 1.0)
    c = nrm(ks[1], (BATCH, D_MODEL), 1.0)
    w_ada = nrm(ks[2], (L, D_MODEL, 6 * D_MODEL), 0.5 * D_MODEL ** -0.5)
    b_ada = nrm(ks[3], (L, 6 * D_MODEL), 0.02)
    norm1_w = 1.0 + nrm(ks[4], (L, D_MODEL), 0.02)
    w_in = nrm(ks[5], (L, D_MODEL, IN_COLS), D_MODEL ** -0.5)
    conv_w = nrm(ks[6], (L, SSD_CONV, CONV_CH), SSD_CONV ** -0.5)
    conv_b = nrm(ks[7], (L, CONV_CH), 0.02)
    dt0 = jnp.exp(jax.random.uniform(ks[8], (L, SSD_HEADS), f32, math.log(1e-3), math.log(1e-1)))
    dt_bias = dt0 + jnp.log(-jnp.expm1(-dt0))
    a_log = jnp.log(jax.random.uniform(ks[9], (L, SSD_HEADS), f32, 1.0, 16.0))
    d_ssd = 1.0 + nrm(ks[10], (L, SSD_HEADS), 0.02)
    ssd_norm_w = 1.0 + nrm(ks[11], (L, SSD_INNER), 0.02)
    s5_lam_re = -0.5 + nrm(ks[12], (L, S5_GROUPS, S5_STATE), 0.01)
    s5_lam_im = math.pi * jnp.arange(S5_STATE, dtype=f32) + nrm(ks[13], (L, S5_GROUPS, S5_STATE), 0.01)
    s5_log_dt = jax.random.uniform(ks[14], (L, S5_GROUPS), f32, math.log(1e-3), math.log(1e-1))
    s5_b_re = nrm(ks[15], (L, S5_GROUPS, S5_STATE, S5_GROUP_CH), (2 * S5_GROUP_CH) ** -0.5)
    s5_b_im = nrm(ks[16], (L, S5_GROUPS, S5_STATE, S5_GROUP_CH), (2 * S5_GROUP_CH) ** -0.5)
    s5_c_re = nrm(ks[17], (L, S5_GROUPS, S5_GROUP_CH, S5_STATE), (2 * S5_STATE) ** -0.5)
    s5_c_im = nrm(ks[18], (L, S5_GROUPS, S5_GROUP_CH, S5_STATE), (2 * S5_STATE) ** -0.5)
    s5_d = nrm(ks[19], (L, S5_WIDTH), 1.0)
    w_glu = nrm(ks[20], (L, S5_WIDTH, S5_WIDTH), S5_WIDTH ** -0.5)
    b_glu = nrm(ks[21], (L, S5_WIDTH), 0.02)
    s5_norm_w = 1.0 + nrm(ks[22], (L, S5_WIDTH), 0.02)
    w_out = nrm(ks[23], (L, MIX_WIDTH, D_MODEL), MIX_WIDTH ** -0.5)
    norm2_w = 1.0 + nrm(ks[24], (L, D_MODEL), 0.02)
    w_rg = nrm(ks[25], (L, D_MODEL, N_EXPERT_GROUPS), D_MODEL ** -0.5)
    b_rg = nrm(ks[26], (L, N_EXPERT_GROUPS), 0.01)
    w_re = nrm(ks[27], (L, D_MODEL, N_EXPERTS), D_MODEL ** -0.5)
    b_re = nrm(ks[28], (L, N_EXPERTS), 0.01)
    w_eg = nrm(ks[29], (L, N_EXPERTS, D_MODEL, EXPERT_FF), D_MODEL ** -0.5)
    w_eu = nrm(ks[30], (L, N_EXPERTS, D_MODEL, EXPERT_FF), D_MODEL ** -0.5)
    w_ed = nrm(ks[31], (L, N_EXPERTS, EXPERT_FF, D_MODEL), EXPERT_FF ** -0.5)
    final_norm_w = 1.0 + nrm(ks[32], (D_MODEL,), 0.02)
    return {'x': x, 'c': c, 'w_ada': w_ada, 'b_ada': b_ada, 'norm1_w': norm1_w, 'w_in': w_in,
            'conv_w': conv_w, 'conv_b': conv_b, 'dt_bias': dt_bias, 'a_log': a_log, 'd_ssd': d_ssd,
            'ssd_norm_w': ssd_norm_w, 's5_lam_re': s5_lam_re, 's5_lam_im': s5_lam_im,
            's5_log_dt': s5_log_dt, 's5_b_re': s5_b_re, 's5_b_im': s5_b_im, 's5_c_re': s5_c_re,
            's5_c_im': s5_c_im, 's5_d': s5_d, 'w_glu': w_glu, 'b_glu': b_glu, 's5_norm_w': s5_norm_w,
            'w_out': w_out, 'norm2_w': norm2_w, 'w_rg': w_rg, 'b_rg': b_rg, 'w_re': w_re, 'b_re': b_re,
            'w_eg': w_eg, 'w_eu': w_eu, 'w_ed': w_ed, 'final_norm_w': final_norm_w}


def reference(x, c, w_ada, b_ada, norm1_w, w_in, conv_w, conv_b, dt_bias, a_log, d_ssd, ssd_norm_w,
              s5_lam_re, s5_lam_im, s5_log_dt, s5_b_re, s5_b_im, s5_c_re, s5_c_im, s5_d, w_glu, b_glu,
              s5_norm_w, w_out, norm2_w, w_rg, b_rg, w_re, b_re, w_eg, w_eu, w_ed, final_norm_w):
    c_act = jax.nn.silu(c)
    for l in range(DEPTH):
        ada = c_act @ w_ada[l] + b_ada[l]
        shift_m, scale_m, gate_m, shift_f, scale_f, gate_f = jnp.split(ada, 6, axis=-1)
        h = modulate(rmsnorm(x, norm1_w[l]), shift_m, scale_m)
        m = hybrid_mixer(h, w_in[l], conv_w[l], conv_b[l], dt_bias[l], a_log[l], d_ssd[l], ssd_norm_w[l],
                         s5_lam_re[l], s5_lam_im[l], s5_log_dt[l], s5_b_re[l], s5_b_im[l], s5_c_re[l],
                         s5_c_im[l], s5_d[l], w_glu[l], b_glu[l], s5_norm_w[l], w_out[l])
        x = x + gate_m[:, None, :] * m
        h = modulate(rmsnorm(x, norm2_w[l]), shift_f, scale_f)
        f = hier_moe(h, w_rg[l], b_rg[l], w_re[l], b_re[l], w_eg[l], w_eu[l], w_ed[l])
        x = x + gate_f[:, None, :] * f
    return rmsnorm(x, final_norm_w)
```

```python
import functools
import math

import jax
import jax.numpy as jnp
from jax import lax
from jax.experimental import pallas as pl
from jax.experimental.pallas import tpu as pltpu

F32 = jnp.float32
BF16 = jnp.bfloat16
EPS = 1e-6

SSD_GROUPS = 2
SSD_STATE = 128
SSD_CHUNK = 128
S5_CHUNK = 16
N_EXPERT_GROUPS = 4
TOP_K = 2
LANES = 128
ROUTER_ROWS = 40
VMEM_LIMIT = 48 * 1024 * 1024


def _dot(a, b):
    return jnp.dot(a, b, preferred_element_type=F32)


def _split2(a):
    hi = a.astype(BF16)
    lo = (a - hi.astype(F32)).astype(BF16)
    return hi, lo


def _split3(a):
    hi = a.astype(BF16)
    r = a - hi.astype(F32)
    mid = r.astype(BF16)
    lo = (r - mid.astype(F32)).astype(BF16)
    return hi, mid, lo


def _params(*sem):
    return pltpu.CompilerParams(dimension_semantics=sem, vmem_limit_bytes=VMEM_LIMIT)


def _ada_kernel(c_ref, w_ref, b_ref, o_ref):
    ca = jax.nn.silu(c_ref[...])
    c_hi, c_lo = _split2(ca)
    w_hi, w_lo = _split2(w_ref[0])
    o_ref[0] = _dot(c_hi, w_hi) + _dot(c_lo, w_hi) + _dot(c_hi, w_lo) + b_ref[0]


def _ada(c, w_ada, b_ada):
    n_layers, d, n_out = w_ada.shape
    bsz = c.shape[0]
    tn = 1536
    return pl.pallas_call(
        _ada_kernel,
        grid=(n_layers, n_out // tn),
        in_specs=[pl.BlockSpec((bsz, d), lambda l, j: (0, 0)),
                  pl.BlockSpec((1, d, tn), lambda l, j: (l, 0, j)),
                  pl.BlockSpec((1, 1, tn), lambda l, j: (l, 0, j))],
        out_specs=pl.BlockSpec((1, bsz, tn), lambda l, j: (l, 0, j)),
        out_shape=jax.ShapeDtypeStruct((n_layers, bsz, n_out), F32),
        compiler_params=_params("parallel", "parallel"),
        name="ada",
    )(c, w_ada, b_ada.reshape(n_layers, 1, n_out))


def _inproj_kernel(x_ref, nw_ref, shift_ref, scale_ref, w_ref, z_ref, xbc_ref, u_ref, dt_ref,
                   *, n_z, n_xbc, n_u):
    x = x_ref[...]
    h = x * lax.rsqrt(jnp.mean(x * x, axis=-1, keepdims=True) + EPS) * nw_ref[...]
    h = h * (1.0 + scale_ref[0]) + shift_ref[0]
    p = _dot(h.astype(BF16), w_ref[...])
    z_ref[...] = p[:, :n_z]
    xbc_ref[...] = p[:, n_z:n_z + n_xbc]
    u_ref[...] = p[:, n_z + n_xbc:n_z + n_xbc + n_u]
    dt_ref[...] = p[:, n_z + n_xbc + n_u:]


def _inproj(x2, nw, ada3, w_cat, layer, bsz, seqlen, n_z, n_xbc, n_u):
    t, d = x2.shape
    tm = min(512, seqlen)
    per_b = seqlen // tm
    row = lambda i: (i, 0)
    ada_blk = lambda k: pl.BlockSpec((1, 1, d), lambda i: (layer * bsz + i // per_b, 0, k))
    return pl.pallas_call(
        functools.partial(_inproj_kernel, n_z=n_z, n_xbc=n_xbc, n_u=n_u),
        grid=(t // tm,),
        in_specs=[pl.BlockSpec((tm, d), row),
                  pl.BlockSpec((1, d), lambda i: (0, 0)),
                  ada_blk(0), ada_blk(1),
                  pl.BlockSpec(w_cat.shape, lambda i: (0, 0))],
        out_specs=[pl.BlockSpec((tm, n_z), row), pl.BlockSpec((tm, n_xbc), row),
                   pl.BlockSpec((tm, n_u), row), pl.BlockSpec((tm, LANES), row)],
        out_shape=[jax.ShapeDtypeStruct((t, n_z), F32), jax.ShapeDtypeStruct((t, n_xbc), F32),
                   jax.ShapeDtypeStruct((t, n_u), F32), jax.ShapeDtypeStruct((t, LANES), F32)],
        compiler_params=_params("parallel"),
        name="inproj",
    )(x2, nw, ada3, ada3, w_cat)


def _ssd_kernel(z_ref, xbc_ref, dt_ref, cw_ref, cb_ref, dtb_ref, alog_ref, dexp_ref, nw_ref,
                tri_ref, e_ref, y_ref, ext_ref, st_ref, *, inner, heads, n_conv):
    L = SSD_CHUNK
    N = SSD_STATE
    hd = inner // heads
    hpg = heads // SSD_GROUPS
    gw = inner // SSD_GROUPS
    tail = 8

    @pl.when(pl.program_id(1) == 0)
    def _():
        ext_ref[0:tail, :] = jnp.zeros((tail, ext_ref.shape[1]), F32)
        st_ref[...] = jnp.zeros(st_ref.shape, F32)

    ext_ref[tail:tail + L, :] = xbc_ref[...]
    conv = cb_ref[...]
    for k in range(n_conv):
        conv = conv + cw_ref[k:k + 1, :] * ext_ref[pl.ds(tail - (n_conv - 1) + k, L), :]
    ext_ref[0:tail, :] = ext_ref[L:L + tail, :]
    xa = jax.nn.silu(conv)
    xs = xa[:, :inner]
    bm = xa[:, inner:inner + SSD_GROUPS * N]
    cm = xa[:, inner + SSD_GROUPS * N:]

    pre = dt_ref[...] + dtb_ref[...]
    dt = jnp.maximum(pre, 0.0) + jnp.log1p(jnp.exp(-jnp.abs(pre)))
    ad = dt * (-jnp.exp(alog_ref[...]))
    tri = tri_ref[...]
    a1, a2, a3 = _split3(ad)
    acs = _dot(tri, a1) + _dot(tri, a2) + _dot(tri, a3)
    acs_t = acs.T
    dt_t = dt.T
    w = dt * jnp.exp(acs[L - 1:L, :] - acs)
    eacs = jnp.exp(acs)
    e_mat = e_ref[...]
    w1, w2 = _split2(w)
    w_exp = _dot(w1, e_mat) + _dot(w2, e_mat)
    q1, q2 = _split2(eacs)
    eacs_exp = _dot(q1, e_mat) + _dot(q2, e_mat)

    row = lax.broadcasted_iota(jnp.int32, (L, L), 0)
    col = lax.broadcasted_iota(jnp.int32, (L, L), 1)
    causal = row >= col
    lane = lax.broadcasted_iota(jnp.int32, (L, 2 * hd), 1)
    h_prev = st_ref[...]

    y_diag, y_off, st_new = [], [], []
    for g in range(SSD_GROUPS):
        bg = bm[:, g * N:(g + 1) * N]
        cg = cm[:, g * N:(g + 1) * N].astype(BF16)
        cb = lax.dot_general(cg, bg.astype(BF16), (((1,), (1,)), ((), ())), preferred_element_type=F32)
        for j in range(hpg // 2):
            h0 = g * hpg + 2 * j
            lms = []
            for h in (h0, h0 + 1):
                seg = jnp.where(causal, acs[:, h:h + 1] - acs_t[h:h + 1, :], -1e30)
                lms.append((cb * jnp.exp(seg) * dt_t[h:h + 1, :]).astype(BF16))
            pair = xs[:, h0 * hd:(h0 + 2) * hd]
            w_bd = jnp.concatenate([jnp.where(lane < hd, pair, 0.0), jnp.where(lane >= hd, pair, 0.0)],
                                   axis=0).astype(BF16)
            y_diag.append(_dot(jnp.concatenate(lms, axis=1), w_bd))
        sl = slice(g * gw, (g + 1) * gw)
        y_off.append(_dot(cg, h_prev[:, sl].astype(BF16)))
        st_new.append(_dot(bg.T.astype(BF16), (xs[:, sl] * w_exp[:, sl]).astype(BF16)))

    y = jnp.concatenate(y_diag, axis=1) + jnp.concatenate(y_off, axis=1) * eacs_exp + dexp_ref[...] * xs
    st_ref[...] = h_prev * eacs_exp[L - 1:L, :] + jnp.concatenate(st_new, axis=1)
    y = y * jax.nn.silu(z_ref[...])
    outs = []
    for g in range(SSD_GROUPS):
        yg = y[:, g * gw:(g + 1) * gw]
        outs.append(yg * lax.rsqrt(jnp.mean(yg * yg, axis=-1, keepdims=True) + EPS))
    y_ref[...] = (jnp.concatenate(outs, axis=1) * nw_ref[...]).astype(y_ref.dtype)


def _ssd(z, xbc, dtp, conv_w, conv_b, dtb, alog, dexp, nw, tri, e_mat, bsz, seqlen, heads):
    t, inner = z.shape
    cch = xbc.shape[1]
    L = SSD_CHUNK
    nc = seqlen // L
    n_conv = conv_w.shape[0]
    row = lambda b, c: (b * nc + c, 0)
    const = lambda shape: pl.BlockSpec(shape, lambda b, c: (0, 0))
    return pl.pallas_call(
        functools.partial(_ssd_kernel, inner=inner, heads=heads, n_conv=n_conv),
        grid=(bsz, nc),
        in_specs=[pl.BlockSpec((L, inner), row), pl.BlockSpec((L, cch), row), pl.BlockSpec((L, LANES), row),
                  const((n_conv, cch)), const((1, cch)), const((1, LANES)), const((1, LANES)),
                  const((1, inner)), const((1, inner)), const((L, L)), const((LANES, inner))],
        out_specs=pl.BlockSpec((L, inner), row),
        out_shape=jax.ShapeDtypeStruct((t, inner), BF16),
        scratch_shapes=[pltpu.VMEM((L + 8, cch), F32), pltpu.VMEM((SSD_STATE, inner), F32)],
        compiler_params=_params("parallel", "arbitrary"),
        name="ssd",
    )(z, xbc, dtp, conv_w, conv_b, dtb, alog, dexp, nw, tri, e_mat)


def _s5_kernel(u_ref, toep_ref, wsr_ref, wsi_ref, wor_ref, woi_ref, ar_ref, ai_ref, y_ref,
               sre, sim, pre, pim, *, nb, nc):
    u = u_ref[0]
    sre[...] = _dot(u, wsr_ref[0])
    sim[...] = _dot(u, wsi_ref[0])
    p = sre.shape[1]
    ar = jnp.broadcast_to(ar_ref[0], (nb, p))
    ai = jnp.broadcast_to(ai_ref[0], (nb, p))

    def body(c, carry):
        r, i = carry
        idx = pl.ds(pl.multiple_of(c * nb, nb), nb)
        pre[idx, :] = r
        pim[idx, :] = i
        return ar * r - ai * i + sre[idx, :], ar * i + ai * r + sim[idx, :]

    zero = jnp.zeros((nb, p), F32)
    lax.fori_loop(0, nc, body, (zero, zero))
    y_ref[0] = (_dot(u, toep_ref[0]) + _dot(pre[...].astype(BF16), wor_ref[0])
                + _dot(pim[...].astype(BF16), woi_ref[0]))


def _s5_core(u_fold, toep, wsr, wsi, wor, woi, a_re, a_im, bsz):
    g, m, k = u_fold.shape
    p = wsr.shape[2]
    nc = m // bsz
    blk = lambda a: pl.BlockSpec((1,) + a.shape[1:], lambda i: (i, 0, 0))
    return pl.pallas_call(
        functools.partial(_s5_kernel, nb=bsz, nc=nc),
        grid=(g,),
        in_specs=[blk(u_fold), blk(toep), blk(wsr), blk(wsi), blk(wor), blk(woi), blk(a_re), blk(a_im)],
        out_specs=pl.BlockSpec((1, m, k), lambda i: (i, 0, 0)),
        out_shape=jax.ShapeDtypeStruct((g, m, k), F32),
        scratch_shapes=[pltpu.VMEM((m, p), F32)] * 4,
        compiler_params=_params("parallel"),
        name="s5_core",
    )(u_fold, toep, wsr, wsi, wor, woi, a_re, a_im)


def _s5_weights(lam_re, lam_im, log_dt, b_re, b_im, c_re, c_im):
    lc = S5_CHUNK
    g, p, h = b_re.shape
    step = jnp.exp(log_dt)[:, None]
    mag = jnp.exp(lam_re * step)
    ab_re, ab_im = mag * jnp.cos(lam_im * step), mag * jnp.sin(lam_im * step)
    den = lam_re * lam_re + lam_im * lam_im
    nr = ab_re - 1.0
    coef_re = (nr * lam_re + ab_im * lam_im) / den
    coef_im = (ab_im * lam_re - nr * lam_im) / den
    bb_re = coef_re[..., None] * b_re - coef_im[..., None] * b_im
    bb_im = coef_re[..., None] * b_im + coef_im[..., None] * b_re
    jj = jnp.arange(lc + 1, dtype=F32)[:, None, None]
    pmag = jnp.exp(jj * (lam_re * step)[None])
    pw_re = pmag * jnp.cos(jj * (lam_im * step)[None])
    pw_im = pmag * jnp.sin(jj * (lam_im * step)[None])
    cp_re = c_re[None] * pw_re[:, :, None, :] - c_im[None] * pw_im[:, :, None, :]
    cp_im = c_re[None] * pw_im[:, :, None, :] + c_im[None] * pw_re[:, :, None, :]
    kern = (jnp.einsum('jghp,gpk->jghk', cp_re[:lc], bb_re, precision='highest')
            - jnp.einsum('jghp,gpk->jghk', cp_im[:lc], bb_im, precision='highest'))
    tt = jnp.arange(lc)
    lag = tt[None, :] - tt[:, None]
    toep = jnp.where((lag >= 0)[:, :, None, None, None], kern[jnp.clip(lag, 0, lc - 1)], 0.0)
    toep = toep.transpose(2, 0, 4, 1, 3).reshape(g, lc * h, lc * h)
    rev_re, rev_im = pw_re[lc - 1 - tt], pw_im[lc - 1 - tt]
    ws_re = rev_re[..., None] * bb_re[None] - rev_im[..., None] * bb_im[None]
    ws_im = rev_re[..., None] * bb_im[None] + rev_im[..., None] * bb_re[None]
    ws_re = ws_re.transpose(1, 0, 3, 2).reshape(g, lc * h, p)
    ws_im = ws_im.transpose(1, 0, 3, 2).reshape(g, lc * h, p)
    wo_re = cp_re[1:].transpose(1, 3, 0, 2).reshape(g, p, lc * h)
    wo_im = (-cp_im[1:]).transpose(1, 3, 0, 2).reshape(g, p, lc * h)
    return (toep.astype(BF16), ws_re.astype(BF16), ws_im.astype(BF16), wo_re.astype(BF16),
            wo_im.astype(BF16), pw_re[lc][:, None, :], pw_im[lc][:, None, :])


def _outproj_kernel(x_ref, ys_ref, y5_ref, u5_ref, d5_ref, wg_ref, bg_ref, n5_ref, wa_ref, wb_ref,
                    gate_ref, nw_ref, shift_ref, scale_ref, wrh_ref, wrl_ref, br_ref,
                    x1_ref, h2_ref, lg_ref):
    y = y5_ref[...] + d5_ref[...] * u5_ref[...]
    y = jax.nn.gelu(y)
    y = y * jax.nn.sigmoid(_dot(y.astype(BF16), wg_ref[...]) + bg_ref[...])
    y = y * lax.rsqrt(jnp.mean(y * y, axis=-1, keepdims=True) + EPS) * n5_ref[...]
    m = _dot(ys_ref[...], wa_ref[...]) + _dot(y.astype(BF16), wb_ref[...])
    x1 = x_ref[...] + gate_ref[0] * m
    x1_ref[...] = x1
    h = x1 * lax.rsqrt(jnp.mean(x1 * x1, axis=-1, keepdims=True) + EPS) * nw_ref[...]
    h = h * (1.0 + scale_ref[0]) + shift_ref[0]
    h2_ref[...] = h
    h_hi, h_lo = _split2(h)
    wrh = wrh_ref[...]
    lg = _dot(h_hi, wrh) + _dot(h_lo, wrh) + _dot(h_hi, wrl_ref[...]) + br_ref[...]
    lg_ref[...] = lg.T[:ROUTER_ROWS, :]


def _outproj(x2, y_ssd, y5, u5, d5, w_glu, b_glu, n5, w_a, w_b, nw, ada3, wr_hi, wr_lo, br,
             layer, bsz, seqlen):
    t, d = x2.shape
    n_s = y_ssd.shape[1]
    n_5 = y5.shape[1]
    tm = min(512, seqlen)
    per_b = seqlen // tm
    row = lambda i: (i, 0)
    const = lambda a: pl.BlockSpec(a.shape, lambda i: (0, 0))
    ada_blk = lambda k: pl.BlockSpec((1, 1, d), lambda i: (layer * bsz + i // per_b, 0, k))
    return pl.pallas_call(
        _outproj_kernel,
        grid=(t // tm,),
        in_specs=[pl.BlockSpec((tm, d), row), pl.BlockSpec((tm, n_s), row), pl.BlockSpec((tm, n_5), row),
                  pl.BlockSpec((tm, n_5), row), const(d5), const(w_glu), const(b_glu), const(n5),
                  const(w_a), const(w_b), ada_blk(2), const(nw), ada_blk(3), ada_blk(4),
                  const(wr_hi), const(wr_lo), const(br)],
        out_specs=[pl.BlockSpec((tm, d), row), pl.BlockSpec((tm, d), row),
                   pl.BlockSpec((ROUTER_ROWS, tm), lambda i: (0, i))],
        out_shape=[jax.ShapeDtypeStruct((t, d), F32), jax.ShapeDtypeStruct((t, d), F32),
                   jax.ShapeDtypeStruct((ROUTER_ROWS, t), F32)],
        compiler_params=_params("parallel"),
        name="outproj",
    )(x2, y_ssd, y5, u5, d5, w_glu, b_glu, n5, w_a, w_b, ada3, nw, ada3, ada3, wr_hi, wr_lo, br)


def _route_kernel(lg_ref, upper_ref, dest_ref, gate_ref, cnt_ref, counts, carry, pstart,
                  *, n_experts, per_group, block_rows, sub):
    ph = pl.program_id(0)
    i = pl.program_id(1)
    tr = lg_ref.shape[1]
    lg = lg_ref[...]

    @pl.when((ph == 0) & (i == 0))
    def _():
        counts[...] = jnp.zeros(counts.shape, F32)

    gl = [lg[n_experts + k:n_experts + k + 1, :] for k in range(N_EXPERT_GROUPS)]
    gmax = functools.reduce(jnp.maximum, gl)
    gidx = jnp.full((1, tr), N_EXPERT_GROUPS - 1, jnp.int32)
    for k in range(N_EXPERT_GROUPS - 2, -1, -1):
        gidx = jnp.where(gl[k] == gmax, k, gidx)
    gsum = functools.reduce(lambda a, b: a + b, [jnp.exp(v - gmax) for v in gl])
    g_w = 1.0 / gsum
    el = lg[0:per_group, :]
    for k in range(1, N_EXPERT_GROUPS):
        el = jnp.where(gidx == k, lg[k * per_group:(k + 1) * per_group, :], el)
    ep = jnp.exp(el - jnp.max(el, axis=0, keepdims=True))
    prob = ep / jnp.sum(ep, axis=0, keepdims=True)
    jj = lax.broadcasted_iota(jnp.int32, (per_group, tr), 0).astype(F32)
    p1 = jnp.max(prob, axis=0, keepdims=True)
    i1 = jnp.min(jnp.where(prob == p1, jj, float(per_group)), axis=0, keepdims=True)
    prob2 = jnp.where(jj == i1, -1.0, prob)
    p2 = jnp.max(prob2, axis=0, keepdims=True)
    i2 = jnp.min(jnp.where(prob2 == p2, jj, float(per_group)), axis=0, keepdims=True)
    den = p1 + p2
    gate_ref[0] = jnp.concatenate([g_w * p1 / den, g_w * p2 / den], axis=0)
    e1 = gidx * per_group + i1.astype(jnp.int32)
    e2 = gidx * per_group + i2.astype(jnp.int32)
    rr = lax.broadcasted_iota(jnp.int32, (n_experts, tr), 0)
    oh1 = rr == e1
    oh2 = rr == e2
    member = jnp.where(oh1 | oh2, 1.0, 0.0)

    @pl.when(ph == 0)
    def _():
        counts[...] = counts[...] + jnp.sum(member, axis=1, keepdims=True)
        dest_ref[...] = jnp.zeros(dest_ref.shape, jnp.int32)

    @pl.when(ph == 1)
    def _():
        @pl.when(i == 0)
        def _():
            blocks = (counts[...].astype(jnp.int32) + (block_rows - 1)) >> int(math.log2(block_rows))
            hi = (blocks >> 4).astype(F32).astype(BF16)
            lo = (blocks & 15).astype(F32).astype(BF16)
            er = lax.broadcasted_iota(jnp.int32, (n_experts, n_experts), 0)
            ec = lax.broadcasted_iota(jnp.int32, (n_experts, n_experts), 1)
            lower = jnp.where(ec < er, 1.0, 0.0).astype(BF16)
            pstart[...] = (16.0 * _dot(lower, hi) + _dot(lower, lo)) * float(block_rows)
            carry[...] = jnp.zeros(carry.shape, F32)

        run = carry[...]
        upper = upper_ref[...]
        pieces = []
        for b in range(tr // sub):
            mb = member[:, b * sub:(b + 1) * sub]
            pieces.append(_dot(mb.astype(BF16), upper) + jnp.concatenate([run] * (sub // LANES), axis=1))
            run = run + jnp.sum(mb, axis=1, keepdims=True)
        carry[...] = run
        base = jnp.concatenate(pieces, axis=1) + jnp.concatenate([pstart[...]] * (tr // LANES), axis=1)
        d1 = jnp.sum(jnp.where(oh1, base, 0.0), axis=0, keepdims=True)
        d2 = jnp.sum(jnp.where(oh2, base, 0.0), axis=0, keepdims=True)
        dest_ref[0] = jnp.concatenate([d1, d2], axis=0).astype(jnp.int32)

    cnt_ref[...] = counts[...].astype(jnp.int32)


def _route(logits_t, upper, n_experts, block_rows):
    rows, t = logits_t.shape
    tr = 1024
    sub = upper.shape[0]
    return pl.pallas_call(
        functools.partial(_route_kernel, n_experts=n_experts, per_group=n_experts // N_EXPERT_GROUPS,
                          block_rows=block_rows, sub=sub),
        grid=(2, t // tr),
        in_specs=[pl.BlockSpec((rows, tr), lambda ph, i: (0, i)),
                  pl.BlockSpec(upper.shape, lambda ph, i: (0, 0))],
        out_specs=[pl.BlockSpec((1, TOP_K, tr), lambda ph, i: (ph, 0, i)),
                   pl.BlockSpec((1, TOP_K, tr), lambda ph, i: (ph, 0, i)),
                   pl.BlockSpec((n_experts, LANES), lambda ph, i: (0, 0))],
        out_shape=[jax.ShapeDtypeStruct((2, TOP_K, t), jnp.int32), jax.ShapeDtypeStruct((2, TOP_K, t), F32),
                   jax.ShapeDtypeStruct((n_experts, LANES), jnp.int32)],
        scratch_shapes=[pltpu.VMEM((n_experts, LANES), F32)] * 3,
        compiler_params=_params("arbitrary", "arbitrary"),
        name="route",
    )(logits_t, upper)


def _row_copy(src, dst, sem, s, d):
    return pltpu.make_async_copy(src.at[pl.ds(s, 1)], dst.at[pl.ds(d, 1)], sem)


def _dispatch_kernel(dest_ref, h_ref, init_ref, rows_ref, sem, *, tq, t):
    del init_ref
    base = pl.program_id(0) * tq

    def issue(j, _):
        for k in range(TOP_K):
            _row_copy(h_ref, rows_ref, sem, base + j, dest_ref[k * t + base + j]).start()
        return 0

    def drain(j, _):
        for k in range(TOP_K):
            _row_copy(h_ref, rows_ref, sem, base + j, dest_ref[k * t + base + j]).wait()
        return 0

    lax.fori_loop(0, tq, issue, 0)
    lax.fori_loop(0, tq, drain, 0)


def _dispatch(dest_flat, h2, n_pad):
    t, d = h2.shape
    tq = 256
    init = jnp.zeros((n_pad, d), h2.dtype)
    return pl.pallas_call(
        functools.partial(_dispatch_kernel, tq=tq, t=t),
        grid_spec=pltpu.PrefetchScalarGridSpec(
            num_scalar_prefetch=1, grid=(t // tq,),
            in_specs=[pl.BlockSpec(memory_space=pl.ANY), pl.BlockSpec(memory_space=pl.ANY)],
            out_specs=pl.BlockSpec(memory_space=pl.ANY),
            scratch_shapes=[pltpu.SemaphoreType.DMA]),
        out_shape=jax.ShapeDtypeStruct((n_pad, d), h2.dtype),
        input_output_aliases={2: 0},
        compiler_params=_params("arbitrary"),
        name="dispatch",
    )(dest_flat, h2, init)


def _ffn_kernel(be_ref, nused_ref, x_ref, wg_ref, wu_ref, wd_ref, y_ref):
    del be_ref
    i = pl.program_id(0)

    @pl.when(i < nused_ref[0])
    def _():
        xb = x_ref[...].astype(BF16)
        a = jax.nn.silu(_dot(xb, wg_ref[0])) * _dot(xb, wu_ref[0])
        y_ref[...] = _dot(a.astype(BF16), wd_ref[0])

    @pl.when(i >= nused_ref[0])
    def _():
        y_ref[...] = jnp.zeros(y_ref.shape, y_ref.dtype)


def _ffn(block_expert, n_used, rows, w_eg, w_eu, w_ed, bm):
    n_pad, d = rows.shape
    ff = w_eg.shape[2]
    return pl.pallas_call(
        _ffn_kernel,
        grid_spec=pltpu.PrefetchScalarGridSpec(
            num_scalar_prefetch=2, grid=(n_pad // bm,),
            in_specs=[pl.BlockSpec((bm, d), lambda i, be, nu: (i, 0)),
                      pl.BlockSpec((1, d, ff), lambda i, be, nu: (be[i], 0, 0)),
                      pl.BlockSpec((1, d, ff), lambda i, be, nu: (be[i], 0, 0)),
                      pl.BlockSpec((1, ff, d), lambda i, be, nu: (be[i], 0, 0))],
            out_specs=pl.BlockSpec((bm, d), lambda i, be, nu: (i, 0))),
        out_shape=jax.ShapeDtypeStruct((n_pad, d), F32),
        compiler_params=_params("arbitrary"),
        name="ffn",
    )(block_expert, n_used, rows, w_eg, w_eu, w_ed)


def _combine_kernel(dest_ref, yrows_ref, x_ref, g_ref, gate_ref, fw_ref, o_ref, buf, sem, *, tq, t, final):
    base = pl.program_id(0) * tq

    def issue(j, _):
        for k in range(TOP_K):
            pltpu.make_async_copy(yrows_ref.at[pl.ds(dest_ref[k * t + base + j], 1)],
                                  buf.at[k, pl.ds(j, 1)], sem).start()
        return 0

    def drain(j, _):
        for k in range(TOP_K):
            pltpu.make_async_copy(yrows_ref.at[pl.ds(dest_ref[k * t + base + j], 1)],
                                  buf.at[k, pl.ds(j, 1)], sem).wait()
        return 0

    lax.fori_loop(0, tq, issue, 0)
    lax.fori_loop(0, tq, drain, 0)
    g = g_ref[...]
    f = g[:, 0:1] * buf[0] + g[:, 1:2] * buf[1]
    x2 = x_ref[...] + gate_ref[0] * f
    if final:
        x2 = x2 * lax.rsqrt(jnp.mean(x2 * x2, axis=-1, keepdims=True) + EPS) * fw_ref[...]
    o_ref[...] = x2


def _combine(dest_flat, y_rows, x1, gates_t, ada3, fw, layer, bsz, seqlen, final):
    t, d = x1.shape
    tq = 256
    per_b = seqlen // tq
    return pl.pallas_call(
        functools.partial(_combine_kernel, tq=tq, t=t, final=final),
        grid_spec=pltpu.PrefetchScalarGridSpec(
            num_scalar_prefetch=1, grid=(t // tq,),
            in_specs=[pl.BlockSpec(memory_space=pl.ANY),
                      pl.BlockSpec((tq, d), lambda i, dr: (i, 0)),
                      pl.BlockSpec((tq, TOP_K), lambda i, dr: (i, 0)),
                      pl.BlockSpec((1, 1, d), lambda i, dr: (layer * bsz + i // per_b, 0, 5)),
                      pl.BlockSpec((1, d), lambda i, dr: (0, 0))],
            out_specs=pl.BlockSpec((tq, d), lambda i, dr: (i, 0)),
            scratch_shapes=[pltpu.VMEM((TOP_K, tq, d), F32), pltpu.SemaphoreType.DMA]),
        out_shape=jax.ShapeDtypeStruct((t, d), F32),
        compiler_params=_params("arbitrary"),
        name="combine",
    )(dest_flat, y_rows, x1, gates_t, ada3, fw)


def kernel(x, c, w_ada, b_ada, norm1_w, w_in, conv_w, conv_b, dt_bias, a_log, d_ssd, ssd_norm_w,
           s5_lam_re, s5_lam_im, s5_log_dt, s5_b_re, s5_b_im, s5_c_re, s5_c_im, s5_d, w_glu, b_glu,
           s5_norm_w, w_out, norm2_w, w_rg, b_rg, w_re, b_re, w_eg, w_eu, w_ed, final_norm_w):
    bsz, seqlen, d = x.shape
    t = bsz * seqlen
    depth = w_in.shape[0]
    heads = dt_bias.shape[1]
    inner = ssd_norm_w.shape[1]
    cch = conv_w.shape[2]
    n5 = s5_d.shape[1]
    s5_groups, _, s5_h = s5_b_re.shape[1:]
    n_experts = w_re.shape[2]
    moe_block = 256
    n_rows = t * TOP_K
    n_pad = n_rows + n_experts * moe_block
    n_blocks = n_pad // moe_block

    i0, i1, i2 = inner, inner + cch, inner + cch + heads
    pad_dt = jnp.zeros((depth, d, LANES - heads), F32)
    w_cat = jnp.concatenate([w_in[..., :i1], w_in[..., i2:], w_in[..., i1:i2], pad_dt], axis=-1).astype(BF16)
    lane_pad = lambda a: jnp.pad(a, ((0, 0), (0, LANES - a.shape[1])))[:, None, :]
    dtb, alog = lane_pad(dt_bias), lane_pad(a_log)
    dexp = jnp.repeat(d_ssd, inner // heads, axis=1)[:, None, :]
    li = jnp.arange(SSD_CHUNK)
    tri = (li[None, :] <= li[:, None]).astype(BF16)
    e_mat = (jnp.arange(LANES)[:, None] == (jnp.arange(inner) // (inner // heads))[None, :]).astype(BF16)
    w_out_a = w_out[:, :inner].astype(BF16)
    w_out_b = w_out[:, inner:].astype(BF16)
    w_glu_b = w_glu.astype(BF16)
    wr = jnp.concatenate([w_re, w_rg, jnp.zeros((depth, d, LANES - n_experts - N_EXPERT_GROUPS), F32)], axis=-1)
    wr_hi = wr.astype(BF16)
    wr_lo = (wr - wr_hi.astype(F32)).astype(BF16)
    br = jnp.concatenate([b_re, b_rg, jnp.zeros((depth, LANES - n_experts - N_EXPERT_GROUPS), F32)], axis=-1)[:, None, :]
    w_eg_b, w_eu_b, w_ed_b = w_eg.astype(BF16), w_eu.astype(BF16), w_ed.astype(BF16)
    sub = 256
    si = jnp.arange(sub)
    upper = (si[:, None] < si[None, :]).astype(BF16)

    ada3 = _ada(c, w_ada, b_ada).reshape(depth * bsz, 1, 6 * d)
    x2 = x.reshape(t, d)
    nc5 = seqlen // S5_CHUNK
    for l in range(depth):
        z, xbc, u5, dtp = _inproj(x2, norm1_w[l][None], ada3, w_cat[l], l, bsz, seqlen, inner, cch, n5)
        y_ssd = _ssd(z, xbc, dtp, conv_w[l], conv_b[l][None], dtb[l], alog[l], dexp[l], ssd_norm_w[l][None],
                     tri, e_mat, bsz, seqlen, heads)
        s5w = _s5_weights(s5_lam_re[l], s5_lam_im[l], s5_log_dt[l], s5_b_re[l], s5_b_im[l], s5_c_re[l], s5_c_im[l])
        u_fold = (u5.reshape(bsz, nc5, S5_CHUNK, s5_groups, s5_h).transpose(3, 1, 0, 2, 4)
                  .reshape(s5_groups, nc5 * bsz, S5_CHUNK * s5_h).astype(BF16))
        y_fold = _s5_core(u_fold, *s5w, bsz)
        y5 = (y_fold.reshape(s5_groups, nc5, bsz, S5_CHUNK, s5_h).transpose(2, 1, 3, 0, 4).reshape(t, n5))
        x1, h2, logits_t = _outproj(x2, y_ssd, y5, u5, s5_d[l][None], w_glu_b[l], b_glu[l][None],
                                    s5_norm_w[l][None], w_out_a[l], w_out_b[l], norm2_w[l][None], ada3,
                                    wr_hi[l], wr_lo[l], br[l], l, bsz, seqlen)
        dest, gates, counts = _route(logits_t, upper, n_experts, moe_block)
        padded = (counts[:, 0] + moe_block - 1) // moe_block
        ends = jnp.cumsum(padded)
        block_expert = jnp.minimum(jnp.searchsorted(ends, jnp.arange(n_blocks, dtype=jnp.int32), side='right'),
                                   n_experts - 1).astype(jnp.int32)
        dest_flat = dest[1].reshape(-1)
        rows = _dispatch(dest_flat, h2, n_pad)
        y_rows = _ffn(block_expert, ends[-1:].astype(jnp.int32), rows, w_eg_b[l], w_eu_b[l], w_ed_b[l], moe_block)
        x2 = _combine(dest_flat, y_rows, x1, gates[1].T, ada3, final_norm_w[None], l, bsz, seqlen, l == depth - 1)
    return x2.reshape(bsz, seqlen, d)
```

```python
import functools
import math

import jax
import jax.numpy as jnp
from jax import lax
from jax.experimental import pallas as pl
from jax.experimental.pallas import tpu as pltpu

F32 = jnp.float32
BF16 = jnp.bfloat16
EPS = 1e-6

SSD_GROUPS = 2
SSD_STATE = 128
SSD_CHUNK = 128
S5_CHUNK = 16
N_EXPERT_GROUPS = 4
TOP_K = 2
LANES = 128
ROUTER_ROWS = 40
VMEM_LIMIT = 48 * 1024 * 1024


def _dot(a, b):
    return jnp.dot(a, b, preferred_element_type=F32)


def _split2(a):
    hi = a.astype(BF16)
    lo = (a - hi.astype(F32)).astype(BF16)
    return hi, lo


def _split3(a):
    hi = a.astype(BF16)
    r = a - hi.astype(F32)
    mid = r.astype(BF16)
    lo = (r - mid.astype(F32)).astype(BF16)
    return hi, mid, lo


def _params(*sem):
    return pltpu.CompilerParams(dimension_semantics=sem, vmem_limit_bytes=VMEM_LIMIT)


def _ada_kernel(c_ref, w_ref, b_ref, o_ref):
    ca = jax.nn.silu(c_ref[...])
    c_hi, c_lo = _split2(ca)
    w_hi, w_lo = _split2(w_ref[0])
    o_ref[0] = _dot(c_hi, w_hi) + _dot(c_lo, w_hi) + _dot(c_hi, w_lo) + b_ref[0]


def _ada(c, w_ada, b_ada):
    n_layers, d, n_out = w_ada.shape
    bsz = c.shape[0]
    tn = 1536
    return pl.pallas_call(
        _ada_kernel,
        grid=(n_layers, n_out // tn),
        in_specs=[pl.BlockSpec((bsz, d), lambda l, j: (0, 0)),
                  pl.BlockSpec((1, d, tn), lambda l, j: (l, 0, j)),
                  pl.BlockSpec((1, 1, tn), lambda l, j: (l, 0, j))],
        out_specs=pl.BlockSpec((1, bsz, tn), lambda l, j: (l, 0, j)),
        out_shape=jax.ShapeDtypeStruct((n_layers, bsz, n_out), F32),
        compiler_params=_params("parallel", "parallel"),
        name="ada",
    )(c, w_ada, b_ada.reshape(n_layers, 1, n_out))


def _inproj_kernel(x_ref, nw_ref, shift_ref, scale_ref, w_ref, z_ref, xbc_ref, u_ref, dt_ref,
                   *, n_z, n_xbc, n_u):
    x = x_ref[...]
    h = x * lax.rsqrt(jnp.mean(x * x, axis=-1, keepdims=True) + EPS) * nw_ref[...]
    h = h * (1.0 + scale_ref[0]) + shift_ref[0]
    p = _dot(h.astype(BF16), w_ref[...])
    z_ref[...] = p[:, :n_z]
    xbc_ref[...] = p[:, n_z:n_z + n_xbc]
    u_ref[...] = p[:, n_z + n_xbc:n_z + n_xbc + n_u]
    dt_ref[...] = p[:, n_z + n_xbc + n_u:]


def _inproj(x2, nw, ada3, w_cat, layer, bsz, seqlen, n_z, n_xbc, n_u):
    t, d = x2.shape
    tm = min(512, seqlen)
    per_b = seqlen // tm
    row = lambda i: (i, 0)
    ada_blk = lambda k: pl.BlockSpec((1, 1, d), lambda i: (layer * bsz + i // per_b, 0, k))
    return pl.pallas_call(
        functools.partial(_inproj_kernel, n_z=n_z, n_xbc=n_xbc, n_u=n_u),
        grid=(t // tm,),
        in_specs=[pl.BlockSpec((tm, d), row),
                  pl.BlockSpec((1, d), lambda i: (0, 0)),
                  ada_blk(0), ada_blk(1),
                  pl.BlockSpec(w_cat.shape, lambda i: (0, 0))],
        out_specs=[pl.BlockSpec((tm, n_z), row), pl.BlockSpec((tm, n_xbc), row),
                   pl.BlockSpec((tm, n_u), row), pl.BlockSpec((tm, LANES), row)],
        out_shape=[jax.ShapeDtypeStruct((t, n_z), F32), jax.ShapeDtypeStruct((t, n_xbc), F32),
                   jax.ShapeDtypeStruct((t, n_u), F32), jax.ShapeDtypeStruct((t, LANES), F32)],
        compiler_params=_params("parallel"),
        name="inproj",
    )(x2, nw, ada3, ada3, w_cat)


def _ssd_kernel(z_ref, xbc_ref, dt_ref, cw_ref, cb_ref, dtb_ref, alog_ref, dexp_ref, nw_ref,
                tri_ref, e_ref, y_ref, ext_ref, st_ref, *, inner, heads, n_conv):
    L = SSD_CHUNK
    N = SSD_STATE
    hd = inner // heads
    hpg = heads // SSD_GROUPS
    gw = inner // SSD_GROUPS
    tail = 8

    @pl.when(pl.program_id(1) == 0)
    def _():
        ext_ref[0:tail, :] = jnp.zeros((tail, ext_ref.shape[1]), F32)
        st_ref[...] = jnp.zeros(st_ref.shape, F32)

    ext_ref[tail:tail + L, :] = xbc_ref[...]
    conv = cb_ref[...]
    for k in range(n_conv):
        conv = conv + cw_ref[k:k + 1, :] * ext_ref[pl.ds(tail - (n_conv - 1) + k, L), :]
    ext_ref[0:tail, :] = ext_ref[L:L + tail, :]
    xa = jax.nn.silu(conv)
    xs = xa[:, :inner]
    bm = xa[:, inner:inner + SSD_GROUPS * N]
    cm = xa[:, inner + SSD_GROUPS * N:]

    pre = dt_ref[...] + dtb_ref[...]
    dt = jnp.maximum(pre, 0.0) + jnp.log1p(jnp.exp(-jnp.abs(pre)))
    ad = dt * (-jnp.exp(alog_ref[...]))
    tri = tri_ref[...]
    a1, a2, a3 = _split3(ad)
    acs = _dot(tri, a1) + _dot(tri, a2) + _dot(tri, a3)
    acs_t = acs.T
    dt_t = dt.T
    w = dt * jnp.exp(acs[L - 1:L, :] - acs)
    eacs = jnp.exp(acs)
    e_mat = e_ref[...]
    w1, w2 = _split2(w)
    w_exp = _dot(w1, e_mat) + _dot(w2, e_mat)
    q1, q2 = _split2(eacs)
    eacs_exp = _dot(q1, e_mat) + _dot(q2, e_mat)

    row = lax.broadcasted_iota(jnp.int32, (L, L), 0)
    col = lax.broadcasted_iota(jnp.int32, (L, L), 1)
    causal = row >= col
    lane = lax.broadcasted_iota(jnp.int32, (L, 2 * hd), 1)
    h_prev = st_ref[...]

    y_diag, y_off, st_new = [], [], []
    for g in range(SSD_GROUPS):
        bg = bm[:, g * N:(g + 1) * N]
        cg = cm[:, g * N:(g + 1) * N].astype(BF16)
        cb = lax.dot_general(cg, bg.astype(BF16), (((1,), (1,)), ((), ())), preferred_element_type=F32)
        for j in range(hpg // 2):
            h0 = g * hpg + 2 * j
            lms = []
            for h in (h0, h0 + 1):
                seg = jnp.where(causal, acs[:, h:h + 1] - acs_t[h:h + 1, :], -1e30)
                lms.append((cb * jnp.exp(seg) * dt_t[h:h + 1, :]).astype(BF16))
            pair = xs[:, h0 * hd:(h0 + 2) * hd]
            w_bd = jnp.concatenate([jnp.where(lane < hd, pair, 0.0), jnp.where(lane >= hd, pair, 0.0)],
                                   axis=0).astype(BF16)
            y_diag.append(_dot(jnp.concatenate(lms, axis=1), w_bd))
        sl = slice(g * gw, (g + 1) * gw)
        y_off.append(_dot(cg, h_prev[:, sl].astype(BF16)))
        st_new.append(_dot(bg.T.astype(BF16), (xs[:, sl] * w_exp[:, sl]).astype(BF16)))

    y = jnp.concatenate(y_diag, axis=1) + jnp.concatenate(y_off, axis=1) * eacs_exp + dexp_ref[...] * xs
    st_ref[...] = h_prev * eacs_exp[L - 1:L, :] + jnp.concatenate(st_new, axis=1)
    y = y * jax.nn.silu(z_ref[...])
    outs = []
    for g in range(SSD_GROUPS):
        yg = y[:, g * gw:(g + 1) * gw]
        outs.append(yg * lax.rsqrt(jnp.mean(yg * yg, axis=-1, keepdims=True) + EPS))
    y_ref[...] = (jnp.concatenate(outs, axis=1) * nw_ref[...]).astype(y_ref.dtype)


def _ssd(z, xbc, dtp, conv_w, conv_b, dtb, alog, dexp, nw, tri, e_mat, bsz, seqlen, heads):
    t, inner = z.shape
    cch = xbc.shape[1]
    L = SSD_CHUNK
    nc = seqlen // L
    n_conv = conv_w.shape[0]
    row = lambda b, c: (b * nc + c, 0)
    const = lambda shape: pl.BlockSpec(shape, lambda b, c: (0, 0))
    return pl.pallas_call(
        functools.partial(_ssd_kernel, inner=inner, heads=heads, n_conv=n_conv),
        grid=(bsz, nc),
        in_specs=[pl.BlockSpec((L, inner), row), pl.BlockSpec((L, cch), row), pl.BlockSpec((L, LANES), row),
                  const((n_conv, cch)), const((1, cch)), const((1, LANES)), const((1, LANES)),
                  const((1, inner)), const((1, inner)), const((L, L)), const((LANES, inner))],
        out_specs=pl.BlockSpec((L, inner), row),
        out_shape=jax.ShapeDtypeStruct((t, inner), BF16),
        scratch_shapes=[pltpu.VMEM((L + 8, cch), F32), pltpu.VMEM((SSD_STATE, inner), F32)],
        compiler_params=_params("parallel", "arbitrary"),
        name="ssd",
    )(z, xbc, dtp, conv_w, conv_b, dtb, alog, dexp, nw, tri, e_mat)


def _s5_kernel(u_ref, toep_ref, wsr_ref, wsi_ref, wor_ref, woi_ref, ar_ref, ai_ref, y_ref,
               sre, sim, pre, pim, *, nb, nc):
    u = u_ref[0]
    sre[...] = _dot(u, wsr_ref[0])
    sim[...] = _dot(u, wsi_ref[0])
    p = sre.shape[1]
    ar = jnp.broadcast_to(ar_ref[0], (nb, p))
    ai = jnp.broadcast_to(ai_ref[0], (nb, p))

    def body(c, carry):
        r, i = carry
        idx = pl.ds(pl.multiple_of(c * nb, nb), nb)
        pre[idx, :] = r
        pim[idx, :] = i
        return ar * r - ai * i + sre[idx, :], ar * i + ai * r + sim[idx, :]

    zero = jnp.zeros((nb, p), F32)
    lax.fori_loop(0, nc, body, (zero, zero))
    y_ref[0] = (_dot(u, toep_ref[0]) + _dot(pre[...].astype(BF16), wor_ref[0])
                + _dot(pim[...].astype(BF16), woi_ref[0]))


def _s5_core(u_fold, toep, wsr, wsi, wor, woi, a_re, a_im, bsz):
    g, m, k = u_fold.shape
    p = wsr.shape[2]
    nc = m // bsz
    blk = lambda a: pl.BlockSpec((1,) + a.shape[1:], lambda i: (i, 0, 0))
    return pl.pallas_call(
        functools.partial(_s5_kernel, nb=bsz, nc=nc),
        grid=(g,),
        in_specs=[blk(u_fold), blk(toep), blk(wsr), blk(wsi), blk(wor), blk(woi), blk(a_re), blk(a_im)],
        out_specs=pl.BlockSpec((1, m, k), lambda i: (i, 0, 0)),
        out_shape=jax.ShapeDtypeStruct((g, m, k), F32),
        scratch_shapes=[pltpu.VMEM((m, p), F32)] * 4,
        compiler_params=_params("parallel"),
        name="s5_core",
    )(u_fold, toep, wsr, wsi, wor, woi, a_re, a_im)


def _s5_weights(lam_re, lam_im, log_dt, b_re, b_im, c_re, c_im):
    lc = S5_CHUNK
    g, p, h = b_re.shape
    step = jnp.exp(log_dt)[:, None]
    mag = jnp.exp(lam_re * step)
    ab_re, ab_im = mag * jnp.cos(lam_im * step), mag * jnp.sin(lam_im * step)
    den = lam_re * lam_re + lam_im * lam_im
    nr = ab_re - 1.0
    coef_re = (nr * lam_re + ab_im * lam_im) / den
    coef_im = (ab_im * lam_re - nr * lam_im) / den
    bb_re = coef_re[..., None] * b_re - coef_im[..., None] * b_im
    bb_im = coef_re[..., None] * b_im + coef_im[..., None] * b_re
    jj = jnp.arange(lc + 1, dtype=F32)[:, None, None]
    pmag = jnp.exp(jj * (lam_re * step)[None])
    pw_re = pmag * jnp.cos(jj * (lam_im * step)[None])
    pw_im = pmag * jnp.sin(jj * (lam_im * step)[None])
    cp_re = c_re[None] * pw_re[:, :, None, :] - c_im[None] * pw_im[:, :, None, :]
    cp_im = c_re[None] * pw_im[:, :, None, :] + c_im[None] * pw_re[:, :, None, :]
    kern = (jnp.einsum('jghp,gpk->jghk', cp_re[:lc], bb_re, precision='highest')
            - jnp.einsum('jghp,gpk->jghk', cp_im[:lc], bb_im, precision='highest'))
    tt = jnp.arange(lc)
    lag = tt[None, :] - tt[:, None]
    toep = jnp.where((lag >= 0)[:, :, None, None, None], kern[jnp.clip(lag, 0, lc - 1)], 0.0)
    toep = toep.transpose(2, 0, 4, 1, 3).reshape(g, lc * h, lc * h)
    rev_re, rev_im = pw_re[lc - 1 - tt], pw_im[lc - 1 - tt]
    ws_re = rev_re[..., None] * bb_re[None] - rev_im[..., None] * bb_im[None]
    ws_im = rev_re[..., None] * bb_im[None] + rev_im[..., None] * bb_re[None]
    ws_re = ws_re.transpose(1, 0, 3, 2).reshape(g, lc * h, p)
    ws_im = ws_im.transpose(1, 0, 3, 2).reshape(g, lc * h, p)
    wo_re = cp_re[1:].transpose(1, 3, 0, 2).reshape(g, p, lc * h)
    wo_im = (-cp_im[1:]).transpose(1, 3, 0, 2).reshape(g, p, lc * h)
    return (toep.astype(BF16), ws_re.astype(BF16), ws_im.astype(BF16), wo_re.astype(BF16),
            wo_im.astype(BF16), pw_re[lc][:, None, :], pw_im[lc][:, None, :])


def _outproj_kernel(x_ref, ys_ref, y5_ref, u5_ref, d5_ref, wg_ref, bg_ref, n5_ref, wa_ref, wb_ref,
                    gate_ref, nw_ref, shift_ref, scale_ref, wrh_ref, wrl_ref, br_ref,
                    x1_ref, h2_ref, lg_ref):
    y = y5_ref[...] + d5_ref[...] * u5_ref[...]
    y = jax.nn.gelu(y)
    y = y * jax.nn.sigmoid(_dot(y.astype(BF16), wg_ref[...]) + bg_ref[...])
    y = y * lax.rsqrt(jnp.mean(y * y, axis=-1, keepdims=True) + EPS) * n5_ref[...]
    m = _dot(ys_ref[...], wa_ref[...]) + _dot(y.astype(BF16), wb_ref[...])
    x1 = x_ref[...] + gate_ref[0] * m
    x1_ref[...] = x1
    h = x1 * lax.rsqrt(jnp.mean(x1 * x1, axis=-1, keepdims=True) + EPS) * nw_ref[...]
    h = h * (1.0 + scale_ref[0]) + shift_ref[0]
    h2_ref[...] = h
    h_hi, h_lo = _split2(h)
    wrh = wrh_ref[...]
    lg = _dot(h_hi, wrh) + _dot(h_lo, wrh) + _dot(h_hi, wrl_ref[...]) + br_ref[...]
    lg_ref[...] = lg.T[:ROUTER_ROWS, :]


def _outproj(x2, y_ssd, y5, u5, d5, w_glu, b_glu, n5, w_a, w_b, nw, ada3, wr_hi, wr_lo, br,
             layer, bsz, seqlen):
    t, d = x2.shape
    n_s = y_ssd.shape[1]
    n_5 = y5.shape[1]
    tm = min(512, seqlen)
    per_b = seqlen // tm
    row = lambda i: (i, 0)
    const = lambda a: pl.BlockSpec(a.shape, lambda i: (0, 0))
    ada_blk = lambda k: pl.BlockSpec((1, 1, d), lambda i: (layer * bsz + i // per_b, 0, k))
    return pl.pallas_call(
        _outproj_kernel,
        grid=(t // tm,),
        in_specs=[pl.BlockSpec((tm, d), row), pl.BlockSpec((tm, n_s), row), pl.BlockSpec((tm, n_5), row),
                  pl.BlockSpec((tm, n_5), row), const(d5), const(w_glu), const(b_glu), const(n5),
                  const(w_a), const(w_b), ada_blk(2), const(nw), ada_blk(3), ada_blk(4),
                  const(wr_hi), const(wr_lo), const(br)],
        out_specs=[pl.BlockSpec((tm, d), row), pl.BlockSpec((tm, d), row),
                   pl.BlockSpec((ROUTER_ROWS, tm), lambda i: (0, i))],
        out_shape=[jax.ShapeDtypeStruct((t, d), F32), jax.ShapeDtypeStruct((t, d), F32),
                   jax.ShapeDtypeStruct((ROUTER_ROWS, t), F32)],
        compiler_params=_params("parallel"),
        name="outproj",
    )(x2, y_ssd, y5, u5, d5, w_glu, b_glu, n5, w_a, w_b, ada3, nw, ada3, ada3, wr_hi, wr_lo, br)


def _route_kernel(lg_ref, upper_ref, dest_ref, gate_ref, cnt_ref, counts, carry, pstart,
                  *, n_experts, per_group, block_rows, sub):
    ph = pl.program_id(0)
    i = pl.program_id(1)
    tr = lg_ref.shape[1]
    lg = lg_ref[...]

    @pl.when((ph == 0) & (i == 0))
    def _():
        counts[...] = jnp.zeros(counts.shape, F32)

    gl = [lg[n_experts + k:n_experts + k + 1, :] for k in range(N_EXPERT_GROUPS)]
    gmax = functools.reduce(jnp.maximum, gl)
    gidx = jnp.full((1, tr), N_EXPERT_GROUPS - 1, jnp.int32)
    for k in range(N_EXPERT_GROUPS - 2, -1, -1):
        gidx = jnp.where(gl[k] == gmax, k, gidx)
    gsum = functools.reduce(lambda a, b: a + b, [jnp.exp(v - gmax) for v in gl])
    g_w = 1.0 / gsum
    el = lg[0:per_group, :]
    for k in range(1, N_EXPERT_GROUPS):
        el = jnp.where(gidx == k, lg[k * per_group:(k + 1) * per_group, :], el)
    ep = jnp.exp(el - jnp.max(el, axis=0, keepdims=True))
    prob = ep / jnp.sum(ep, axis=0, keepdims=True)
    jj = lax.broadcasted_iota(jnp.int32, (per_group, tr), 0).astype(F32)
    p1 = jnp.max(prob, axis=0, keepdims=True)
    i1 = jnp.min(jnp.where(prob == p1, jj, float(per_group)), axis=0, keepdims=True)
    prob2 = jnp.where(jj == i1, -1.0, prob)
    p2 = jnp.max(prob2, axis=0, keepdims=True)
    i2 = jnp.min(jnp.where(prob2 == p2, jj, float(per_group)), axis=0, keepdims=True)
    den = p1 + p2
    gate_ref[0] = jnp.concatenate([g_w * p1 / den, g_w * p2 / den], axis=0)
    e1 = gidx * per_group + i1.astype(jnp.int32)
    e2 = gidx * per_group + i2.astype(jnp.int32)
    rr = lax.broadcasted_iota(jnp.int32, (n_experts, tr), 0)
    oh1 = rr == e1
    oh2 = rr == e2
    member = jnp.where(oh1 | oh2, 1.0, 0.0)

    @pl.when(ph == 0)
    def _():
        counts[...] = counts[...] + jnp.sum(member, axis=1, keepdims=True)
        dest_ref[...] = jnp.zeros(dest_ref.shape, jnp.int32)

    @pl.when(ph == 1)
    def _():
        @pl.when(i == 0)
        def _():
            blocks = (counts[...].astype(jnp.int32) + (block_rows - 1)) >> int(math.log2(block_rows))
            hi = (blocks >> 4).astype(F32).astype(BF16)
            lo = (blocks & 15).astype(F32).astype(BF16)
            er = lax.broadcasted_iota(jnp.int32, (n_experts, n_experts), 0)
            ec = lax.broadcasted_iota(jnp.int32, (n_experts, n_experts), 1)
            lower = jnp.where(ec < er, 1.0, 0.0).astype(BF16)
            pstart[...] = (16.0 * _dot(lower, hi) + _dot(lower, lo)) * float(block_rows)
            carry[...] = jnp.zeros(carry.shape, F32)

        run = carry[...]
        upper = upper_ref[...]
        pieces = []
        for b in range(tr // sub):
            mb = member[:, b * sub:(b + 1) * sub]
            pieces.append(_dot(mb.astype(BF16), upper) + jnp.concatenate([run] * (sub // LANES), axis=1))
            run = run + jnp.sum(mb, axis=1, keepdims=True)
        carry[...] = run
        base = jnp.concatenate(pieces, axis=1) + jnp.concatenate([pstart[...]] * (tr // LANES), axis=1)
        d1 = jnp.sum(jnp.where(oh1, base, 0.0), axis=0, keepdims=True)
        d2 = jnp.sum(jnp.where(oh2, base, 0.0), axis=0, keepdims=True)
        dest_ref[0] = jnp.concatenate([d1, d2], axis=0).astype(jnp.int32)

    cnt_ref[...] = counts[...].astype(jnp.int32)


def _route(logits_t, upper, n_experts, block_rows):
    rows, t = logits_t.shape
    tr = 1024
    sub = upper.shape[0]
    return pl.pallas_call(
        functools.partial(_route_kernel, n_experts=n_experts, per_group=n_experts // N_EXPERT_GROUPS,
                          block_rows=block_rows, sub=sub),
        grid=(2, t // tr),
        in_specs=[pl.BlockSpec((rows, tr), lambda ph, i: (0, i)),
                  pl.BlockSpec(upper.shape, lambda ph, i: (0, 0))],
        out_specs=[pl.BlockSpec((1, TOP_K, tr), lambda ph, i: (ph, 0, i)),
                   pl.BlockSpec((1, TOP_K, tr), lambda ph, i: (ph, 0, i)),
                   pl.BlockSpec((n_experts, LANES), lambda ph, i: (0, 0))],
        out_shape=[jax.ShapeDtypeStruct((2, TOP_K, t), jnp.int32), jax.ShapeDtypeStruct((2, TOP_K, t), F32),
                   jax.ShapeDtypeStruct((n_experts, LANES), jnp.int32)],
        scratch_shapes=[pltpu.VMEM((n_experts, LANES), F32)] * 3,
        compiler_params=_params("arbitrary", "arbitrary"),
        name="route",
    )(logits_t, upper)


DMA_UNROLL = 8


def _dispatch_kernel(dest_ref, h_ref, init_ref, rows_ref, sem, *, tq, t):
    del init_ref
    base = pl.program_id(0) * tq

    def copy(j, k):
        return pltpu.make_async_copy(h_ref.at[pl.ds(j, 1)], rows_ref.at[pl.ds(dest_ref[k * t + base + j], 1)], sem)

    def issue(j, _):
        for k in range(TOP_K):
            copy(j, k).start()
        return 0

    def drain(j, _):
        for k in range(TOP_K):
            copy(j, k).wait()
        return 0

    lax.fori_loop(0, tq, issue, 0, unroll=DMA_UNROLL)
    lax.fori_loop(0, tq, drain, 0, unroll=DMA_UNROLL)


def _dispatch(dest_flat, h2, n_pad):
    t, d = h2.shape
    tq = 256
    init = jnp.zeros((n_pad, d), h2.dtype)
    return pl.pallas_call(
        functools.partial(_dispatch_kernel, tq=tq, t=t),
        grid_spec=pltpu.PrefetchScalarGridSpec(
            num_scalar_prefetch=1, grid=(t // tq,),
            in_specs=[pl.BlockSpec((tq, d), lambda i, dr: (i, 0)), pl.BlockSpec(memory_space=pl.ANY)],
            out_specs=pl.BlockSpec(memory_space=pl.ANY),
            scratch_shapes=[pltpu.SemaphoreType.DMA]),
        out_shape=jax.ShapeDtypeStruct((n_pad, d), h2.dtype),
        input_output_aliases={2: 0},
        compiler_params=_params("arbitrary"),
        name="dispatch",
    )(dest_flat, h2, init)


def _ffn_kernel(be_ref, nused_ref, x_ref, wg_ref, wu_ref, wd_ref, y_ref, wg16, wu16, wd16):
    i = pl.program_id(0)

    @pl.when((i == 0) | (be_ref[i] != be_ref[jnp.maximum(i - 1, 0)]))
    def _():
        wg16[...] = wg_ref[0].astype(BF16)
        wu16[...] = wu_ref[0].astype(BF16)
        wd16[...] = wd_ref[0].astype(BF16)

    @pl.when(i < nused_ref[0])
    def _():
        xb = x_ref[...].astype(BF16)
        a = jax.nn.silu(_dot(xb, wg16[...])) * _dot(xb, wu16[...])
        y_ref[...] = _dot(a.astype(BF16), wd16[...])

    @pl.when(i >= nused_ref[0])
    def _():
        y_ref[...] = jnp.zeros(y_ref.shape, y_ref.dtype)


def _ffn(block_expert, n_used, rows, w_eg, w_eu, w_ed, bm):
    n_pad, d = rows.shape
    ff = w_eg.shape[2]
    return pl.pallas_call(
        _ffn_kernel,
        grid_spec=pltpu.PrefetchScalarGridSpec(
            num_scalar_prefetch=2, grid=(n_pad // bm,),
            in_specs=[pl.BlockSpec((bm, d), lambda i, be, nu: (i, 0)),
                      pl.BlockSpec((1, d, ff), lambda i, be, nu: (be[i], 0, 0)),
                      pl.BlockSpec((1, d, ff), lambda i, be, nu: (be[i], 0, 0)),
                      pl.BlockSpec((1, ff, d), lambda i, be, nu: (be[i], 0, 0))],
            out_specs=pl.BlockSpec((bm, d), lambda i, be, nu: (i, 0)),
            scratch_shapes=[pltpu.VMEM((d, ff), BF16), pltpu.VMEM((d, ff), BF16), pltpu.VMEM((ff, d), BF16)]),
        out_shape=jax.ShapeDtypeStruct((n_pad, d), F32),
        compiler_params=_params("arbitrary"),
        name="ffn",
    )(block_expert, n_used, rows, w_eg, w_eu, w_ed)


def _combine_kernel(dest_ref, yrows_ref, x_ref, g_ref, gate_ref, fw_ref, o_ref, buf, sem, *, tq, t, final):
    base = pl.program_id(0) * tq

    def copy(j, k):
        return pltpu.make_async_copy(yrows_ref.at[pl.ds(dest_ref[k * t + base + j], 1)],
                                     buf.at[k, pl.ds(j, 1)], sem)

    def issue(j, _):
        for k in range(TOP_K):
            copy(j, k).start()
        return 0

    def drain(j, _):
        for k in range(TOP_K):
            copy(j, k).wait()
        return 0

    lax.fori_loop(0, tq, issue, 0, unroll=DMA_UNROLL)
    lax.fori_loop(0, tq, drain, 0, unroll=DMA_UNROLL)
    g = g_ref[...]
    f = g[:, 0:1] * buf[0] + g[:, 1:2] * buf[1]
    x2 = x_ref[...] + gate_ref[0] * f
    if final:
        x2 = x2 * lax.rsqrt(jnp.mean(x2 * x2, axis=-1, keepdims=True) + EPS) * fw_ref[...]
    o_ref[...] = x2


def _combine(dest_flat, y_rows, x1, gates_t, ada3, fw, layer, bsz, seqlen, final):
    t, d = x1.shape
    tq = 256
    per_b = seqlen // tq
    return pl.pallas_call(
        functools.partial(_combine_kernel, tq=tq, t=t, final=final),
        grid_spec=pltpu.PrefetchScalarGridSpec(
            num_scalar_prefetch=1, grid=(t // tq,),
            in_specs=[pl.BlockSpec(memory_space=pl.ANY),
                      pl.BlockSpec((tq, d), lambda i, dr: (i, 0)),
                      pl.BlockSpec((tq, TOP_K), lambda i, dr: (i, 0)),
                      pl.BlockSpec((1, 1, d), lambda i, dr: (layer * bsz + i // per_b, 0, 5)),
                      pl.BlockSpec((1, d), lambda i, dr: (0, 0))],
            out_specs=pl.BlockSpec((tq, d), lambda i, dr: (i, 0)),
            scratch_shapes=[pltpu.VMEM((TOP_K, tq, d), F32), pltpu.SemaphoreType.DMA]),
        out_shape=jax.ShapeDtypeStruct((t, d), F32),
        compiler_params=_params("arbitrary"),
        name="combine",
    )(dest_flat, y_rows, x1, gates_t, ada3, fw)


def kernel(x, c, w_ada, b_ada, norm1_w, w_in, conv_w, conv_b, dt_bias, a_log, d_ssd, ssd_norm_w,
           s5_lam_re, s5_lam_im, s5_log_dt, s5_b_re, s5_b_im, s5_c_re, s5_c_im, s5_d, w_glu, b_glu,
           s5_norm_w, w_out, norm2_w, w_rg, b_rg, w_re, b_re, w_eg, w_eu, w_ed, final_norm_w):
    bsz, seqlen, d = x.shape
    t = bsz * seqlen
    depth = w_in.shape[0]
    heads = dt_bias.shape[1]
    inner = ssd_norm_w.shape[1]
    cch = conv_w.shape[2]
    n5 = s5_d.shape[1]
    s5_groups, _, s5_h = s5_b_re.shape[1:]
    n_experts = w_re.shape[2]
    moe_block = 256
    n_rows = t * TOP_K
    n_pad = n_rows + n_experts * moe_block
    n_blocks = n_pad // moe_block

    i0, i1, i2 = inner, inner + cch, inner + cch + heads
    pad_dt = jnp.zeros((depth, d, LANES - heads), F32)
    w_cat = jnp.concatenate([w_in[..., :i1], w_in[..., i2:], w_in[..., i1:i2], pad_dt], axis=-1).astype(BF16)
    lane_pad = lambda a: jnp.pad(a, ((0, 0), (0, LANES - a.shape[1])))[:, None, :]
    dtb, alog = lane_pad(dt_bias), lane_pad(a_log)
    dexp = jnp.repeat(d_ssd, inner // heads, axis=1)[:, None, :]
    li = jnp.arange(SSD_CHUNK)
    tri = (li[None, :] <= li[:, None]).astype(BF16)
    e_mat = (jnp.arange(LANES)[:, None] == (jnp.arange(inner) // (inner // heads))[None, :]).astype(BF16)
    w_out_a = w_out[:, :inner].astype(BF16)
    w_out_b = w_out[:, inner:].astype(BF16)
    w_glu_b = w_glu.astype(BF16)
    wr = jnp.concatenate([w_re, w_rg, jnp.zeros((depth, d, LANES - n_experts - N_EXPERT_GROUPS), F32)], axis=-1)
    wr_hi = wr.astype(BF16)
    wr_lo = (wr - wr_hi.astype(F32)).astype(BF16)
    br = jnp.concatenate([b_re, b_rg, jnp.zeros((depth, LANES - n_experts - N_EXPERT_GROUPS), F32)], axis=-1)[:, None, :]
    sub = 256
    si = jnp.arange(sub)
    upper = (si[:, None] < si[None, :]).astype(BF16)

    ada3 = _ada(c, w_ada, b_ada).reshape(depth * bsz, 1, 6 * d)
    x2 = x.reshape(t, d)
    nc5 = seqlen // S5_CHUNK
    for l in range(depth):
        z, xbc, u5, dtp = _inproj(x2, norm1_w[l][None], ada3, w_cat[l], l, bsz, seqlen, inner, cch, n5)
        y_ssd = _ssd(z, xbc, dtp, conv_w[l], conv_b[l][None], dtb[l], alog[l], dexp[l], ssd_norm_w[l][None],
                     tri, e_mat, bsz, seqlen, heads)
        s5w = _s5_weights(s5_lam_re[l], s5_lam_im[l], s5_log_dt[l], s5_b_re[l], s5_b_im[l], s5_c_re[l], s5_c_im[l])
        u_fold = (u5.reshape(bsz, nc5, S5_CHUNK, s5_groups, s5_h).transpose(3, 1, 0, 2, 4)
                  .reshape(s5_groups, nc5 * bsz, S5_CHUNK * s5_h).astype(BF16))
        y_fold = _s5_core(u_fold, *s5w, bsz)
        y5 = (y_fold.reshape(s5_groups, nc5, bsz, S5_CHUNK, s5_h).transpose(2, 1, 3, 0, 4).reshape(t, n5))
        x1, h2, logits_t = _outproj(x2, y_ssd, y5, u5, s5_d[l][None], w_glu_b[l], b_glu[l][None],
                                    s5_norm_w[l][None], w_out_a[l], w_out_b[l], norm2_w[l][None], ada3,
                                    wr_hi[l], wr_lo[l], br[l], l, bsz, seqlen)
        dest, gates, counts = _route(logits_t, upper, n_experts, moe_block)
        padded = (counts[:, 0] + moe_block - 1) // moe_block
        ends = jnp.cumsum(padded)
        block_expert = jnp.minimum(jnp.sum(ends[None, :] <= jnp.arange(n_blocks, dtype=jnp.int32)[:, None], axis=1),
                                   n_experts - 1).astype(jnp.int32)
        dest_flat = dest[1].reshape(-1)
        rows = _dispatch(dest_flat, h2, n_pad)
        y_rows = _ffn(block_expert, ends[-1:].astype(jnp.int32), rows, w_eg[l], w_eu[l], w_ed[l], moe_block)
        x2 = _combine(dest_flat, y_rows, x1, gates[1].T, ada3, final_norm_w[None], l, bsz, seqlen, l == depth - 1)
    return x2.reshape(bsz, seqlen, d)
```

```python
import functools
import math

import jax
import jax.numpy as jnp
from jax import lax
from jax.experimental import pallas as pl
from jax.experimental.pallas import tpu as pltpu

F32 = jnp.float32
BF16 = jnp.bfloat16
EPS = 1e-6

SSD_GROUPS = 2
SSD_STATE = 128
SSD_CHUNK = 128
S5_CHUNK = 16
N_EXPERT_GROUPS = 4
TOP_K = 2
LANES = 128
ROUTER_ROWS = 40
VMEM_LIMIT = 48 * 1024 * 1024


def _dot(a, b):
    return jnp.dot(a, b, preferred_element_type=F32)


def _split2(a):
    hi = a.astype(BF16)
    lo = (a - hi.astype(F32)).astype(BF16)
    return hi, lo


def _split3(a):
    hi = a.astype(BF16)
    r = a - hi.astype(F32)
    mid = r.astype(BF16)
    lo = (r - mid.astype(F32)).astype(BF16)
    return hi, mid, lo


def _params(*sem):
    return pltpu.CompilerParams(dimension_semantics=sem, vmem_limit_bytes=VMEM_LIMIT)


def _ada_kernel(c_ref, w_ref, b_ref, o_ref):
    ca = jax.nn.silu(c_ref[...])
    c_hi, c_lo = _split2(ca)
    w_hi, w_lo = _split2(w_ref[0])
    o_ref[0] = _dot(c_hi, w_hi) + _dot(c_lo, w_hi) + _dot(c_hi, w_lo) + b_ref[0]


def _ada(c, w_ada, b_ada):
    n_layers, d, n_out = w_ada.shape
    bsz = c.shape[0]
    tn = 1536
    return pl.pallas_call(
        _ada_kernel,
        grid=(n_layers, n_out // tn),
        in_specs=[pl.BlockSpec((bsz, d), lambda l, j: (0, 0)),
                  pl.BlockSpec((1, d, tn), lambda l, j: (l, 0, j)),
                  pl.BlockSpec((1, 1, tn), lambda l, j: (l, 0, j))],
        out_specs=pl.BlockSpec((1, bsz, tn), lambda l, j: (l, 0, j)),
        out_shape=jax.ShapeDtypeStruct((n_layers, bsz, n_out), F32),
        compiler_params=_params("parallel", "parallel"),
        name="ada",
    )(c, w_ada, b_ada.reshape(n_layers, 1, n_out))


def _inproj_kernel(x_ref, nw_ref, shift_ref, scale_ref, wzx_ref, wu_ref, wdt_ref,
                   z_ref, xbc_ref, u_ref, dt_ref, *, n_z):
    x = x_ref[...]
    h = x * lax.rsqrt(jnp.mean(x * x, axis=-1, keepdims=True) + EPS) * nw_ref[...]
    h = (h * (1.0 + scale_ref[0]) + shift_ref[0]).astype(BF16)
    p = _dot(h, wzx_ref[...])
    z_ref[...] = p[:, :n_z]
    xbc_ref[...] = p[:, n_z:]
    u_ref[...] = _dot(h, wu_ref[...])
    dt_ref[...] = _dot(h, wdt_ref[...])


def _inproj(x2, nw, ada3, w_zx, w_u, w_dt, layer, bsz, seqlen, n_z):
    t, d = x2.shape
    n_xbc = w_zx.shape[1] - n_z
    n_u = w_u.shape[1]
    tm = min(512, seqlen)
    per_b = seqlen // tm
    row = lambda i: (i, 0)
    const = lambda a: pl.BlockSpec(a.shape, lambda i: (0, 0))
    ada_blk = lambda k: pl.BlockSpec((1, 1, d), lambda i: (layer * bsz + i // per_b, 0, k))
    return pl.pallas_call(
        functools.partial(_inproj_kernel, n_z=n_z),
        grid=(t // tm,),
        in_specs=[pl.BlockSpec((tm, d), row),
                  pl.BlockSpec((1, d), lambda i: (0, 0)),
                  ada_blk(0), ada_blk(1), const(w_zx), const(w_u), const(w_dt)],
        out_specs=[pl.BlockSpec((tm, n_z), row), pl.BlockSpec((tm, n_xbc), row),
                   pl.BlockSpec((tm, n_u), row), pl.BlockSpec((tm, LANES), row)],
        out_shape=[jax.ShapeDtypeStruct((t, n_z), F32), jax.ShapeDtypeStruct((t, n_xbc), F32),
                   jax.ShapeDtypeStruct((t, n_u), F32), jax.ShapeDtypeStruct((t, LANES), F32)],
        compiler_params=_params("parallel"),
        name="inproj",
    )(x2, nw, ada3, ada3, w_zx, w_u, w_dt)


def _ssd_kernel(z_ref, xbc_ref, dt_ref, cw_ref, cb_ref, dtb_ref, alog_ref, dexp_ref, nw_ref,
                tri_ref, e_ref, y_ref, ext_ref, st_ref, *, inner, heads, n_conv):
    L = SSD_CHUNK
    N = SSD_STATE
    hd = inner // heads
    hpg = heads // SSD_GROUPS
    gw = inner // SSD_GROUPS
    tail = 8

    @pl.when(pl.program_id(1) == 0)
    def _():
        ext_ref[0:tail, :] = jnp.zeros((tail, ext_ref.shape[1]), F32)
        st_ref[...] = jnp.zeros(st_ref.shape, F32)

    ext_ref[tail:tail + L, :] = xbc_ref[...]
    conv = cb_ref[...]
    for k in range(n_conv):
        conv = conv + cw_ref[k:k + 1, :] * ext_ref[pl.ds(tail - (n_conv - 1) + k, L), :]
    ext_ref[0:tail, :] = ext_ref[L:L + tail, :]
    xa = jax.nn.silu(conv)
    xs = xa[:, :inner]
    bm = xa[:, inner:inner + SSD_GROUPS * N]
    cm = xa[:, inner + SSD_GROUPS * N:]

    pre = dt_ref[...] + dtb_ref[...]
    dt = jnp.maximum(pre, 0.0) + jnp.log1p(jnp.exp(-jnp.abs(pre)))
    ad = dt * (-jnp.exp(alog_ref[...]))
    tri = tri_ref[...]
    a1, a2, a3 = _split3(ad)
    acs = _dot(tri, a1) + _dot(tri, a2) + _dot(tri, a3)
    acs_t = acs.T
    dt_t = dt.T
    w = dt * jnp.exp(acs[L - 1:L, :] - acs)
    eacs = jnp.exp(acs)
    e_mat = e_ref[...]
    w1, w2 = _split2(w)
    w_exp = _dot(w1, e_mat) + _dot(w2, e_mat)
    q1, q2 = _split2(eacs)
    eacs_exp = _dot(q1, e_mat) + _dot(q2, e_mat)

    row = lax.broadcasted_iota(jnp.int32, (L, L), 0)
    col = lax.broadcasted_iota(jnp.int32, (L, L), 1)
    causal = row >= col
    lane = lax.broadcasted_iota(jnp.int32, (L, 2 * hd), 1)
    h_prev = st_ref[...]

    y_diag, y_off, st_new = [], [], []
    for g in range(SSD_GROUPS):
        bg = bm[:, g * N:(g + 1) * N]
        cg = cm[:, g * N:(g + 1) * N].astype(BF16)
        cb = lax.dot_general(cg, bg.astype(BF16), (((1,), (1,)), ((), ())), preferred_element_type=F32)
        for j in range(hpg // 2):
            h0 = g * hpg + 2 * j
            lms = []
            for h in (h0, h0 + 1):
                seg = jnp.where(causal, acs[:, h:h + 1] - acs_t[h:h + 1, :], -1e30)
                lms.append((cb * jnp.exp(seg) * dt_t[h:h + 1, :]).astype(BF16))
            pair = xs[:, h0 * hd:(h0 + 2) * hd]
            w_bd = jnp.concatenate([jnp.where(lane < hd, pair, 0.0), jnp.where(lane >= hd, pair, 0.0)],
                                   axis=0).astype(BF16)
            y_diag.append(_dot(jnp.concatenate(lms, axis=1), w_bd))
        sl = slice(g * gw, (g + 1) * gw)
        y_off.append(_dot(cg, h_prev[:, sl].astype(BF16)))
        st_new.append(_dot(bg.T.astype(BF16), (xs[:, sl] * w_exp[:, sl]).astype(BF16)))

    y = jnp.concatenate(y_diag, axis=1) + jnp.concatenate(y_off, axis=1) * eacs_exp + dexp_ref[...] * xs
    st_ref[...] = h_prev * eacs_exp[L - 1:L, :] + jnp.concatenate(st_new, axis=1)
    y = y * jax.nn.silu(z_ref[...])
    outs = []
    for g in range(SSD_GROUPS):
        yg = y[:, g * gw:(g + 1) * gw]
        outs.append(yg * lax.rsqrt(jnp.mean(yg * yg, axis=-1, keepdims=True) + EPS))
    y_ref[...] = (jnp.concatenate(outs, axis=1) * nw_ref[...]).astype(y_ref.dtype)


def _ssd(z, xbc, dtp, conv_w, conv_b, dtb, alog, dexp, nw, tri, e_mat, bsz, seqlen, heads):
    t, inner = z.shape
    cch = xbc.shape[1]
    L = SSD_CHUNK
    nc = seqlen // L
    n_conv = conv_w.shape[0]
    row = lambda b, c: (b * nc + c, 0)
    const = lambda shape: pl.BlockSpec(shape, lambda b, c: (0, 0))
    return pl.pallas_call(
        functools.partial(_ssd_kernel, inner=inner, heads=heads, n_conv=n_conv),
        grid=(bsz, nc),
        in_specs=[pl.BlockSpec((L, inner), row), pl.BlockSpec((L, cch), row), pl.BlockSpec((L, LANES), row),
                  const((n_conv, cch)), const((1, cch)), const((1, LANES)), const((1, LANES)),
                  const((1, inner)), const((1, inner)), const((L, L)), const((LANES, inner))],
        out_specs=pl.BlockSpec((L, inner), row),
        out_shape=jax.ShapeDtypeStruct((t, inner), BF16),
        scratch_shapes=[pltpu.VMEM((L + 8, cch), F32), pltpu.VMEM((SSD_STATE, inner), F32)],
        compiler_params=_params("parallel", "arbitrary"),
        name="ssd",
    )(z, xbc, dtp, conv_w, conv_b, dtb, alog, dexp, nw, tri, e_mat)


def _s5_kernel(*refs, groups, gch):
    n_t = groups * gch // LANES
    u_refs = refs[:n_t]
    toep_ref, wsr_ref, wsi_ref, wor_ref, woi_ref, ar_ref, ai_ref = refs[n_t:n_t + 7]
    y_refs = refs[n_t + 7:2 * n_t + 7]
    fold, yfold, sre, sim, pre, pim = refs[2 * n_t + 7:]
    lc = S5_CHUNK
    nc = u_refs[0].shape[0] // lc
    per_tile = LANES // gch
    n_tiles = groups // per_tile
    halves = lc * gch // LANES
    w2 = halves * LANES
    rb = 16
    piece = lax.broadcasted_iota(jnp.int32, (rb, LANES), 1) // gch

    def merge(srcs, shift_of):
        acc = None
        for q, src in enumerate(srcs):
            sh = shift_of(q) * gch
            r = pltpu.roll(src, sh, axis=1) if sh else src
            acc = r if acc is None else jnp.where(piece == q, r, acc)
        return acc

    def fold_rows(b, _):
        r0 = pl.multiple_of(b * rb, rb)
        for j in range(n_tiles):
            tiles = [u_refs[j][pl.ds(b * (rb * lc) + t, rb, stride=lc), :] for t in range(lc)]
            for gp in range(per_tile):
                g = j * per_tile + gp
                parts = [merge(tiles[hv * per_tile:(hv + 1) * per_tile], lambda tt: (tt - gp) % per_tile)
                         for hv in range(halves)]
                fold[g // 2, pl.ds(r0, rb), (g % 2) * w2:(g % 2 + 1) * w2] = (
                    jnp.concatenate(parts, axis=1).astype(BF16))
        return 0

    lax.fori_loop(0, nc // rb, fold_rows, 0)

    n_pairs = groups // 2
    for k in range(n_pairs):
        sre[:, k * LANES:(k + 1) * LANES] = _dot(fold[k], wsr_ref[k])
        sim[:, k * LANES:(k + 1) * LANES] = _dot(fold[k], wsi_ref[k])

    ar = ar_ref[...]
    ai = ai_ref[...]

    def body(c, carry):
        r, i = carry
        pre[pl.ds(c, 1), :] = r
        pim[pl.ds(c, 1), :] = i
        return ar * r - ai * i + sre[pl.ds(c, 1), :], ar * i + ai * r + sim[pl.ds(c, 1), :]

    zero = jnp.zeros((1, sre.shape[1]), F32)
    lax.fori_loop(0, nc, body, (zero, zero))

    for k in range(n_pairs):
        a = fold[k]
        intra = jnp.concatenate([_dot(a[:, :w2], toep_ref[2 * k]), _dot(a[:, w2:], toep_ref[2 * k + 1])], axis=1)
        yfold[k] = (intra + _dot(pre[:, k * LANES:(k + 1) * LANES].astype(BF16), wor_ref[k])
                    + _dot(pim[:, k * LANES:(k + 1) * LANES].astype(BF16), woi_ref[k]))

    def unfold_rows(b, _):
        r0 = pl.multiple_of(b * rb, rb)
        for j in range(n_tiles):
            for hv in range(halves):
                srcs = []
                for gp in range(per_tile):
                    g = j * per_tile + gp
                    srcs.append(yfold[g // 2, pl.ds(r0, rb), pl.ds((g % 2) * w2 + hv * LANES, LANES)])
                for tt in range(per_tile):
                    y_refs[j][pl.ds(b * (rb * lc) + hv * per_tile + tt, rb, stride=lc), :] = merge(
                        srcs, lambda gp: (gp - tt) % per_tile)
        return 0

    lax.fori_loop(0, nc // rb, unfold_rows, 0)


def _s5_core(u5, toep, wsr, wsi, wor, woi, a_re, a_im, bsz, seqlen, gch):
    t, n5 = u5.shape
    groups = n5 // gch
    nc = seqlen // S5_CHUNK
    n_t = n5 // LANES
    const = lambda a: pl.BlockSpec(a.shape, lambda b: (0,) * a.ndim, pipeline_mode=pl.Buffered(1))
    col = lambda j: pl.BlockSpec((seqlen, LANES), lambda b: (b, j))
    return pl.pallas_call(
        functools.partial(_s5_kernel, groups=groups, gch=gch),
        grid=(bsz,),
        in_specs=[col(j) for j in range(n_t)] + [const(toep), const(wsr), const(wsi), const(wor), const(woi),
                                                 const(a_re), const(a_im)],
        out_specs=[pl.BlockSpec((seqlen, LANES), lambda b: (b, 0))] * n_t,
        out_shape=[jax.ShapeDtypeStruct((t, LANES), F32)] * n_t,
        scratch_shapes=[pltpu.VMEM((groups // 2, nc, 2 * S5_CHUNK * gch), BF16),
                        pltpu.VMEM((groups // 2, nc, 2 * S5_CHUNK * gch), F32)]
        + [pltpu.VMEM((nc, a_re.shape[1]), F32)] * 4,
        compiler_params=_params("parallel"),
        name="s5_core",
    )(*([u5] * n_t), toep, wsr, wsi, wor, woi, a_re, a_im)


def _s5_weights(lam_re, lam_im, log_dt, b_re, b_im, c_re, c_im):
    lc = S5_CHUNK
    g, p, h = b_re.shape
    step = jnp.exp(log_dt)[:, None]
    mag = jnp.exp(lam_re * step)
    ab_re, ab_im = mag * jnp.cos(lam_im * step), mag * jnp.sin(lam_im * step)
    den = lam_re * lam_re + lam_im * lam_im
    nr = ab_re - 1.0
    coef_re = (nr * lam_re + ab_im * lam_im) / den
    coef_im = (ab_im * lam_re - nr * lam_im) / den
    bb_re = coef_re[..., None] * b_re - coef_im[..., None] * b_im
    bb_im = coef_re[..., None] * b_im + coef_im[..., None] * b_re
    jj = jnp.arange(lc + 1, dtype=F32)[:, None, None]
    pmag = jnp.exp(jj * (lam_re * step)[None])
    pw_re = pmag * jnp.cos(jj * (lam_im * step)[None])
    pw_im = pmag * jnp.sin(jj * (lam_im * step)[None])
    cp_re = c_re[None] * pw_re[:, :, None, :] - c_im[None] * pw_im[:, :, None, :]
    cp_im = c_re[None] * pw_im[:, :, None, :] + c_im[None] * pw_re[:, :, None, :]
    kern = (jnp.einsum('jghp,gpk->jghk', cp_re[:lc], bb_re, precision='highest')
            - jnp.einsum('jghp,gpk->jghk', cp_im[:lc], bb_im, precision='highest'))
    tt = jnp.arange(lc)
    lag = tt[None, :] - tt[:, None]
    toep = jnp.where((lag >= 0)[:, :, None, None, None], kern[jnp.clip(lag, 0, lc - 1)], 0.0)
    toep = toep.transpose(2, 0, 4, 1, 3).reshape(g, lc * h, lc * h)
    rev_re, rev_im = pw_re[lc - 1 - tt], pw_im[lc - 1 - tt]
    ws_re = rev_re[..., None] * bb_re[None] - rev_im[..., None] * bb_im[None]
    ws_im = rev_re[..., None] * bb_im[None] + rev_im[..., None] * bb_re[None]
    ws_re = ws_re.transpose(1, 0, 3, 2).reshape(g, lc * h, p)
    ws_im = ws_im.transpose(1, 0, 3, 2).reshape(g, lc * h, p)
    wo_re = cp_re[1:].transpose(1, 3, 0, 2).reshape(g, p, lc * h)
    wo_im = (-cp_im[1:]).transpose(1, 3, 0, 2).reshape(g, p, lc * h)
    eye2 = jnp.eye(2, dtype=F32)
    pair_in = lambda w: jnp.einsum('kiap,ij->kiajp', w.reshape(g // 2, 2, lc * h, p), eye2).reshape(
        g // 2, 2 * lc * h, 2 * p).astype(BF16)
    pair_out = lambda w: jnp.einsum('kipa,ij->kipja', w.reshape(g // 2, 2, p, lc * h), eye2).reshape(
        g // 2, 2 * p, 2 * lc * h).astype(BF16)
    return (toep.astype(BF16), pair_in(ws_re), pair_in(ws_im), pair_out(wo_re), pair_out(wo_im),
            pw_re[lc].reshape(1, g * p), pw_im[lc].reshape(1, g * p))


def _outproj_kernel(x_ref, ys_ref, *refs):
    n_t = len(refs) - 17
    y5_refs = refs[:n_t]
    (u5_ref, d5_ref, wg_ref, bg_ref, n5_ref, wa_ref, wb_ref, gate_ref, nw_ref, shift_ref, scale_ref,
     wrh_ref, wrl_ref, br_ref, x1_ref, h2_ref, lg_ref) = refs[n_t:]
    y = jnp.concatenate([r[...] for r in y5_refs], axis=1) + d5_ref[...] * u5_ref[...]
    y = jax.nn.gelu(y)
    y = y * jax.nn.sigmoid(_dot(y.astype(BF16), wg_ref[...]) + bg_ref[...])
    y = y * lax.rsqrt(jnp.mean(y * y, axis=-1, keepdims=True) + EPS) * n5_ref[...]
    m = _dot(ys_ref[...], wa_ref[...]) + _dot(y.astype(BF16), wb_ref[...])
    x1 = x_ref[...] + gate_ref[0] * m
    x1_ref[...] = x1
    h = x1 * lax.rsqrt(jnp.mean(x1 * x1, axis=-1, keepdims=True) + EPS) * nw_ref[...]
    h = h * (1.0 + scale_ref[0]) + shift_ref[0]
    h2_ref[...] = h
    h_hi, h_lo = _split2(h)
    wrh = wrh_ref[...]
    lg = _dot(h_hi, wrh) + _dot(h_lo, wrh) + _dot(h_hi, wrl_ref[...]) + br_ref[...]
    lg_ref[...] = lg.T[:ROUTER_ROWS, :]


def _outproj(x2, y_ssd, y5, u5, d5, w_glu, b_glu, n5, w_a, w_b, nw, ada3, wr_hi, wr_lo, br,
             layer, bsz, seqlen):
    t, d = x2.shape
    n_s = y_ssd.shape[1]
    n_5 = u5.shape[1]
    tm = min(512, seqlen)
    per_b = seqlen // tm
    row = lambda i: (i, 0)
    const = lambda a: pl.BlockSpec(a.shape, lambda i: (0, 0))
    ada_blk = lambda k: pl.BlockSpec((1, 1, d), lambda i: (layer * bsz + i // per_b, 0, k))
    return pl.pallas_call(
        _outproj_kernel,
        grid=(t // tm,),
        in_specs=[pl.BlockSpec((tm, d), row), pl.BlockSpec((tm, n_s), row)]
        + [pl.BlockSpec((tm, LANES), row)] * len(y5)
        + [pl.BlockSpec((tm, n_5), row), const(d5), const(w_glu), const(b_glu), const(n5),
                  const(w_a), const(w_b), ada_blk(2), const(nw), ada_blk(3), ada_blk(4),
                  const(wr_hi), const(wr_lo), const(br)],
        out_specs=[pl.BlockSpec((tm, d), row), pl.BlockSpec((tm, d), row),
                   pl.BlockSpec((ROUTER_ROWS, tm), lambda i: (0, i))],
        out_shape=[jax.ShapeDtypeStruct((t, d), F32), jax.ShapeDtypeStruct((t, d), F32),
                   jax.ShapeDtypeStruct((ROUTER_ROWS, t), F32)],
        compiler_params=_params("parallel"),
        name="outproj",
    )(x2, y_ssd, *y5, u5, d5, w_glu, b_glu, n5, w_a, w_b, ada3, nw, ada3, ada3, wr_hi, wr_lo, br)


def _route_kernel(lg_ref, upper_ref, dest_ref, gate_ref, cnt_ref, counts, carry, pstart,
                  *, n_experts, per_group, block_rows, sub):
    ph = pl.program_id(0)
    i = pl.program_id(1)
    tr = lg_ref.shape[1]
    lg = lg_ref[...]

    @pl.when((ph == 0) & (i == 0))
    def _():
        counts[...] = jnp.zeros(counts.shape, F32)

    gl = [lg[n_experts + k:n_experts + k + 1, :] for k in range(N_EXPERT_GROUPS)]
    gmax = functools.reduce(jnp.maximum, gl)
    gidx = jnp.full((1, tr), N_EXPERT_GROUPS - 1, jnp.int32)
    for k in range(N_EXPERT_GROUPS - 2, -1, -1):
        gidx = jnp.where(gl[k] == gmax, k, gidx)
    gsum = functools.reduce(lambda a, b: a + b, [jnp.exp(v - gmax) for v in gl])
    g_w = 1.0 / gsum
    el = lg[0:per_group, :]
    for k in range(1, N_EXPERT_GROUPS):
        el = jnp.where(gidx == k, lg[k * per_group:(k + 1) * per_group, :], el)
    ep = jnp.exp(el - jnp.max(el, axis=0, keepdims=True))
    prob = ep / jnp.sum(ep, axis=0, keepdims=True)
    jj = lax.broadcasted_iota(jnp.int32, (per_group, tr), 0).astype(F32)
    p1 = jnp.max(prob, axis=0, keepdims=True)
    i1 = jnp.min(jnp.where(prob == p1, jj, float(per_group)), axis=0, keepdims=True)
    prob2 = jnp.where(jj == i1, -1.0, prob)
    p2 = jnp.max(prob2, axis=0, keepdims=True)
    i2 = jnp.min(jnp.where(prob2 == p2, jj, float(per_group)), axis=0, keepdims=True)
    den = p1 + p2
    gate_ref[0] = jnp.concatenate([g_w * p1 / den, g_w * p2 / den], axis=0)
    e1 = gidx * per_group + i1.astype(jnp.int32)
    e2 = gidx * per_group + i2.astype(jnp.int32)
    rr = lax.broadcasted_iota(jnp.int32, (n_experts, tr), 0)
    oh1 = rr == e1
    oh2 = rr == e2
    member = jnp.where(oh1 | oh2, 1.0, 0.0)

    @pl.when(ph == 0)
    def _():
        counts[...] = counts[...] + jnp.sum(member, axis=1, keepdims=True)
        dest_ref[...] = jnp.zeros(dest_ref.shape, jnp.int32)

    @pl.when(ph == 1)
    def _():
        @pl.when(i == 0)
        def _():
            blocks = (counts[...].astype(jnp.int32) + (block_rows - 1)) >> int(math.log2(block_rows))
            hi = (blocks >> 4).astype(F32).astype(BF16)
            lo = (blocks & 15).astype(F32).astype(BF16)
            er = lax.broadcasted_iota(jnp.int32, (n_experts, n_experts), 0)
            ec = lax.broadcasted_iota(jnp.int32, (n_experts, n_experts), 1)
            lower = jnp.where(ec < er, 1.0, 0.0).astype(BF16)
            pstart[...] = (16.0 * _dot(lower, hi) + _dot(lower, lo)) * float(block_rows)
            carry[...] = jnp.zeros(carry.shape, F32)

        run = carry[...]
        upper = upper_ref[...]
        pieces = []
        for b in range(tr // sub):
            mb = member[:, b * sub:(b + 1) * sub]
            pieces.append(_dot(mb.astype(BF16), upper) + jnp.concatenate([run] * (sub // LANES), axis=1))
            run = run + jnp.sum(mb, axis=1, keepdims=True)
        carry[...] = run
        base = jnp.concatenate(pieces, axis=1) + jnp.concatenate([pstart[...]] * (tr // LANES), axis=1)
        d1 = jnp.sum(jnp.where(oh1, base, 0.0), axis=0, keepdims=True)
        d2 = jnp.sum(jnp.where(oh2, base, 0.0), axis=0, keepdims=True)
        dest_ref[0] = jnp.concatenate([d1, d2], axis=0).astype(jnp.int32)

    cnt_ref[...] = counts[...].astype(jnp.int32)


def _route(logits_t, upper, n_experts, block_rows):
    rows, t = logits_t.shape
    tr = 1024
    sub = upper.shape[0]
    return pl.pallas_call(
        functools.partial(_route_kernel, n_experts=n_experts, per_group=n_experts // N_EXPERT_GROUPS,
                          block_rows=block_rows, sub=sub),
        grid=(2, t // tr),
        in_specs=[pl.BlockSpec((rows, tr), lambda ph, i: (0, i)),
                  pl.BlockSpec(upper.shape, lambda ph, i: (0, 0))],
        out_specs=[pl.BlockSpec((1, TOP_K, tr), lambda ph, i: (ph, 0, i)),
                   pl.BlockSpec((1, TOP_K, tr), lambda ph, i: (ph, 0, i)),
                   pl.BlockSpec((n_experts, LANES), lambda ph, i: (0, 0))],
        out_shape=[jax.ShapeDtypeStruct((2, TOP_K, t), jnp.int32), jax.ShapeDtypeStruct((2, TOP_K, t), F32),
                   jax.ShapeDtypeStruct((n_experts, LANES), jnp.int32)],
        scratch_shapes=[pltpu.VMEM((n_experts, LANES), F32)] * 3,
        compiler_params=_params("arbitrary", "arbitrary"),
        name="route",
    )(logits_t, upper)


DMA_UNROLL = 8


def _dispatch_kernel(dest_ref, h_ref, init_ref, rows_ref, sem, *, tq, t):
    del init_ref
    base = pl.program_id(0) * tq

    def copy(j, k):
        return pltpu.make_async_copy(h_ref.at[pl.ds(j, 1)], rows_ref.at[pl.ds(dest_ref[k * t + base + j], 1)], sem)

    def issue(j, _):
        for k in range(TOP_K):
            copy(j, k).start()
        return 0

    def drain(j, _):
        for k in range(TOP_K):
            copy(j, k).wait()
        return 0

    lax.fori_loop(0, tq, issue, 0, unroll=DMA_UNROLL)
    lax.fori_loop(0, tq, drain, 0, unroll=DMA_UNROLL)


def _dispatch(dest_flat, h2, n_pad):
    t, d = h2.shape
    tq = 256
    init = jnp.zeros((n_pad, d), h2.dtype)
    return pl.pallas_call(
        functools.partial(_dispatch_kernel, tq=tq, t=t),
        grid_spec=pltpu.PrefetchScalarGridSpec(
            num_scalar_prefetch=1, grid=(t // tq,),
            in_specs=[pl.BlockSpec((tq, d), lambda i, dr: (i, 0)), pl.BlockSpec(memory_space=pl.ANY)],
            out_specs=pl.BlockSpec(memory_space=pl.ANY),
            scratch_shapes=[pltpu.SemaphoreType.DMA]),
        out_shape=jax.ShapeDtypeStruct((n_pad, d), h2.dtype),
        input_output_aliases={2: 0},
        compiler_params=_params("arbitrary"),
        name="dispatch",
    )(dest_flat, h2, init)


def _ffn_kernel(be_ref, first_ref, slot_ref, nxt_ref, hasnext_ref, nused_ref, x_ref, wg_hbm, wu_hbm, wd_hbm,
                y_ref, sg, su, sd, wg16, wu16, wd16, sems):
    i = pl.program_id(0)

    def weight_copies(e, s):
        return (pltpu.make_async_copy(wg_hbm.at[e], sg.at[s], sems.at[s, 0]),
                pltpu.make_async_copy(wu_hbm.at[e], su.at[s], sems.at[s, 1]),
                pltpu.make_async_copy(wd_hbm.at[e], sd.at[s], sems.at[s, 2]))

    @pl.when(first_ref[i] == 1)
    def _():
        s = slot_ref[i]
        e = be_ref[i]

        @pl.when(i == 0)
        def _():
            for cp in weight_copies(e, s):
                cp.start()

        for cp in weight_copies(e, s):
            cp.wait()
        wg16[...] = sg[s].astype(BF16)
        wu16[...] = su[s].astype(BF16)
        wd16[...] = sd[s].astype(BF16)

        @pl.when(hasnext_ref[i] == 1)
        def _():
            for cp in weight_copies(nxt_ref[i], 1 - s):
                cp.start()

    @pl.when(i < nused_ref[0])
    def _():
        xb = x_ref[...].astype(BF16)
        a = jax.nn.silu(_dot(xb, wg16[...])) * _dot(xb, wu16[...])
        y_ref[...] = _dot(a.astype(BF16), wd16[...])

    @pl.when(i >= nused_ref[0])
    def _():
        y_ref[...] = jnp.zeros(y_ref.shape, y_ref.dtype)


def _ffn(counts, rows, w_eg, w_eu, w_ed, bm):
    n_pad, d = rows.shape
    n_experts, _, ff = w_eg.shape
    n_blocks = n_pad // bm
    ends = jnp.cumsum((counts + bm - 1) // bm).astype(jnp.int32)
    n_used = ends[-1]
    bidx = jnp.arange(n_blocks, dtype=jnp.int32)
    be = jnp.minimum(jnp.sum(ends[None, :] <= bidx[:, None], axis=1), n_experts - 1).astype(jnp.int32)
    be = jnp.where(bidx < n_used, be, be[n_used - 1])
    first = jnp.concatenate([jnp.ones((1,), jnp.int32), (be[1:] != be[:-1]).astype(jnp.int32)])
    slot = ((jnp.cumsum(first) - 1) & 1).astype(jnp.int32)
    seg_end = ends[be]
    has_next = (seg_end < n_used).astype(jnp.int32)
    nxt = be[jnp.minimum(seg_end, n_blocks - 1)]
    row = lambda i, *_: (i, 0)
    return pl.pallas_call(
        _ffn_kernel,
        grid_spec=pltpu.PrefetchScalarGridSpec(
            num_scalar_prefetch=6, grid=(n_blocks,),
            in_specs=[pl.BlockSpec((bm, d), row)] + [pl.BlockSpec(memory_space=pl.ANY)] * 3,
            out_specs=pl.BlockSpec((bm, d), row),
            scratch_shapes=[pltpu.VMEM((2, d, ff), F32), pltpu.VMEM((2, d, ff), F32), pltpu.VMEM((2, ff, d), F32),
                            pltpu.VMEM((d, ff), BF16), pltpu.VMEM((d, ff), BF16), pltpu.VMEM((ff, d), BF16),
                            pltpu.SemaphoreType.DMA((2, 3))]),
        out_shape=jax.ShapeDtypeStruct((n_pad, d), F32),
        compiler_params=_params("arbitrary"),
        name="ffn",
    )(be, first, slot, nxt, has_next, n_used[None], rows, w_eg, w_eu, w_ed)


def _combine_kernel(dest_ref, yrows_ref, x_ref, g_ref, gate_ref, fw_ref, o_ref, buf, sem, *, tq, t, final):
    base = pl.program_id(0) * tq

    def copy(j, k):
        return pltpu.make_async_copy(yrows_ref.at[pl.ds(dest_ref[k * t + base + j], 1)],
                                     buf.at[k, pl.ds(j, 1)], sem)

    def issue(j, _):
        for k in range(TOP_K):
            copy(j, k).start()
        return 0

    def drain(j, _):
        for k in range(TOP_K):
            copy(j, k).wait()
        return 0

    lax.fori_loop(0, tq, issue, 0, unroll=DMA_UNROLL)
    lax.fori_loop(0, tq, drain, 0, unroll=DMA_UNROLL)
    g = g_ref[...]
    f = g[:, 0:1] * buf[0] + g[:, 1:2] * buf[1]
    x2 = x_ref[...] + gate_ref[0] * f
    if final:
        x2 = x2 * lax.rsqrt(jnp.mean(x2 * x2, axis=-1, keepdims=True) + EPS) * fw_ref[...]
    o_ref[...] = x2


def _combine(dest_flat, y_rows, x1, gates_t, ada3, fw, layer, bsz, seqlen, final):
    t, d = x1.shape
    tq = 256
    per_b = seqlen // tq
    return pl.pallas_call(
        functools.partial(_combine_kernel, tq=tq, t=t, final=final),
        grid_spec=pltpu.PrefetchScalarGridSpec(
            num_scalar_prefetch=1, grid=(t // tq,),
            in_specs=[pl.BlockSpec(memory_space=pl.ANY),
                      pl.BlockSpec((tq, d), lambda i, dr: (i, 0)),
                      pl.BlockSpec((tq, TOP_K), lambda i, dr: (i, 0)),
                      pl.BlockSpec((1, 1, d), lambda i, dr: (layer * bsz + i // per_b, 0, 5)),
                      pl.BlockSpec((1, d), lambda i, dr: (0, 0))],
            out_specs=pl.BlockSpec((tq, d), lambda i, dr: (i, 0)),
            scratch_shapes=[pltpu.VMEM((TOP_K, tq, d), F32), pltpu.SemaphoreType.DMA]),
        out_shape=jax.ShapeDtypeStruct((t, d), F32),
        compiler_params=_params("arbitrary"),
        name="combine",
    )(dest_flat, y_rows, x1, gates_t, ada3, fw)


def kernel(x, c, w_ada, b_ada, norm1_w, w_in, conv_w, conv_b, dt_bias, a_log, d_ssd, ssd_norm_w,
           s5_lam_re, s5_lam_im, s5_log_dt, s5_b_re, s5_b_im, s5_c_re, s5_c_im, s5_d, w_glu, b_glu,
           s5_norm_w, w_out, norm2_w, w_rg, b_rg, w_re, b_re, w_eg, w_eu, w_ed, final_norm_w):
    bsz, seqlen, d = x.shape
    t = bsz * seqlen
    depth = w_in.shape[0]
    heads = dt_bias.shape[1]
    inner = ssd_norm_w.shape[1]
    cch = conv_w.shape[2]
    n5 = s5_d.shape[1]
    s5_groups, _, s5_h = s5_b_re.shape[1:]
    n_experts = w_re.shape[2]
    moe_block = 256
    n_rows = t * TOP_K
    n_pad = n_rows + n_experts * moe_block
    n_blocks = n_pad // moe_block

    i0, i1, i2 = inner, inner + cch, inner + cch + heads
    w_zx = w_in[..., :i1].astype(BF16)
    w_u = w_in[..., i2:].astype(BF16)
    w_dt = jnp.pad(w_in[..., i1:i2], ((0, 0), (0, 0), (0, LANES - heads))).astype(BF16)
    lane_pad = lambda a: jnp.pad(a, ((0, 0), (0, LANES - a.shape[1])))[:, None, :]
    dtb, alog = lane_pad(dt_bias), lane_pad(a_log)
    dexp = jnp.repeat(d_ssd, inner // heads, axis=1)[:, None, :]
    li = jnp.arange(SSD_CHUNK)
    tri = (li[None, :] <= li[:, None]).astype(BF16)
    e_mat = (jnp.arange(LANES)[:, None] == (jnp.arange(inner) // (inner // heads))[None, :]).astype(BF16)
    w_out_a = w_out[:, :inner].astype(BF16)
    w_out_b = w_out[:, inner:].astype(BF16)
    w_glu_b = w_glu.astype(BF16)
    wr = jnp.concatenate([w_re, w_rg, jnp.zeros((depth, d, LANES - n_experts - N_EXPERT_GROUPS), F32)], axis=-1)
    wr_hi = wr.astype(BF16)
    wr_lo = (wr - wr_hi.astype(F32)).astype(BF16)
    br = jnp.concatenate([b_re, b_rg, jnp.zeros((depth, LANES - n_experts - N_EXPERT_GROUPS), F32)], axis=-1)[:, None, :]
    sub = 256
    si = jnp.arange(sub)
    upper = (si[:, None] < si[None, :]).astype(BF16)

    ada3 = _ada(c, w_ada, b_ada).reshape(depth * bsz, 1, 6 * d)
    x2 = x.reshape(t, d)
    nc5 = seqlen // S5_CHUNK
    for l in range(depth):
        z, xbc, u5, dtp = _inproj(x2, norm1_w[l][None], ada3, w_zx[l], w_u[l], w_dt[l], l, bsz, seqlen, inner)
        y_ssd = _ssd(z, xbc, dtp, conv_w[l], conv_b[l][None], dtb[l], alog[l], dexp[l], ssd_norm_w[l][None],
                     tri, e_mat, bsz, seqlen, heads)
        s5w = _s5_weights(s5_lam_re[l], s5_lam_im[l], s5_log_dt[l], s5_b_re[l], s5_b_im[l], s5_c_re[l], s5_c_im[l])
        y5 = _s5_core(u5, *s5w, bsz, seqlen, s5_h)
        x1, h2, logits_t = _outproj(x2, y_ssd, y5, u5, s5_d[l][None], w_glu_b[l], b_glu[l][None],
                                    s5_norm_w[l][None], w_out_a[l], w_out_b[l], norm2_w[l][None], ada3,
                                    wr_hi[l], wr_lo[l], br[l], l, bsz, seqlen)
        dest, gates, counts = _route(logits_t, upper, n_experts, moe_block)
        dest_flat = dest[1].reshape(-1)
        rows = _dispatch(dest_flat, h2, n_pad)
        y_rows = _ffn(counts[:, 0], rows, w_eg[l], w_eu[l], w_ed[l], moe_block)
        x2 = _combine(dest_flat, y_rows, x1, gates[1].T, ada3, final_norm_w[None], l, bsz, seqlen, l == depth - 1)
    return x2.reshape(bsz, seqlen, d)
```

```python
import functools
import math

import jax
import jax.numpy as jnp
from jax import lax
from jax.experimental import pallas as pl
from jax.experimental.pallas import tpu as pltpu

F32 = jnp.float32
BF16 = jnp.bfloat16
EPS = 1e-6

SSD_GROUPS = 2
SSD_STATE = 128
SSD_CHUNK = 128
S5_CHUNK = 16
N_EXPERT_GROUPS = 4
TOP_K = 2
LANES = 128
ROUTER_ROWS = 40
VMEM_LIMIT = 48 * 1024 * 1024


def _dot(a, b):
    return jnp.dot(a, b, preferred_element_type=F32)


def _split2(a):
    hi = a.astype(BF16)
    lo = (a - hi.astype(F32)).astype(BF16)
    return hi, lo


def _split3(a):
    hi = a.astype(BF16)
    r = a - hi.astype(F32)
    mid = r.astype(BF16)
    lo = (r - mid.astype(F32)).astype(BF16)
    return hi, mid, lo


def _pack_rows(x):
    w = x.shape[1] // 2
    bits = lambda v: lax.bitcast_convert_type(v.astype(BF16).astype(F32), jnp.uint32)
    return (bits(x[:, :w]) >> 16) | bits(x[:, w:])


def _unpack_rows(words):
    lo = lax.bitcast_convert_type(words << 16, F32)
    hi = lax.bitcast_convert_type(words & jnp.uint32(0xFFFF0000), F32)
    return jnp.concatenate([lo, hi], axis=1)


def _params(*sem):
    return pltpu.CompilerParams(dimension_semantics=sem, vmem_limit_bytes=VMEM_LIMIT)


def _ada_kernel(c_ref, w_ref, b_ref, o_ref):
    ca = jax.nn.silu(c_ref[...])
    c_hi, c_lo = _split2(ca)
    w_hi, w_lo = _split2(w_ref[0])
    o_ref[0] = _dot(c_hi, w_hi) + _dot(c_lo, w_hi) + _dot(c_hi, w_lo) + b_ref[0]


def _ada(c, w_ada, b_ada):
    n_layers, d, n_out = w_ada.shape
    bsz = c.shape[0]
    tn = 1536
    return pl.pallas_call(
        _ada_kernel,
        grid=(n_layers, n_out // tn),
        in_specs=[pl.BlockSpec((bsz, d), lambda l, j: (0, 0)),
                  pl.BlockSpec((1, d, tn), lambda l, j: (l, 0, j)),
                  pl.BlockSpec((1, 1, tn), lambda l, j: (l, 0, j))],
        out_specs=pl.BlockSpec((1, bsz, tn), lambda l, j: (l, 0, j)),
        out_shape=jax.ShapeDtypeStruct((n_layers, bsz, n_out), F32),
        compiler_params=_params("parallel", "parallel"),
        name="ada",
    )(c, w_ada, b_ada.reshape(n_layers, 1, n_out))


def _inproj_kernel(x_ref, nw_ref, shift_ref, scale_ref, wzx_ref, wdu_ref,
                   z_ref, xbc_ref, u_ref, dt_ref, *, n_z, n_dt):
    x = x_ref[...]
    h = x * lax.rsqrt(jnp.mean(x * x, axis=-1, keepdims=True) + EPS) * nw_ref[...]
    h = (h * (1.0 + scale_ref[0]) + shift_ref[0]).astype(BF16)
    p = _dot(h, wzx_ref[...])
    z_ref[...] = p[:, :n_z]
    xbc_ref[...] = p[:, n_z:]
    q = _dot(h, wdu_ref[...])
    dt_ref[...] = q[:, :LANES]
    u_ref[...] = q[:, n_dt:n_dt + u_ref.shape[1]]


def _inproj(x2, nw, ada3, w_zx, w_du, n_u, layer, bsz, seqlen, n_z, n_dt):
    t, d = x2.shape
    n_xbc = w_zx.shape[1] - n_z
    tm = min(512, seqlen)
    per_b = seqlen // tm
    row = lambda i: (i, 0)
    const = lambda a: pl.BlockSpec(a.shape, lambda i: (0, 0))
    ada_blk = lambda k: pl.BlockSpec((1, 1, d), lambda i: (layer * bsz + i // per_b, 0, k))
    return pl.pallas_call(
        functools.partial(_inproj_kernel, n_z=n_z, n_dt=n_dt),
        grid=(t // tm,),
        in_specs=[pl.BlockSpec((tm, d), row),
                  pl.BlockSpec((1, d), lambda i: (0, 0)),
                  ada_blk(0), ada_blk(1), const(w_zx), const(w_du)],
        out_specs=[pl.BlockSpec((tm, n_z), row), pl.BlockSpec((tm, n_xbc), row),
                   pl.BlockSpec((tm, n_u), row), pl.BlockSpec((tm, LANES), row)],
        out_shape=[jax.ShapeDtypeStruct((t, n_z), F32), jax.ShapeDtypeStruct((t, n_xbc), F32),
                   jax.ShapeDtypeStruct((t, n_u), F32), jax.ShapeDtypeStruct((t, LANES), F32)],
        compiler_params=_params("parallel"),
        name="inproj",
    )(x2, nw, ada3, ada3, w_zx, w_du)


def _ssd_kernel(z_ref, xbc_ref, dt_ref, cw_ref, cb_ref, dtb_ref, alog_ref, dexp_ref, nw_ref,
                tri_ref, e_ref, y_ref, ext_ref, st_ref, *, inner, heads, n_conv):
    L = SSD_CHUNK
    N = SSD_STATE
    hd = inner // heads
    hpg = heads // SSD_GROUPS
    gw = inner // SSD_GROUPS
    tail = 8

    @pl.when(pl.program_id(1) == 0)
    def _():
        ext_ref[0:tail, :] = jnp.zeros((tail, ext_ref.shape[1]), F32)
        st_ref[...] = jnp.zeros(st_ref.shape, F32)

    u_new = xbc_ref[...]
    ext = jnp.concatenate([ext_ref[...], u_new], axis=0)
    conv = cb_ref[...] + cw_ref[n_conv - 1:n_conv, :] * u_new
    for k in range(n_conv - 1):
        conv = conv + cw_ref[k:k + 1, :] * pltpu.roll(ext, n_conv - 1 - k, axis=0)[tail:, :]
    ext_ref[...] = u_new[L - tail:, :]
    xa = jax.nn.silu(conv)
    xs = xa[:, :inner]
    bm = xa[:, inner:inner + SSD_GROUPS * N]
    cm = xa[:, inner + SSD_GROUPS * N:]

    pre = dt_ref[...] + dtb_ref[...]
    dt = jnp.maximum(pre, 0.0) + jnp.log1p(jnp.exp(-jnp.abs(pre)))
    ad = dt * (-jnp.exp(alog_ref[...]))
    tri = tri_ref[...]
    a1, a2, a3 = _split3(ad)
    acs = _dot(tri, a1) + _dot(tri, a2) + _dot(tri, a3)
    acs_t = acs.T
    dt_t = dt.T
    w = dt * jnp.exp(acs[L - 1:L, :] - acs)
    eacs = jnp.exp(acs)
    e_mat = e_ref[...]
    w1, w2 = _split2(w)
    w_exp = _dot(w1, e_mat) + _dot(w2, e_mat)
    q1, q2 = _split2(eacs)
    eacs_exp = _dot(q1, e_mat) + _dot(q2, e_mat)

    row = lax.broadcasted_iota(jnp.int32, (L, L), 0)
    col = lax.broadcasted_iota(jnp.int32, (L, L), 1)
    causal = row >= col
    lane = lax.broadcasted_iota(jnp.int32, (L, 2 * hd), 1)
    h_prev = st_ref[...]

    y_diag, y_off, st_new = [], [], []
    for g in range(SSD_GROUPS):
        bg = bm[:, g * N:(g + 1) * N]
        cg = cm[:, g * N:(g + 1) * N].astype(BF16)
        cb = lax.dot_general(cg, bg.astype(BF16), (((1,), (1,)), ((), ())), preferred_element_type=F32)
        for j in range(hpg // 2):
            h0 = g * hpg + 2 * j
            lms = []
            for h in (h0, h0 + 1):
                seg = jnp.where(causal, acs[:, h:h + 1] - acs_t[h:h + 1, :], -1e30)
                lms.append((cb * jnp.exp(seg) * dt_t[h:h + 1, :]).astype(BF16))
            pair = xs[:, h0 * hd:(h0 + 2) * hd]
            w_bd = jnp.concatenate([jnp.where(lane < hd, pair, 0.0), jnp.where(lane >= hd, pair, 0.0)],
                                   axis=0).astype(BF16)
            y_diag.append(_dot(jnp.concatenate(lms, axis=1), w_bd))
        sl = slice(g * gw, (g + 1) * gw)
        y_off.append(_dot(cg, h_prev[:, sl].astype(BF16)))
        st_new.append(_dot(bg.T.astype(BF16), (xs[:, sl] * w_exp[:, sl]).astype(BF16)))

    y = jnp.concatenate(y_diag, axis=1) + jnp.concatenate(y_off, axis=1) * eacs_exp + dexp_ref[...] * xs
    st_ref[...] = h_prev * eacs_exp[L - 1:L, :] + jnp.concatenate(st_new, axis=1)
    y = y * jax.nn.silu(z_ref[...])
    outs = []
    for g in range(SSD_GROUPS):
        yg = y[:, g * gw:(g + 1) * gw]
        outs.append(yg * lax.rsqrt(jnp.mean(yg * yg, axis=-1, keepdims=True) + EPS))
    y_ref[...] = (jnp.concatenate(outs, axis=1) * nw_ref[...]).astype(y_ref.dtype)


def _ssd(z, xbc, dtp, conv_w, conv_b, dtb, alog, dexp, nw, tri, e_mat, bsz, seqlen, heads):
    t, inner = z.shape
    cch = xbc.shape[1]
    L = SSD_CHUNK
    nc = seqlen // L
    n_conv = conv_w.shape[0]
    row = lambda b, c: (b * nc + c, 0)
    const = lambda shape: pl.BlockSpec(shape, lambda b, c: (0, 0))
    return pl.pallas_call(
        functools.partial(_ssd_kernel, inner=inner, heads=heads, n_conv=n_conv),
        grid=(bsz, nc),
        in_specs=[pl.BlockSpec((L, inner), row), pl.BlockSpec((L, cch), row), pl.BlockSpec((L, LANES), row),
                  const((n_conv, cch)), const((1, cch)), const((1, LANES)), const((1, LANES)),
                  const((1, inner)), const((1, inner)), const((L, L)), const((LANES, inner))],
        out_specs=pl.BlockSpec((L, inner), row),
        out_shape=jax.ShapeDtypeStruct((t, inner), BF16),
        scratch_shapes=[pltpu.VMEM((8, cch), F32), pltpu.VMEM((SSD_STATE, inner), F32)],
        compiler_params=_params("parallel", "arbitrary"),
        name="ssd",
    )(z, xbc, dtp, conv_w, conv_b, dtb, alog, dexp, nw, tri, e_mat)


def _s5_kernel(*refs, groups, gch):
    n_t = groups * gch // LANES
    u_refs = refs[:n_t]
    toep_ref, wsr_ref, wsi_ref, wor_ref, woi_ref, ar_ref, ai_ref = refs[n_t:n_t + 7]
    y_refs = refs[n_t + 7:2 * n_t + 7]
    fold, yfold, sre, sim, pre, pim = refs[2 * n_t + 7:]
    lc = S5_CHUNK
    nc = u_refs[0].shape[0] // lc
    per_tile = LANES // gch
    n_tiles = groups // per_tile
    halves = lc * gch // LANES
    w2 = halves * LANES
    rb = 16
    piece = lax.broadcasted_iota(jnp.int32, (rb, LANES), 1) // gch

    def merge(srcs, shift_of):
        acc = None
        for q, src in enumerate(srcs):
            sh = shift_of(q) * gch
            r = pltpu.roll(src, sh, axis=1) if sh else src
            acc = r if acc is None else jnp.where(piece == q, r, acc)
        return acc

    def fold_rows(b, _):
        r0 = pl.multiple_of(b * rb, rb)
        for j in range(n_tiles):
            tiles = [u_refs[j][pl.ds(b * (rb * lc) + t, rb, stride=lc), :] for t in range(lc)]
            for gp in range(per_tile):
                g = j * per_tile + gp
                parts = [merge(tiles[hv * per_tile:(hv + 1) * per_tile], lambda tt: (tt - gp) % per_tile)
                         for hv in range(halves)]
                fold[g // 2, pl.ds(r0, rb), (g % 2) * w2:(g % 2 + 1) * w2] = (
                    jnp.concatenate(parts, axis=1).astype(BF16))
        return 0

    lax.fori_loop(0, nc // rb, fold_rows, 0)

    n_pairs = groups // 2
    for k in range(n_pairs):
        sre[:, k * LANES:(k + 1) * LANES] = _dot(fold[k], wsr_ref[k])
        sim[:, k * LANES:(k + 1) * LANES] = _dot(fold[k], wsi_ref[k])

    ar = ar_ref[...]
    ai = ai_ref[...]

    def body(c, carry):
        r, i = carry
        pre[pl.ds(c, 1), :] = r
        pim[pl.ds(c, 1), :] = i
        return ar * r - ai * i + sre[pl.ds(c, 1), :], ar * i + ai * r + sim[pl.ds(c, 1), :]

    zero = jnp.zeros((1, sre.shape[1]), F32)
    lax.fori_loop(0, nc, body, (zero, zero))

    for k in range(n_pairs):
        a = fold[k]
        intra = jnp.concatenate([_dot(a[:, :w2], toep_ref[2 * k]), _dot(a[:, w2:], toep_ref[2 * k + 1])], axis=1)
        yfold[k] = (intra + _dot(pre[:, k * LANES:(k + 1) * LANES].astype(BF16), wor_ref[k])
                    + _dot(pim[:, k * LANES:(k + 1) * LANES].astype(BF16), woi_ref[k]))

    def unfold_rows(b, _):
        r0 = pl.multiple_of(b * rb, rb)
        for j in range(n_tiles):
            for hv in range(halves):
                srcs = []
                for gp in range(per_tile):
                    g = j * per_tile + gp
                    srcs.append(yfold[g // 2, pl.ds(r0, rb), pl.ds((g % 2) * w2 + hv * LANES, LANES)])
                for tt in range(per_tile):
                    y_refs[j][pl.ds(b * (rb * lc) + hv * per_tile + tt, rb, stride=lc), :] = merge(
                        srcs, lambda gp: (gp - tt) % per_tile)
        return 0

    lax.fori_loop(0, nc // rb, unfold_rows, 0)


def _s5_core(u5, toep, wsr, wsi, wor, woi, a_re, a_im, bsz, seqlen, gch):
    t, n5 = u5.shape
    groups = n5 // gch
    nc = seqlen // S5_CHUNK
    n_t = n5 // LANES
    const = lambda a: pl.BlockSpec(a.shape, lambda b: (0,) * a.ndim, pipeline_mode=pl.Buffered(1))
    col = lambda j: pl.BlockSpec((seqlen, LANES), lambda b: (b, j))
    return pl.pallas_call(
        functools.partial(_s5_kernel, groups=groups, gch=gch),
        grid=(bsz,),
        in_specs=[col(j) for j in range(n_t)] + [const(toep), const(wsr), const(wsi), const(wor), const(woi),
                                                 const(a_re), const(a_im)],
        out_specs=[pl.BlockSpec((seqlen, LANES), lambda b: (b, 0))] * n_t,
        out_shape=[jax.ShapeDtypeStruct((t, LANES), F32)] * n_t,
        scratch_shapes=[pltpu.VMEM((groups // 2, nc, 2 * S5_CHUNK * gch), BF16),
                        pltpu.VMEM((groups // 2, nc, 2 * S5_CHUNK * gch), F32)]
        + [pltpu.VMEM((nc, a_re.shape[1]), F32)] * 4,
        compiler_params=_params("parallel"),
        name="s5_core",
    )(*([u5] * n_t), toep, wsr, wsi, wor, woi, a_re, a_im)


def _s5_weights(lam_re, lam_im, log_dt, b_re, b_im, c_re, c_im):
    lc = S5_CHUNK
    g, p, h = b_re.shape
    step = jnp.exp(log_dt)[:, None]
    mag = jnp.exp(lam_re * step)
    ab_re, ab_im = mag * jnp.cos(lam_im * step), mag * jnp.sin(lam_im * step)
    den = lam_re * lam_re + lam_im * lam_im
    nr = ab_re - 1.0
    coef_re = (nr * lam_re + ab_im * lam_im) / den
    coef_im = (ab_im * lam_re - nr * lam_im) / den
    bb_re = coef_re[..., None] * b_re - coef_im[..., None] * b_im
    bb_im = coef_re[..., None] * b_im + coef_im[..., None] * b_re
    jj = jnp.arange(lc + 1, dtype=F32)[:, None, None]
    pmag = jnp.exp(jj * (lam_re * step)[None])
    pw_re = pmag * jnp.cos(jj * (lam_im * step)[None])
    pw_im = pmag * jnp.sin(jj * (lam_im * step)[None])
    cp_re = c_re[None] * pw_re[:, :, None, :] - c_im[None] * pw_im[:, :, None, :]
    cp_im = c_re[None] * pw_im[:, :, None, :] + c_im[None] * pw_re[:, :, None, :]
    kern = (jnp.einsum('jghp,gpk->jghk', cp_re[:lc], bb_re, precision='highest')
            - jnp.einsum('jghp,gpk->jghk', cp_im[:lc], bb_im, precision='highest'))
    tt = jnp.arange(lc)
    lag = tt[None, :] - tt[:, None]
    toep = jnp.where((lag >= 0)[:, :, None, None, None], kern[jnp.clip(lag, 0, lc - 1)], 0.0)
    toep = toep.transpose(2, 0, 4, 1, 3).reshape(g, lc * h, lc * h)
    rev_re, rev_im = pw_re[lc - 1 - tt], pw_im[lc - 1 - tt]
    ws_re = rev_re[..., None] * bb_re[None] - rev_im[..., None] * bb_im[None]
    ws_im = rev_re[..., None] * bb_im[None] + rev_im[..., None] * bb_re[None]
    ws_re = ws_re.transpose(1, 0, 3, 2).reshape(g, lc * h, p)
    ws_im = ws_im.transpose(1, 0, 3, 2).reshape(g, lc * h, p)
    wo_re = cp_re[1:].transpose(1, 3, 0, 2).reshape(g, p, lc * h)
    wo_im = (-cp_im[1:]).transpose(1, 3, 0, 2).reshape(g, p, lc * h)
    eye2 = jnp.eye(2, dtype=F32)
    pair_in = lambda w: jnp.einsum('kiap,ij->kiajp', w.reshape(g // 2, 2, lc * h, p), eye2).reshape(
        g // 2, 2 * lc * h, 2 * p).astype(BF16)
    pair_out = lambda w: jnp.einsum('kipa,ij->kipja', w.reshape(g // 2, 2, p, lc * h), eye2).reshape(
        g // 2, 2 * p, 2 * lc * h).astype(BF16)
    return (toep.astype(BF16), pair_in(ws_re), pair_in(ws_im), pair_out(wo_re), pair_out(wo_im),
            pw_re[lc].reshape(1, g * p), pw_im[lc].reshape(1, g * p))


def _outproj_kernel(x_ref, ys_ref, *refs):
    n_t = len(refs) - 17
    y5_refs = refs[:n_t]
    (u5_ref, d5_ref, wg_ref, bg_ref, n5_ref, wa_ref, wb_ref, gate_ref, nw_ref, shift_ref, scale_ref,
     wrh_ref, wrl_ref, br_ref, x1_ref, h2_ref, lg_ref) = refs[n_t:]
    y = jnp.concatenate([r[...] for r in y5_refs], axis=1) + d5_ref[...] * u5_ref[...]
    y = jax.nn.gelu(y)
    y = y * jax.nn.sigmoid(_dot(y.astype(BF16), wg_ref[...]) + bg_ref[...])
    y = y * lax.rsqrt(jnp.mean(y * y, axis=-1, keepdims=True) + EPS) * n5_ref[...]
    m = _dot(ys_ref[...], wa_ref[...]) + _dot(y.astype(BF16), wb_ref[...])
    x1 = x_ref[...] + gate_ref[0] * m
    x1_ref[...] = x1
    h = x1 * lax.rsqrt(jnp.mean(x1 * x1, axis=-1, keepdims=True) + EPS) * nw_ref[...]
    h = h * (1.0 + scale_ref[0]) + shift_ref[0]
    h2_ref[...] = _pack_rows(h)
    h_hi, h_lo = _split2(h)
    wrh = wrh_ref[...]
    lg = _dot(h_hi, wrh) + _dot(h_lo, wrh) + _dot(h_hi, wrl_ref[...]) + br_ref[...]
    lg_ref[...] = lg.T[:ROUTER_ROWS, :]


def _outproj(x2, y_ssd, y5, u5, d5, w_glu, b_glu, n5, w_a, w_b, nw, ada3, wr_hi, wr_lo, br,
             layer, bsz, seqlen):
    t, d = x2.shape
    n_s = y_ssd.shape[1]
    n_5 = u5.shape[1]
    tm = min(512, seqlen)
    per_b = seqlen // tm
    row = lambda i: (i, 0)
    const = lambda a: pl.BlockSpec(a.shape, lambda i: (0, 0))
    ada_blk = lambda k: pl.BlockSpec((1, 1, d), lambda i: (layer * bsz + i // per_b, 0, k))
    return pl.pallas_call(
        _outproj_kernel,
        grid=(t // tm,),
        in_specs=[pl.BlockSpec((tm, d), row), pl.BlockSpec((tm, n_s), row)]
        + [pl.BlockSpec((tm, LANES), row)] * len(y5)
        + [pl.BlockSpec((tm, n_5), row), const(d5), const(w_glu), const(b_glu), const(n5),
                  const(w_a), const(w_b), ada_blk(2), const(nw), ada_blk(3), ada_blk(4),
                  const(wr_hi), const(wr_lo), const(br)],
        out_specs=[pl.BlockSpec((tm, d), row), pl.BlockSpec((tm, d // 2), row),
                   pl.BlockSpec((ROUTER_ROWS, tm), lambda i: (0, i))],
        out_shape=[jax.ShapeDtypeStruct((t, d), F32), jax.ShapeDtypeStruct((t, d // 2), jnp.uint32),
                   jax.ShapeDtypeStruct((ROUTER_ROWS, t), F32)],
        compiler_params=_params("parallel"),
        name="outproj",
    )(x2, y_ssd, *y5, u5, d5, w_glu, b_glu, n5, w_a, w_b, ada3, nw, ada3, ada3, wr_hi, wr_lo, br)


def _route_kernel(lg_ref, upper_ref, dest_ref, gate_ref, cnt_ref, counts, carry, pstart,
                  *, n_experts, per_group, block_rows, sub):
    ph = pl.program_id(0)
    i = pl.program_id(1)
    tr = lg_ref.shape[1]
    lg = lg_ref[...]

    @pl.when((ph == 0) & (i == 0))
    def _():
        counts[...] = jnp.zeros(counts.shape, F32)

    gl = [lg[n_experts + k:n_experts + k + 1, :] for k in range(N_EXPERT_GROUPS)]
    gmax = functools.reduce(jnp.maximum, gl)
    gidx = jnp.full((1, tr), N_EXPERT_GROUPS - 1, jnp.int32)
    for k in range(N_EXPERT_GROUPS - 2, -1, -1):
        gidx = jnp.where(gl[k] == gmax, k, gidx)
    gsum = functools.reduce(lambda a, b: a + b, [jnp.exp(v - gmax) for v in gl])
    g_w = 1.0 / gsum
    el = lg[0:per_group, :]
    for k in range(1, N_EXPERT_GROUPS):
        el = jnp.where(gidx == k, lg[k * per_group:(k + 1) * per_group, :], el)
    ep = jnp.exp(el - jnp.max(el, axis=0, keepdims=True))
    prob = ep / jnp.sum(ep, axis=0, keepdims=True)
    jj = lax.broadcasted_iota(jnp.int32, (per_group, tr), 0).astype(F32)
    p1 = jnp.max(prob, axis=0, keepdims=True)
    i1 = jnp.min(jnp.where(prob == p1, jj, float(per_group)), axis=0, keepdims=True)
    prob2 = jnp.where(jj == i1, -1.0, prob)
    p2 = jnp.max(prob2, axis=0, keepdims=True)
    i2 = jnp.min(jnp.where(prob2 == p2, jj, float(per_group)), axis=0, keepdims=True)
    den = p1 + p2
    gate_ref[0] = jnp.concatenate([g_w * p1 / den, g_w * p2 / den], axis=0)
    e1 = gidx * per_group + i1.astype(jnp.int32)
    e2 = gidx * per_group + i2.astype(jnp.int32)
    rr = lax.broadcasted_iota(jnp.int32, (n_experts, tr), 0)
    oh1 = rr == e1
    oh2 = rr == e2
    member = jnp.where(oh1 | oh2, 1.0, 0.0)

    @pl.when(ph == 0)
    def _():
        counts[...] = counts[...] + jnp.sum(member, axis=1, keepdims=True)
        dest_ref[...] = jnp.zeros(dest_ref.shape, jnp.int32)

    @pl.when(ph == 1)
    def _():
        @pl.when(i == 0)
        def _():
            blocks = (counts[...].astype(jnp.int32) + (block_rows - 1)) >> int(math.log2(block_rows))
            hi = (blocks >> 4).astype(F32).astype(BF16)
            lo = (blocks & 15).astype(F32).astype(BF16)
            er = lax.broadcasted_iota(jnp.int32, (n_experts, n_experts), 0)
            ec = lax.broadcasted_iota(jnp.int32, (n_experts, n_experts), 1)
            lower = jnp.where(ec < er, 1.0, 0.0).astype(BF16)
            pstart[...] = (16.0 * _dot(lower, hi) + _dot(lower, lo)) * float(block_rows)
            carry[...] = jnp.zeros(carry.shape, F32)

        run = carry[...]
        upper = upper_ref[...]
        pieces = []
        for b in range(tr // sub):
            mb = member[:, b * sub:(b + 1) * sub]
            pieces.append(_dot(mb.astype(BF16), upper) + jnp.concatenate([run] * (sub // LANES), axis=1))
            run = run + jnp.sum(mb, axis=1, keepdims=True)
        carry[...] = run
        base = jnp.concatenate(pieces, axis=1) + jnp.concatenate([pstart[...]] * (tr // LANES), axis=1)
        d1 = jnp.sum(jnp.where(oh1, base, 0.0), axis=0, keepdims=True)
        d2 = jnp.sum(jnp.where(oh2, base, 0.0), axis=0, keepdims=True)
        dest_ref[0] = jnp.concatenate([d1, d2], axis=0).astype(jnp.int32)

    cnt_ref[...] = counts[...].astype(jnp.int32)


def _route(logits_t, upper, n_experts, block_rows):
    rows, t = logits_t.shape
    tr = 1024
    sub = upper.shape[0]
    return pl.pallas_call(
        functools.partial(_route_kernel, n_experts=n_experts, per_group=n_experts // N_EXPERT_GROUPS,
                          block_rows=block_rows, sub=sub),
        grid=(2, t // tr),
        in_specs=[pl.BlockSpec((rows, tr), lambda ph, i: (0, i)),
                  pl.BlockSpec(upper.shape, lambda ph, i: (0, 0))],
        out_specs=[pl.BlockSpec((1, TOP_K, tr), lambda ph, i: (ph, 0, i)),
                   pl.BlockSpec((1, TOP_K, tr), lambda ph, i: (ph, 0, i)),
                   pl.BlockSpec((n_experts, LANES), lambda ph, i: (0, 0))],
        out_shape=[jax.ShapeDtypeStruct((2, TOP_K, t), jnp.int32), jax.ShapeDtypeStruct((2, TOP_K, t), F32),
                   jax.ShapeDtypeStruct((n_experts, LANES), jnp.int32)],
        scratch_shapes=[pltpu.VMEM((n_experts, LANES), F32)] * 3,
        compiler_params=_params("arbitrary", "arbitrary"),
        name="route",
    )(logits_t, upper)


DMA_UNROLL = 8


def _dispatch_kernel(dest_ref, h_ref, init_ref, rows_ref, sem, *, tq, t):
    del init_ref
    base = pl.program_id(0) * tq

    def copy(j, k):
        return pltpu.make_async_copy(h_ref.at[pl.ds(j, 1)], rows_ref.at[pl.ds(dest_ref[k * t + base + j], 1)], sem)

    def issue(j, _):
        for k in range(TOP_K):
            copy(j, k).start()
        return 0

    def drain(j, _):
        for k in range(TOP_K):
            copy(j, k).wait()
        return 0

    lax.fori_loop(0, tq, issue, 0, unroll=DMA_UNROLL)
    lax.fori_loop(0, tq, drain, 0, unroll=DMA_UNROLL)


def _dispatch(dest_flat, h2, n_pad):
    t, d = h2.shape
    tq = 256
    init = jnp.zeros((n_pad, d), h2.dtype)
    return pl.pallas_call(
        functools.partial(_dispatch_kernel, tq=tq, t=t),
        grid_spec=pltpu.PrefetchScalarGridSpec(
            num_scalar_prefetch=1, grid=(t // tq,),
            in_specs=[pl.BlockSpec((tq, d), lambda i, dr: (i, 0)), pl.BlockSpec(memory_space=pl.ANY)],
            out_specs=pl.BlockSpec(memory_space=pl.ANY),
            scratch_shapes=[pltpu.SemaphoreType.DMA]),
        out_shape=jax.ShapeDtypeStruct((n_pad, d), h2.dtype),
        input_output_aliases={2: 0},
        compiler_params=_params("arbitrary"),
        name="dispatch",
    )(dest_flat, h2, init)


def _ffn_kernel(be_ref, first_ref, slot_ref, nxt_ref, hasnext_ref, nused_ref, x_ref, wg_hbm, wu_hbm, wd_hbm,
                y_ref, sg, su, sd, wg16, wu16, wd16, sems):
    i = pl.program_id(0)

    def weight_copies(e, s):
        return (pltpu.make_async_copy(wg_hbm.at[e], sg.at[s], sems.at[s, 0]),
                pltpu.make_async_copy(wu_hbm.at[e], su.at[s], sems.at[s, 1]),
                pltpu.make_async_copy(wd_hbm.at[e], sd.at[s], sems.at[s, 2]))

    @pl.when(first_ref[i] == 1)
    def _():
        s = slot_ref[i]
        e = be_ref[i]

        @pl.when(i == 0)
        def _():
            for cp in weight_copies(e, s):
                cp.start()

        for cp in weight_copies(e, s):
            cp.wait()
        wg16[...] = sg[s].astype(BF16)
        wu16[...] = su[s].astype(BF16)
        wd16[...] = sd[s].astype(BF16)

        @pl.when(hasnext_ref[i] == 1)
        def _():
            for cp in weight_copies(nxt_ref[i], 1 - s):
                cp.start()

    @pl.when(i < nused_ref[0])
    def _():
        xb = _unpack_rows(x_ref[...]).astype(BF16)
        a = jax.nn.silu(_dot(xb, wg16[...])) * _dot(xb, wu16[...])
        y_ref[...] = _pack_rows(_dot(a.astype(BF16), wd16[...]))

    @pl.when(i >= nused_ref[0])
    def _():
        y_ref[...] = jnp.zeros(y_ref.shape, y_ref.dtype)


def _ffn(counts, rows, w_eg, w_eu, w_ed, bm):
    n_pad, dw = rows.shape
    n_experts, d, ff = w_eg.shape
    n_blocks = n_pad // bm
    ends = jnp.cumsum((counts + bm - 1) // bm).astype(jnp.int32)
    n_used = ends[-1]
    bidx = jnp.arange(n_blocks, dtype=jnp.int32)
    be = jnp.minimum(jnp.sum(ends[None, :] <= bidx[:, None], axis=1), n_experts - 1).astype(jnp.int32)
    be = jnp.where(bidx < n_used, be, be[n_used - 1])
    first = jnp.concatenate([jnp.ones((1,), jnp.int32), (be[1:] != be[:-1]).astype(jnp.int32)])
    slot = ((jnp.cumsum(first) - 1) & 1).astype(jnp.int32)
    seg_end = ends[be]
    has_next = (seg_end < n_used).astype(jnp.int32)
    nxt = be[jnp.minimum(seg_end, n_blocks - 1)]
    row = lambda i, *_: (i, 0)
    return pl.pallas_call(
        _ffn_kernel,
        grid_spec=pltpu.PrefetchScalarGridSpec(
            num_scalar_prefetch=6, grid=(n_blocks,),
            in_specs=[pl.BlockSpec((bm, dw), row)] + [pl.BlockSpec(memory_space=pl.ANY)] * 3,
            out_specs=pl.BlockSpec((bm, dw), row),
            scratch_shapes=[pltpu.VMEM((2, d, ff), F32), pltpu.VMEM((2, d, ff), F32), pltpu.VMEM((2, ff, d), F32),
                            pltpu.VMEM((d, ff), BF16), pltpu.VMEM((d, ff), BF16), pltpu.VMEM((ff, d), BF16),
                            pltpu.SemaphoreType.DMA((2, 3))]),
        out_shape=jax.ShapeDtypeStruct((n_pad, dw), jnp.uint32),
        compiler_params=_params("arbitrary"),
        name="ffn",
    )(be, first, slot, nxt, has_next, n_used[None], rows, w_eg, w_eu, w_ed)


def _combine_kernel(dest_ref, yrows_ref, x_ref, g_ref, gate_ref, fw_ref, o_ref, buf, sem, *, tq, t, final):
    base = pl.program_id(0) * tq

    def copy(j, k):
        return pltpu.make_async_copy(yrows_ref.at[pl.ds(dest_ref[k * t + base + j], 1)],
                                     buf.at[k, pl.ds(j, 1)], sem)

    def issue(j, _):
        for k in range(TOP_K):
            copy(j, k).start()
        return 0

    def drain(j, _):
        for k in range(TOP_K):
            copy(j, k).wait()
        return 0

    lax.fori_loop(0, tq, issue, 0, unroll=DMA_UNROLL)
    lax.fori_loop(0, tq, drain, 0, unroll=DMA_UNROLL)
    g = g_ref[...]
    f = g[:, 0:1] * _unpack_rows(buf[0]) + g[:, 1:2] * _unpack_rows(buf[1])
    x2 = x_ref[...] + gate_ref[0] * f
    if final:
        x2 = x2 * lax.rsqrt(jnp.mean(x2 * x2, axis=-1, keepdims=True) + EPS) * fw_ref[...]
    o_ref[...] = x2


def _combine(dest_flat, y_rows, x1, gates_t, ada3, fw, layer, bsz, seqlen, final):
    t, d = x1.shape
    tq = 256
    per_b = seqlen // tq
    return pl.pallas_call(
        functools.partial(_combine_kernel, tq=tq, t=t, final=final),
        grid_spec=pltpu.PrefetchScalarGridSpec(
            num_scalar_prefetch=1, grid=(t // tq,),
            in_specs=[pl.BlockSpec(memory_space=pl.ANY),
                      pl.BlockSpec((tq, d), lambda i, dr: (i, 0)),
                      pl.BlockSpec((tq, TOP_K), lambda i, dr: (i, 0)),
                      pl.BlockSpec((1, 1, d), lambda i, dr: (layer * bsz + i // per_b, 0, 5)),
                      pl.BlockSpec((1, d), lambda i, dr: (0, 0))],
            out_specs=pl.BlockSpec((tq, d), lambda i, dr: (i, 0)),
            scratch_shapes=[pltpu.VMEM((TOP_K, tq, d // 2), jnp.uint32), pltpu.SemaphoreType.DMA]),
        out_shape=jax.ShapeDtypeStruct((t, d), F32),
        compiler_params=_params("arbitrary"),
        name="combine",
    )(dest_flat, y_rows, x1, gates_t, ada3, fw)


def kernel(x, c, w_ada, b_ada, norm1_w, w_in, conv_w, conv_b, dt_bias, a_log, d_ssd, ssd_norm_w,
           s5_lam_re, s5_lam_im, s5_log_dt, s5_b_re, s5_b_im, s5_c_re, s5_c_im, s5_d, w_glu, b_glu,
           s5_norm_w, w_out, norm2_w, w_rg, b_rg, w_re, b_re, w_eg, w_eu, w_ed, final_norm_w):
    bsz, seqlen, d = x.shape
    t = bsz * seqlen
    depth = w_in.shape[0]
    heads = dt_bias.shape[1]
    inner = ssd_norm_w.shape[1]
    cch = conv_w.shape[2]
    n5 = s5_d.shape[1]
    s5_groups, _, s5_h = s5_b_re.shape[1:]
    n_experts = w_re.shape[2]
    moe_block = 256
    n_rows = t * TOP_K
    n_pad = n_rows + n_experts * moe_block
    n_blocks = n_pad // moe_block

    i0, i1, i2 = inner, inner + cch, inner + cch + heads
    w_zx = w_in[..., :i1].astype(BF16)
    n_du = w_in.shape[2] - i1
    w_du = jnp.pad(w_in[..., i1:].astype(BF16), ((0, 0), (0, 0), (0, -n_du % LANES)))
    lane_pad = lambda a: jnp.pad(a, ((0, 0), (0, LANES - a.shape[1])))[:, None, :]
    dtb, alog = lane_pad(dt_bias), lane_pad(a_log)
    dexp = jnp.repeat(d_ssd, inner // heads, axis=1)[:, None, :]
    li = jnp.arange(SSD_CHUNK)
    tri = (li[None, :] <= li[:, None]).astype(BF16)
    e_mat = (jnp.arange(LANES)[:, None] == (jnp.arange(inner) // (inner // heads))[None, :]).astype(BF16)
    w_out_a = w_out[:, :inner].astype(BF16)
    w_out_b = w_out[:, inner:].astype(BF16)
    w_glu_b = w_glu.astype(BF16)
    wr = jnp.concatenate([w_re, w_rg, jnp.zeros((depth, d, LANES - n_experts - N_EXPERT_GROUPS), F32)], axis=-1)
    wr_hi = wr.astype(BF16)
    wr_lo = (wr - wr_hi.astype(F32)).astype(BF16)
    br = jnp.concatenate([b_re, b_rg, jnp.zeros((depth, LANES - n_experts - N_EXPERT_GROUPS), F32)], axis=-1)[:, None, :]
    sub = 256
    si = jnp.arange(sub)
    upper = (si[:, None] < si[None, :]).astype(BF16)

    ada3 = _ada(c, w_ada, b_ada).reshape(depth * bsz, 1, 6 * d)
    x2 = x.reshape(t, d)
    nc5 = seqlen // S5_CHUNK
    for l in range(depth):
        z, xbc, u5, dtp = _inproj(x2, norm1_w[l][None], ada3, w_zx[l], w_du[l], n5, l, bsz, seqlen, inner, heads)
        y_ssd = _ssd(z, xbc, dtp, conv_w[l], conv_b[l][None], dtb[l], alog[l], dexp[l], ssd_norm_w[l][None],
                     tri, e_mat, bsz, seqlen, heads)
        s5w = _s5_weights(s5_lam_re[l], s5_lam_im[l], s5_log_dt[l], s5_b_re[l], s5_b_im[l], s5_c_re[l], s5_c_im[l])
        y5 = _s5_core(u5, *s5w, bsz, seqlen, s5_h)
        x1, h2, logits_t = _outproj(x2, y_ssd, y5, u5, s5_d[l][None], w_glu_b[l], b_glu[l][None],
                                    s5_norm_w[l][None], w_out_a[l], w_out_b[l], norm2_w[l][None], ada3,
                                    wr_hi[l], wr_lo[l], br[l], l, bsz, seqlen)
        dest, gates, counts = _route(logits_t, upper, n_experts, moe_block)
        dest_flat = dest[1].reshape(-1)
        rows = _dispatch(dest_flat, h2, n_pad)
        y_rows = _ffn(counts[:, 0], rows, w_eg[l], w_eu[l], w_ed[l], moe_block)
        x2 = _combine(dest_flat, y_rows, x1, gates[1].T, ada3, final_norm_w[None], l, bsz, seqlen, l == depth - 1)
    return x2.reshape(bsz, seqlen, d)
```

```python
import functools
import math

import jax
import jax.numpy as jnp
from jax import lax
from jax.experimental import pallas as pl
from jax.experimental.pallas import tpu as pltpu

F32 = jnp.float32
BF16 = jnp.bfloat16
EPS = 1e-6

SSD_GROUPS = 2
SSD_STATE = 128
SSD_CHUNK = 128
S5_CHUNK = 16
N_EXPERT_GROUPS = 4
TOP_K = 2
LANES = 128
ROUTER_ROWS = 40
VMEM_LIMIT = 48 * 1024 * 1024


def _dot(a, b):
    return jnp.dot(a, b, preferred_element_type=F32)


def _split2(a):
    hi = a.astype(BF16)
    lo = (a - hi.astype(F32)).astype(BF16)
    return hi, lo


def _split3(a):
    hi = a.astype(BF16)
    r = a - hi.astype(F32)
    mid = r.astype(BF16)
    lo = (r - mid.astype(F32)).astype(BF16)
    return hi, mid, lo


def _pack_rows(x):
    w = x.shape[1] // 2
    bits = lambda v: lax.bitcast_convert_type(v.astype(BF16).astype(F32), jnp.uint32)
    return (bits(x[:, :w]) >> 16) | bits(x[:, w:])


def _unpack_rows(words):
    lo = lax.bitcast_convert_type(words << 16, F32)
    hi = lax.bitcast_convert_type(words & jnp.uint32(0xFFFF0000), F32)
    return jnp.concatenate([lo, hi], axis=1)


def _params(*sem):
    return pltpu.CompilerParams(dimension_semantics=sem, vmem_limit_bytes=VMEM_LIMIT)


def _ada_kernel(c_ref, w_ref, b_ref, o_ref):
    ca = jax.nn.silu(c_ref[...])
    c_hi, c_lo = _split2(ca)
    w_hi, w_lo = _split2(w_ref[0])
    o_ref[0] = _dot(c_hi, w_hi) + _dot(c_lo, w_hi) + _dot(c_hi, w_lo) + b_ref[0]


def _ada(c, w_ada, b_ada):
    n_layers, d, n_out = w_ada.shape
    bsz = c.shape[0]
    tn = 1536
    return pl.pallas_call(
        _ada_kernel,
        grid=(n_layers, n_out // tn),
        in_specs=[pl.BlockSpec((bsz, d), lambda l, j: (0, 0)),
                  pl.BlockSpec((1, d, tn), lambda l, j: (l, 0, j)),
                  pl.BlockSpec((1, 1, tn), lambda l, j: (l, 0, j))],
        out_specs=pl.BlockSpec((1, bsz, tn), lambda l, j: (l, 0, j)),
        out_shape=jax.ShapeDtypeStruct((n_layers, bsz, n_out), F32),
        compiler_params=_params("parallel", "parallel"),
        name="ada",
    )(c, w_ada, b_ada.reshape(n_layers, 1, n_out))


def _inproj_kernel(x_ref, nw_ref, shift_ref, scale_ref, wzx_ref, wdu_ref,
                   z_ref, xbc_ref, u_ref, dt_ref, *, n_z, n_dt):
    x = x_ref[...]
    h = x * lax.rsqrt(jnp.mean(x * x, axis=-1, keepdims=True) + EPS) * nw_ref[...]
    h = (h * (1.0 + scale_ref[0]) + shift_ref[0]).astype(BF16)
    p = _dot(h, wzx_ref[...])
    z_ref[...] = p[:, :n_z]
    xbc_ref[...] = p[:, n_z:]
    q = _dot(h, wdu_ref[...])
    dt_ref[...] = q[:, :LANES]
    u_ref[...] = q[:, n_dt:n_dt + u_ref.shape[1]]


def _inproj(x2, nw, ada3, w_zx, w_du, n_u, layer, bsz, seqlen, n_z, n_dt):
    t, d = x2.shape
    n_xbc = w_zx.shape[1] - n_z
    tm = min(512, seqlen)
    per_b = seqlen // tm
    row = lambda i: (i, 0)
    const = lambda a: pl.BlockSpec(a.shape, lambda i: (0, 0))
    ada_blk = lambda k: pl.BlockSpec((1, 1, d), lambda i: (layer * bsz + i // per_b, 0, k))
    return pl.pallas_call(
        functools.partial(_inproj_kernel, n_z=n_z, n_dt=n_dt),
        grid=(t // tm,),
        in_specs=[pl.BlockSpec((tm, d), row),
                  pl.BlockSpec((1, d), lambda i: (0, 0)),
                  ada_blk(0), ada_blk(1), const(w_zx), const(w_du)],
        out_specs=[pl.BlockSpec((tm, n_z), row), pl.BlockSpec((tm, n_xbc), row),
                   pl.BlockSpec((tm, n_u), row), pl.BlockSpec((tm, LANES), row)],
        out_shape=[jax.ShapeDtypeStruct((t, n_z), F32), jax.ShapeDtypeStruct((t, n_xbc), F32),
                   jax.ShapeDtypeStruct((t, n_u), F32), jax.ShapeDtypeStruct((t, LANES), F32)],
        compiler_params=_params("parallel"),
        name="inproj",
    )(x2, nw, ada3, ada3, w_zx, w_du)


def _ssd_kernel(z_ref, xbc_ref, dt_ref, cw_ref, cb_ref, dtb_ref, alog_ref, dexp_ref, nw_ref,
                tri_ref, e_ref, y_ref, ext_ref, st_ref, *, inner, heads, n_conv):
    L = SSD_CHUNK
    N = SSD_STATE
    hd = inner // heads
    hpg = heads // SSD_GROUPS
    gw = inner // SSD_GROUPS
    tail = 8

    @pl.when(pl.program_id(1) == 0)
    def _():
        ext_ref[0:tail, :] = jnp.zeros((tail, ext_ref.shape[1]), F32)
        st_ref[...] = jnp.zeros(st_ref.shape, F32)

    u_new = xbc_ref[...]
    ext = jnp.concatenate([ext_ref[...], u_new], axis=0)
    conv = cb_ref[...] + cw_ref[n_conv - 1:n_conv, :] * u_new
    for k in range(n_conv - 1):
        conv = conv + cw_ref[k:k + 1, :] * pltpu.roll(ext, n_conv - 1 - k, axis=0)[tail:, :]
    ext_ref[...] = u_new[L - tail:, :]
    xa = jax.nn.silu(conv)
    xs = xa[:, :inner]
    bm = xa[:, inner:inner + SSD_GROUPS * N]
    cm = xa[:, inner + SSD_GROUPS * N:]

    pre = dt_ref[...] + dtb_ref[...]
    dt = jnp.maximum(pre, 0.0) + jnp.log1p(jnp.exp(-jnp.abs(pre)))
    ad = dt * (-jnp.exp(alog_ref[...]))
    tri = tri_ref[...]
    a1, a2, a3 = _split3(ad)
    acs = _dot(tri, a1) + _dot(tri, a2) + _dot(tri, a3)
    acs_t = acs.T
    dt_t = dt.T
    w = dt * jnp.exp(acs[L - 1:L, :] - acs)
    eacs = jnp.exp(acs)
    e_mat = e_ref[...]
    w1, w2 = _split2(w)
    w_exp = _dot(w1, e_mat) + _dot(w2, e_mat)
    q1, q2 = _split2(eacs)
    eacs_exp = _dot(q1, e_mat) + _dot(q2, e_mat)

    row = lax.broadcasted_iota(jnp.int32, (L, L), 0)
    col = lax.broadcasted_iota(jnp.int32, (L, L), 1)
    causal = row >= col
    lane = lax.broadcasted_iota(jnp.int32, (L, 2 * hd), 1)
    h_prev = st_ref[...]

    y_diag, y_off, st_new = [], [], []
    for g in range(SSD_GROUPS):
        bg = bm[:, g * N:(g + 1) * N]
        cg = cm[:, g * N:(g + 1) * N].astype(BF16)
        cb = lax.dot_general(cg, bg.astype(BF16), (((1,), (1,)), ((), ())), preferred_element_type=F32)
        for j in range(hpg // 2):
            h0 = g * hpg + 2 * j
            lms = []
            for h in (h0, h0 + 1):
                seg = jnp.where(causal, acs[:, h:h + 1] - acs_t[h:h + 1, :], -1e30)
                lms.append((cb * jnp.exp(seg) * dt_t[h:h + 1, :]).astype(BF16))
            pair = xs[:, h0 * hd:(h0 + 2) * hd]
            w_bd = jnp.concatenate([jnp.where(lane < hd, pair, 0.0), jnp.where(lane >= hd, pair, 0.0)],
                                   axis=0).astype(BF16)
            y_diag.append(_dot(jnp.concatenate(lms, axis=1), w_bd))
        sl = slice(g * gw, (g + 1) * gw)
        y_off.append(_dot(cg, h_prev[:, sl].astype(BF16)))
        st_new.append(_dot(bg.T.astype(BF16), (xs[:, sl] * w_exp[:, sl]).astype(BF16)))

    y = jnp.concatenate(y_diag, axis=1) + jnp.concatenate(y_off, axis=1) * eacs_exp + dexp_ref[...] * xs
    st_ref[...] = h_prev * eacs_exp[L - 1:L, :] + jnp.concatenate(st_new, axis=1)
    y = y * jax.nn.silu(z_ref[...])
    outs = []
    for g in range(SSD_GROUPS):
        yg = y[:, g * gw:(g + 1) * gw]
        outs.append(yg * lax.rsqrt(jnp.mean(yg * yg, axis=-1, keepdims=True) + EPS))
    y_ref[...] = (jnp.concatenate(outs, axis=1) * nw_ref[...]).astype(y_ref.dtype)


def _ssd(z, xbc, dtp, conv_w, conv_b, dtb, alog, dexp, nw, tri, e_mat, bsz, seqlen, heads):
    t, inner = z.shape
    cch = xbc.shape[1]
    L = SSD_CHUNK
    nc = seqlen // L
    n_conv = conv_w.shape[0]
    row = lambda b, c: (b * nc + c, 0)
    const = lambda shape: pl.BlockSpec(shape, lambda b, c: (0, 0))
    return pl.pallas_call(
        functools.partial(_ssd_kernel, inner=inner, heads=heads, n_conv=n_conv),
        grid=(bsz, nc),
        in_specs=[pl.BlockSpec((L, inner), row), pl.BlockSpec((L, cch), row), pl.BlockSpec((L, LANES), row),
                  const((n_conv, cch)), const((1, cch)), const((1, LANES)), const((1, LANES)),
                  const((1, inner)), const((1, inner)), const((L, L)), const((LANES, inner))],
        out_specs=pl.BlockSpec((L, inner), row),
        out_shape=jax.ShapeDtypeStruct((t, inner), BF16),
        scratch_shapes=[pltpu.VMEM((8, cch), F32), pltpu.VMEM((SSD_STATE, inner), F32)],
        compiler_params=_params("parallel", "arbitrary"),
        name="ssd",
    )(z, xbc, dtp, conv_w, conv_b, dtb, alog, dexp, nw, tri, e_mat)


def _s5_kernel(*refs, groups, gch):
    n_t = groups * gch // LANES
    u_refs = refs[:n_t]
    toep_ref, wsr_ref, wsi_ref, wor_ref, woi_ref, ar_ref, ai_ref = refs[n_t:n_t + 7]
    y_refs = refs[n_t + 7:2 * n_t + 7]
    fold, yfold, sre, sim, pre, pim = refs[2 * n_t + 7:]
    lc = S5_CHUNK
    nc = u_refs[0].shape[0] // lc
    per_tile = LANES // gch
    n_tiles = groups // per_tile
    halves = lc * gch // LANES
    w2 = halves * LANES
    rb = 16
    piece = lax.broadcasted_iota(jnp.int32, (rb, LANES), 1) // gch

    def merge(srcs, shift_of):
        acc = None
        for q, src in enumerate(srcs):
            sh = shift_of(q) * gch
            r = pltpu.roll(src, sh, axis=1) if sh else src
            acc = r if acc is None else jnp.where(piece == q, r, acc)
        return acc

    def fold_rows(b, _):
        r0 = pl.multiple_of(b * rb, rb)
        for j in range(n_tiles):
            tiles = [u_refs[j][pl.ds(b * (rb * lc) + t, rb, stride=lc), :] for t in range(lc)]
            for gp in range(per_tile):
                g = j * per_tile + gp
                parts = [merge(tiles[hv * per_tile:(hv + 1) * per_tile], lambda tt: (tt - gp) % per_tile)
                         for hv in range(halves)]
                fold[g // 2, pl.ds(r0, rb), (g % 2) * w2:(g % 2 + 1) * w2] = (
                    jnp.concatenate(parts, axis=1).astype(BF16))
        return 0

    lax.fori_loop(0, nc // rb, fold_rows, 0)

    n_pairs = groups // 2
    for k in range(n_pairs):
        sre[:, k * LANES:(k + 1) * LANES] = _dot(fold[k], wsr_ref[k])
        sim[:, k * LANES:(k + 1) * LANES] = _dot(fold[k], wsi_ref[k])

    ar = ar_ref[...]
    ai = ai_ref[...]

    def body(c, carry):
        r, i = carry
        pre[pl.ds(c, 1), :] = r
        pim[pl.ds(c, 1), :] = i
        return ar * r - ai * i + sre[pl.ds(c, 1), :], ar * i + ai * r + sim[pl.ds(c, 1), :]

    zero = jnp.zeros((1, sre.shape[1]), F32)
    lax.fori_loop(0, nc, body, (zero, zero))

    for k in range(n_pairs):
        a = fold[k]
        intra = jnp.concatenate([_dot(a[:, :w2], toep_ref[2 * k]), _dot(a[:, w2:], toep_ref[2 * k + 1])], axis=1)
        yfold[k] = (intra + _dot(pre[:, k * LANES:(k + 1) * LANES].astype(BF16), wor_ref[k])
                    + _dot(pim[:, k * LANES:(k + 1) * LANES].astype(BF16), woi_ref[k]))

    def unfold_rows(b, _):
        r0 = pl.multiple_of(b * rb, rb)
        for j in range(n_tiles):
            for hv in range(halves):
                srcs = []
                for gp in range(per_tile):
                    g = j * per_tile + gp
                    srcs.append(yfold[g // 2, pl.ds(r0, rb), pl.ds((g % 2) * w2 + hv * LANES, LANES)])
                for tt in range(per_tile):
                    y_refs[j][pl.ds(b * (rb * lc) + hv * per_tile + tt, rb, stride=lc), :] = merge(
                        srcs, lambda gp: (gp - tt) % per_tile)
        return 0

    lax.fori_loop(0, nc // rb, unfold_rows, 0)


def _s5_core(u5, toep, wsr, wsi, wor, woi, a_re, a_im, layer, bsz, seqlen, gch):
    t, n5 = u5.shape
    groups = n5 // gch
    nc = seqlen // S5_CHUNK
    n_t = n5 // LANES
    const = lambda a: pl.BlockSpec((None,) + a.shape[1:], lambda b: (layer,) + (0,) * (a.ndim - 1),
                                   pipeline_mode=pl.Buffered(1))
    col = lambda j: pl.BlockSpec((seqlen, LANES), lambda b: (b, j))
    return pl.pallas_call(
        functools.partial(_s5_kernel, groups=groups, gch=gch),
        grid=(bsz,),
        in_specs=[col(j) for j in range(n_t)] + [const(toep), const(wsr), const(wsi), const(wor), const(woi),
                                                 const(a_re), const(a_im)],
        out_specs=[pl.BlockSpec((seqlen, LANES), lambda b: (b, 0))] * n_t,
        out_shape=[jax.ShapeDtypeStruct((t, LANES), F32)] * n_t,
        scratch_shapes=[pltpu.VMEM((groups // 2, nc, 2 * S5_CHUNK * gch), BF16),
                        pltpu.VMEM((groups // 2, nc, 2 * S5_CHUNK * gch), F32)]
        + [pltpu.VMEM((nc, a_re.shape[2]), F32)] * 4,
        compiler_params=_params("parallel"),
        name="s5_core",
    )(*([u5] * n_t), toep, wsr, wsi, wor, woi, a_re, a_im)


def _s5_weights(lam_re, lam_im, log_dt, b_re, b_im, c_re, c_im):
    lc = S5_CHUNK
    nl, g, p, h = b_re.shape
    step = jnp.exp(log_dt)[..., None]
    mag = jnp.exp(lam_re * step)
    ab_re, ab_im = mag * jnp.cos(lam_im * step), mag * jnp.sin(lam_im * step)
    den = lam_re * lam_re + lam_im * lam_im
    nr = ab_re - 1.0
    coef_re = (nr * lam_re + ab_im * lam_im) / den
    coef_im = (ab_im * lam_re - nr * lam_im) / den
    swap = lambda a: a.transpose(0, 1, 3, 2)
    bt_re, bt_im = swap(b_re), swap(b_im)
    bbt_re = coef_re[:, :, None, :] * bt_re - coef_im[:, :, None, :] * bt_im
    bbt_im = coef_re[:, :, None, :] * bt_im + coef_im[:, :, None, :] * bt_re
    jj = jnp.arange(lc + 1, dtype=F32)
    pmag = jnp.exp((lam_re * step)[..., None] * jj)
    pw_re = pmag * jnp.cos((lam_im * step)[..., None] * jj)
    pw_im = pmag * jnp.sin((lam_im * step)[..., None] * jj)
    ct_re, ct_im = swap(c_re), swap(c_im)
    wide = lambda a: a.reshape(nl, g, p, (lc + 1) * h)
    cpt_re = wide(ct_re[:, :, :, None, :] * pw_re[..., None] - ct_im[:, :, :, None, :] * pw_im[..., None])
    cpt_im = wide(ct_re[:, :, :, None, :] * pw_im[..., None] + ct_im[:, :, :, None, :] * pw_re[..., None])
    w = lc * h
    k2 = (jnp.einsum('lghp,lgpc->lghc', bbt_re, cpt_re[..., :w], precision='highest')
          - jnp.einsum('lghp,lgpc->lghc', bbt_im, cpt_im[..., :w], precision='highest'))
    toep = jnp.stack([jnp.pad(k2, ((0, 0), (0, 0), (0, 0), (ti * h, 0)))[..., :w] for ti in range(lc)], axis=2)
    toep = toep.reshape(nl, g, w, w)
    rev_re = swap(jnp.flip(pw_re[..., :lc], axis=-1))
    rev_im = swap(jnp.flip(pw_im[..., :lc], axis=-1))
    ws_re = (rev_re[:, :, :, None, :] * bbt_re[:, :, None] - rev_im[:, :, :, None, :] * bbt_im[:, :, None]
             ).reshape(nl, g, w, p)
    ws_im = (rev_re[:, :, :, None, :] * bbt_im[:, :, None] + rev_im[:, :, :, None, :] * bbt_re[:, :, None]
             ).reshape(nl, g, w, p)
    wo_re, wo_im = cpt_re[..., h:], -cpt_im[..., h:]

    def pair(a):
        a = a.reshape((nl, g // 2, 2) + a.shape[2:])
        z = jnp.zeros_like(a[:, :, 0])
        return jnp.concatenate([jnp.concatenate([a[:, :, 0], z], axis=-1),
                                jnp.concatenate([z, a[:, :, 1]], axis=-1)], axis=-2).astype(BF16)

    return (toep.astype(BF16), pair(ws_re), pair(ws_im), pair(wo_re), pair(wo_im),
            pw_re[..., lc].reshape(nl, 1, g * p), pw_im[..., lc].reshape(nl, 1, g * p))


def _outproj_kernel(x_ref, ys_ref, *refs):
    n_t = len(refs) - 17
    y5_refs = refs[:n_t]
    (u5_ref, d5_ref, wg_ref, bg_ref, n5_ref, wa_ref, wb_ref, gate_ref, nw_ref, shift_ref, scale_ref,
     wrh_ref, wrl_ref, br_ref, x1_ref, h2_ref, lg_ref) = refs[n_t:]
    y = jnp.concatenate([r[...] for r in y5_refs], axis=1) + d5_ref[...] * u5_ref[...]
    y = jax.nn.gelu(y)
    y = y * jax.nn.sigmoid(_dot(y.astype(BF16), wg_ref[...]) + bg_ref[...])
    y = y * lax.rsqrt(jnp.mean(y * y, axis=-1, keepdims=True) + EPS) * n5_ref[...]
    m = _dot(ys_ref[...], wa_ref[...]) + _dot(y.astype(BF16), wb_ref[...])
    x1 = x_ref[...] + gate_ref[0] * m
    x1_ref[...] = x1
    h = x1 * lax.rsqrt(jnp.mean(x1 * x1, axis=-1, keepdims=True) + EPS) * nw_ref[...]
    h = h * (1.0 + scale_ref[0]) + shift_ref[0]
    h2_ref[...] = _pack_rows(h)
    h_hi, h_lo = _split2(h)
    wrh = wrh_ref[...]
    lg = _dot(h_hi, wrh) + _dot(h_lo, wrh) + _dot(h_hi, wrl_ref[...]) + br_ref[...]
    lg_ref[...] = lg.T[:ROUTER_ROWS, :]


def _outproj(x2, y_ssd, y5, u5, d5, w_glu, b_glu, n5, w_a, w_b, nw, ada3, wr_hi, wr_lo, br,
             layer, bsz, seqlen):
    t, d = x2.shape
    n_s = y_ssd.shape[1]
    n_5 = u5.shape[1]
    tm = min(512, seqlen)
    per_b = seqlen // tm
    row = lambda i: (i, 0)
    const = lambda a: pl.BlockSpec(a.shape, lambda i: (0, 0))
    ada_blk = lambda k: pl.BlockSpec((1, 1, d), lambda i: (layer * bsz + i // per_b, 0, k))
    return pl.pallas_call(
        _outproj_kernel,
        grid=(t // tm,),
        in_specs=[pl.BlockSpec((tm, d), row), pl.BlockSpec((tm, n_s), row)]
        + [pl.BlockSpec((tm, LANES), row)] * len(y5)
        + [pl.BlockSpec((tm, n_5), row), const(d5), const(w_glu), const(b_glu), const(n5),
                  const(w_a), const(w_b), ada_blk(2), const(nw), ada_blk(3), ada_blk(4),
                  const(wr_hi), const(wr_lo), const(br)],
        out_specs=[pl.BlockSpec((tm, d), row), pl.BlockSpec((tm, d // 2), row),
                   pl.BlockSpec((ROUTER_ROWS, tm), lambda i: (0, i))],
        out_shape=[jax.ShapeDtypeStruct((t, d), F32), jax.ShapeDtypeStruct((t, d // 2), jnp.uint32),
                   jax.ShapeDtypeStruct((ROUTER_ROWS, t), F32)],
        compiler_params=_params("parallel"),
        name="outproj",
    )(x2, y_ssd, *y5, u5, d5, w_glu, b_glu, n5, w_a, w_b, ada3, nw, ada3, ada3, wr_hi, wr_lo, br)


def _route_kernel(lg_ref, upper_ref, dest_ref, gate_ref, cnt_ref, counts, carry, pstart,
                  *, n_experts, per_group, block_rows, sub):
    ph = pl.program_id(0)
    i = pl.program_id(1)
    tr = lg_ref.shape[1]
    lg = lg_ref[...]

    @pl.when((ph == 0) & (i == 0))
    def _():
        counts[...] = jnp.zeros(counts.shape, F32)

    gl = [lg[n_experts + k:n_experts + k + 1, :] for k in range(N_EXPERT_GROUPS)]
    gmax = functools.reduce(jnp.maximum, gl)
    gidx = jnp.full((1, tr), N_EXPERT_GROUPS - 1, jnp.int32)
    for k in range(N_EXPERT_GROUPS - 2, -1, -1):
        gidx = jnp.where(gl[k] == gmax, k, gidx)
    gsum = functools.reduce(lambda a, b: a + b, [jnp.exp(v - gmax) for v in gl])
    g_w = 1.0 / gsum
    el = lg[0:per_group, :]
    for k in range(1, N_EXPERT_GROUPS):
        el = jnp.where(gidx == k, lg[k * per_group:(k + 1) * per_group, :], el)
    ep = jnp.exp(el - jnp.max(el, axis=0, keepdims=True))
    prob = ep / jnp.sum(ep, axis=0, keepdims=True)
    jj = lax.broadcasted_iota(jnp.int32, (per_group, tr), 0).astype(F32)
    p1 = jnp.max(prob, axis=0, keepdims=True)
    i1 = jnp.min(jnp.where(prob == p1, jj, float(per_group)), axis=0, keepdims=True)
    prob2 = jnp.where(jj == i1, -1.0, prob)
    p2 = jnp.max(prob2, axis=0, keepdims=True)
    i2 = jnp.min(jnp.where(prob2 == p2, jj, float(per_group)), axis=0, keepdims=True)
    den = p1 + p2
    gate_ref[0] = jnp.concatenate([g_w * p1 / den, g_w * p2 / den], axis=0)
    e1 = gidx * per_group + i1.astype(jnp.int32)
    e2 = gidx * per_group + i2.astype(jnp.int32)
    rr = lax.broadcasted_iota(jnp.int32, (n_experts, tr), 0)
    oh1 = rr == e1
    oh2 = rr == e2
    member = jnp.where(oh1 | oh2, 1.0, 0.0)

    @pl.when(ph == 0)
    def _():
        counts[...] = counts[...] + jnp.sum(member, axis=1, keepdims=True)
        dest_ref[...] = jnp.zeros(dest_ref.shape, jnp.int32)

    @pl.when(ph == 1)
    def _():
        @pl.when(i == 0)
        def _():
            blocks = (counts[...].astype(jnp.int32) + (block_rows - 1)) >> int(math.log2(block_rows))
            hi = (blocks >> 4).astype(F32).astype(BF16)
            lo = (blocks & 15).astype(F32).astype(BF16)
            er = lax.broadcasted_iota(jnp.int32, (n_experts, n_experts), 0)
            ec = lax.broadcasted_iota(jnp.int32, (n_experts, n_experts), 1)
            lower = jnp.where(ec < er, 1.0, 0.0).astype(BF16)
            pstart[...] = (16.0 * _dot(lower, hi) + _dot(lower, lo)) * float(block_rows)
            carry[...] = jnp.zeros(carry.shape, F32)

        run = carry[...]
        upper = upper_ref[...]
        pieces = []
        for b in range(tr // sub):
            mb = member[:, b * sub:(b + 1) * sub]
            pieces.append(_dot(mb.astype(BF16), upper) + jnp.concatenate([run] * (sub // LANES), axis=1))
            run = run + jnp.sum(mb, axis=1, keepdims=True)
        carry[...] = run
        base = jnp.concatenate(pieces, axis=1) + jnp.concatenate([pstart[...]] * (tr // LANES), axis=1)
        d1 = jnp.sum(jnp.where(oh1, base, 0.0), axis=0, keepdims=True)
        d2 = jnp.sum(jnp.where(oh2, base, 0.0), axis=0, keepdims=True)
        dest_ref[0] = jnp.concatenate([d1, d2], axis=0).astype(jnp.int32)

    cnt_ref[...] = counts[...].astype(jnp.int32)


def _route(logits_t, upper, n_experts, block_rows):
    rows, t = logits_t.shape
    tr = 1024
    sub = upper.shape[0]
    return pl.pallas_call(
        functools.partial(_route_kernel, n_experts=n_experts, per_group=n_experts // N_EXPERT_GROUPS,
                          block_rows=block_rows, sub=sub),
        grid=(2, t // tr),
        in_specs=[pl.BlockSpec((rows, tr), lambda ph, i: (0, i)),
                  pl.BlockSpec(upper.shape, lambda ph, i: (0, 0))],
        out_specs=[pl.BlockSpec((1, TOP_K, tr), lambda ph, i: (ph, 0, i)),
                   pl.BlockSpec((1, TOP_K, tr), lambda ph, i: (ph, 0, i)),
                   pl.BlockSpec((n_experts, LANES), lambda ph, i: (0, 0))],
        out_shape=[jax.ShapeDtypeStruct((2, TOP_K, t), jnp.int32), jax.ShapeDtypeStruct((2, TOP_K, t), F32),
                   jax.ShapeDtypeStruct((n_experts, LANES), jnp.int32)],
        scratch_shapes=[pltpu.VMEM((n_experts, LANES), F32)] * 3,
        compiler_params=_params("arbitrary", "arbitrary"),
        name="route",
    )(logits_t, upper)


DMA_UNROLL = 8


def _dispatch_kernel(dest_ref, h_ref, init_ref, rows_ref, sem, *, tq, t):
    del init_ref
    base = pl.program_id(0) * tq

    def copy(j, k):
        return pltpu.make_async_copy(h_ref.at[pl.ds(j, 1)], rows_ref.at[pl.ds(dest_ref[k * t + base + j], 1)], sem)

    def issue(j, _):
        for k in range(TOP_K):
            copy(j, k).start()
        return 0

    def drain(j, _):
        for k in range(TOP_K):
            copy(j, k).wait()
        return 0

    lax.fori_loop(0, tq, issue, 0, unroll=DMA_UNROLL)
    lax.fori_loop(0, tq, drain, 0, unroll=DMA_UNROLL)


def _dispatch(dest_flat, h2, n_pad):
    t, d = h2.shape
    tq = 256
    init = jnp.zeros((n_pad, d), h2.dtype)
    return pl.pallas_call(
        functools.partial(_dispatch_kernel, tq=tq, t=t),
        grid_spec=pltpu.PrefetchScalarGridSpec(
            num_scalar_prefetch=1, grid=(t // tq,),
            in_specs=[pl.BlockSpec((tq, d), lambda i, dr: (i, 0)), pl.BlockSpec(memory_space=pl.ANY)],
            out_specs=pl.BlockSpec(memory_space=pl.ANY),
            scratch_shapes=[pltpu.SemaphoreType.DMA]),
        out_shape=jax.ShapeDtypeStruct((n_pad, d), h2.dtype),
        input_output_aliases={2: 0},
        compiler_params=_params("arbitrary"),
        name="dispatch",
    )(dest_flat, h2, init)


def _ffn_kernel(be_ref, first_ref, slot_ref, nxt_ref, hasnext_ref, nused_ref, x_ref, wg_hbm, wu_hbm, wd_hbm,
                y_ref, sg, su, sd, wg16, wu16, wd16, sems, *, layer):
    i = pl.program_id(0)

    def weight_copies(e, s):
        return (pltpu.make_async_copy(wg_hbm.at[layer, e], sg.at[s], sems.at[s, 0]),
                pltpu.make_async_copy(wu_hbm.at[layer, e], su.at[s], sems.at[s, 1]),
                pltpu.make_async_copy(wd_hbm.at[layer, e], sd.at[s], sems.at[s, 2]))

    @pl.when(first_ref[i] == 1)
    def _():
        s = slot_ref[i]
        e = be_ref[i]

        @pl.when(i == 0)
        def _():
            for cp in weight_copies(e, s):
                cp.start()

        for cp in weight_copies(e, s):
            cp.wait()
        wg16[...] = sg[s].astype(BF16)
        wu16[...] = su[s].astype(BF16)
        wd16[...] = sd[s].astype(BF16)

        @pl.when(hasnext_ref[i] == 1)
        def _():
            for cp in weight_copies(nxt_ref[i], 1 - s):
                cp.start()

    @pl.when(i < nused_ref[0])
    def _():
        xb = _unpack_rows(x_ref[...]).astype(BF16)
        a = jax.nn.silu(_dot(xb, wg16[...])) * _dot(xb, wu16[...])
        y_ref[...] = _pack_rows(_dot(a.astype(BF16), wd16[...]))

    @pl.when(i >= nused_ref[0])
    def _():
        y_ref[...] = jnp.zeros(y_ref.shape, y_ref.dtype)


def _ffn(counts, rows, w_eg, w_eu, w_ed, layer, bm):
    n_pad, dw = rows.shape
    _, n_experts, d, ff = w_eg.shape
    n_blocks = n_pad // bm
    ends = jnp.cumsum((counts + bm - 1) // bm).astype(jnp.int32)
    n_used = ends[-1]
    bidx = jnp.arange(n_blocks, dtype=jnp.int32)
    be = jnp.minimum(jnp.sum(ends[None, :] <= bidx[:, None], axis=1), n_experts - 1).astype(jnp.int32)
    be = jnp.where(bidx < n_used, be, be[n_used - 1])
    first = jnp.concatenate([jnp.ones((1,), jnp.int32), (be[1:] != be[:-1]).astype(jnp.int32)])
    slot = ((jnp.cumsum(first) - 1) & 1).astype(jnp.int32)
    seg_end = ends[be]
    has_next = (seg_end < n_used).astype(jnp.int32)
    nxt = be[jnp.minimum(seg_end, n_blocks - 1)]
    row = lambda i, *_: (i, 0)
    return pl.pallas_call(
        functools.partial(_ffn_kernel, layer=layer),
        grid_spec=pltpu.PrefetchScalarGridSpec(
            num_scalar_prefetch=6, grid=(n_blocks,),
            in_specs=[pl.BlockSpec((bm, dw), row)] + [pl.BlockSpec(memory_space=pl.ANY)] * 3,
            out_specs=pl.BlockSpec((bm, dw), row),
            scratch_shapes=[pltpu.VMEM((2, d, ff), F32), pltpu.VMEM((2, d, ff), F32), pltpu.VMEM((2, ff, d), F32),
                            pltpu.VMEM((d, ff), BF16), pltpu.VMEM((d, ff), BF16), pltpu.VMEM((ff, d), BF16),
                            pltpu.SemaphoreType.DMA((2, 3))]),
        out_shape=jax.ShapeDtypeStruct((n_pad, dw), jnp.uint32),
        compiler_params=_params("arbitrary"),
        name="ffn",
    )(be, first, slot, nxt, has_next, n_used[None], rows, w_eg, w_eu, w_ed)


def _combine_kernel(dest_ref, yrows_ref, x_ref, g_ref, gate_ref, fw_ref, o_ref, buf, sems, *, tq, t, final):
    i = pl.program_id(0)
    n = pl.num_programs(0)

    def copy(tile, j, k):
        slot = tile % 2
        return pltpu.make_async_copy(yrows_ref.at[pl.ds(dest_ref[k * t + tile * tq + j], 1)],
                                     buf.at[slot, k, pl.ds(j, 1)], sems.at[slot])

    def issue(tile):
        def body(j, _):
            for k in range(TOP_K):
                copy(tile, j, k).start()
            return 0
        lax.fori_loop(0, tq, body, 0, unroll=DMA_UNROLL)

    @pl.when(i == 0)
    def _():
        issue(i)

    @pl.when(i + 1 < n)
    def _():
        issue(i + 1)

    def drain(j, _):
        for k in range(TOP_K):
            copy(i, j, k).wait()
        return 0

    lax.fori_loop(0, tq, drain, 0, unroll=DMA_UNROLL)
    g = g_ref[...]
    rows = buf[i % 2]
    f = g[:, 0:1] * _unpack_rows(rows[0]) + g[:, 1:2] * _unpack_rows(rows[1])
    x2 = x_ref[...] + gate_ref[0] * f
    if final:
        x2 = x2 * lax.rsqrt(jnp.mean(x2 * x2, axis=-1, keepdims=True) + EPS) * fw_ref[...]
    o_ref[...] = x2


def _combine(dest_flat, y_rows, x1, gates_t, ada3, fw, layer, bsz, seqlen, final):
    t, d = x1.shape
    tq = 256
    per_b = seqlen // tq
    return pl.pallas_call(
        functools.partial(_combine_kernel, tq=tq, t=t, final=final),
        grid_spec=pltpu.PrefetchScalarGridSpec(
            num_scalar_prefetch=1, grid=(t // tq,),
            in_specs=[pl.BlockSpec(memory_space=pl.ANY),
                      pl.BlockSpec((tq, d), lambda i, dr: (i, 0)),
                      pl.BlockSpec((tq, TOP_K), lambda i, dr: (i, 0)),
                      pl.BlockSpec((1, 1, d), lambda i, dr: (layer * bsz + i // per_b, 0, 5)),
                      pl.BlockSpec((1, d), lambda i, dr: (0, 0))],
            out_specs=pl.BlockSpec((tq, d), lambda i, dr: (i, 0)),
            scratch_shapes=[pltpu.VMEM((2, TOP_K, tq, d // 2), jnp.uint32), pltpu.SemaphoreType.DMA((2,))]),
        out_shape=jax.ShapeDtypeStruct((t, d), F32),
        compiler_params=_params("arbitrary"),
        name="combine",
    )(dest_flat, y_rows, x1, gates_t, ada3, fw)


def kernel(x, c, w_ada, b_ada, norm1_w, w_in, conv_w, conv_b, dt_bias, a_log, d_ssd, ssd_norm_w,
           s5_lam_re, s5_lam_im, s5_log_dt, s5_b_re, s5_b_im, s5_c_re, s5_c_im, s5_d, w_glu, b_glu,
           s5_norm_w, w_out, norm2_w, w_rg, b_rg, w_re, b_re, w_eg, w_eu, w_ed, final_norm_w):
    bsz, seqlen, d = x.shape
    t = bsz * seqlen
    depth = w_in.shape[0]
    heads = dt_bias.shape[1]
    inner = ssd_norm_w.shape[1]
    cch = conv_w.shape[2]
    n5 = s5_d.shape[1]
    s5_groups, _, s5_h = s5_b_re.shape[1:]
    n_experts = w_re.shape[2]
    moe_block = 256
    n_rows = t * TOP_K
    n_pad = n_rows + n_experts * moe_block
    n_blocks = n_pad // moe_block

    i0, i1, i2 = inner, inner + cch, inner + cch + heads
    w_zx = w_in[..., :i1].astype(BF16)
    n_du = w_in.shape[2] - i1
    w_du = jnp.pad(w_in[..., i1:].astype(BF16), ((0, 0), (0, 0), (0, -n_du % LANES)))
    lane_pad = lambda a: jnp.pad(a, ((0, 0), (0, LANES - a.shape[1])))[:, None, :]
    dtb, alog = lane_pad(dt_bias), lane_pad(a_log)
    dexp = jnp.repeat(d_ssd, inner // heads, axis=1)[:, None, :]
    li = jnp.arange(SSD_CHUNK)
    tri = (li[None, :] <= li[:, None]).astype(BF16)
    e_mat = (jnp.arange(LANES)[:, None] == (jnp.arange(inner) // (inner // heads))[None, :]).astype(BF16)
    w_out_a = w_out[:, :inner].astype(BF16)
    w_out_b = w_out[:, inner:].astype(BF16)
    w_glu_b = w_glu.astype(BF16)
    wr = jnp.concatenate([w_re, w_rg, jnp.zeros((depth, d, LANES - n_experts - N_EXPERT_GROUPS), F32)], axis=-1)
    wr_hi = wr.astype(BF16)
    wr_lo = (wr - wr_hi.astype(F32)).astype(BF16)
    br = jnp.concatenate([b_re, b_rg, jnp.zeros((depth, LANES - n_experts - N_EXPERT_GROUPS), F32)], axis=-1)[:, None, :]
    sub = 256
    si = jnp.arange(sub)
    upper = (si[:, None] < si[None, :]).astype(BF16)

    ada3 = _ada(c, w_ada, b_ada).reshape(depth * bsz, 1, 6 * d)
    x2 = x.reshape(t, d)
    s5w = _s5_weights(s5_lam_re, s5_lam_im, s5_log_dt, s5_b_re, s5_b_im, s5_c_re, s5_c_im)
    for l in range(depth):
        z, xbc, u5, dtp = _inproj(x2, norm1_w[l][None], ada3, w_zx[l], w_du[l], n5, l, bsz, seqlen, inner, heads)
        y_ssd = _ssd(z, xbc, dtp, conv_w[l], conv_b[l][None], dtb[l], alog[l], dexp[l], ssd_norm_w[l][None],
                     tri, e_mat, bsz, seqlen, heads)
        y5 = _s5_core(u5, *s5w, l, bsz, seqlen, s5_h)
        x1, h2, logits_t = _outproj(x2, y_ssd, y5, u5, s5_d[l][None], w_glu_b[l], b_glu[l][None],
                                    s5_norm_w[l][None], w_out_a[l], w_out_b[l], norm2_w[l][None], ada3,
                                    wr_hi[l], wr_lo[l], br[l], l, bsz, seqlen)
        dest, gates, counts = _route(logits_t, upper, n_experts, moe_block)
        dest_flat = dest[1].reshape(-1)
        rows = _dispatch(dest_flat, h2, n_pad)
        y_rows = _ffn(counts[:, 0], rows, w_eg, w_eu, w_ed, l, moe_block)
        x2 = _combine(dest_flat, y_rows, x1, gates[1].T, ada3, final_norm_w[None], l, bsz, seqlen, l == depth - 1)
    return x2.reshape(bsz, seqlen, d)
```

```python
import functools
import math

import jax
import jax.numpy as jnp
from jax import lax
from jax.experimental import pallas as pl
from jax.experimental.pallas import tpu as pltpu

F32 = jnp.float32
BF16 = jnp.bfloat16
EPS = 1e-6

SSD_GROUPS = 2
SSD_STATE = 128
SSD_CHUNK = 128
S5_CHUNK = 16
N_EXPERT_GROUPS = 4
TOP_K = 2
LANES = 128
ROUTER_ROWS = 40
VMEM_LIMIT = 48 * 1024 * 1024


def _dot(a, b):
    return jnp.dot(a, b, preferred_element_type=F32)


def _split2(a):
    hi = a.astype(BF16)
    lo = (a - hi.astype(F32)).astype(BF16)
    return hi, lo


def _split3(a):
    hi = a.astype(BF16)
    r = a - hi.astype(F32)
    mid = r.astype(BF16)
    lo = (r - mid.astype(F32)).astype(BF16)
    return hi, mid, lo


def _pack_rows(x):
    w = x.shape[1] // 2
    bits = lambda v: lax.bitcast_convert_type(v.astype(BF16).astype(F32), jnp.uint32)
    return (bits(x[:, :w]) >> 16) | bits(x[:, w:])


def _unpack_rows(words):
    lo = lax.bitcast_convert_type(words << 16, F32)
    hi = lax.bitcast_convert_type(words & jnp.uint32(0xFFFF0000), F32)
    return jnp.concatenate([lo, hi], axis=1)


def _params(*sem):
    return pltpu.CompilerParams(dimension_semantics=sem, vmem_limit_bytes=VMEM_LIMIT)


def _ada_kernel(c_ref, w_ref, b_ref, o_ref):
    ca = jax.nn.silu(c_ref[...])
    c_hi, c_lo = _split2(ca)
    w_hi, w_lo = _split2(w_ref[0])
    o_ref[0] = _dot(c_hi, w_hi) + _dot(c_lo, w_hi) + _dot(c_hi, w_lo) + b_ref[0]


def _ada(c, w_ada, b_ada):
    n_layers, d, n_out = w_ada.shape
    bsz = c.shape[0]
    tn = 1536
    return pl.pallas_call(
        _ada_kernel,
        grid=(n_layers, n_out // tn),
        in_specs=[pl.BlockSpec((bsz, d), lambda l, j: (0, 0)),
                  pl.BlockSpec((1, d, tn), lambda l, j: (l, 0, j)),
                  pl.BlockSpec((1, 1, tn), lambda l, j: (l, 0, j))],
        out_specs=pl.BlockSpec((1, bsz, tn), lambda l, j: (l, 0, j)),
        out_shape=jax.ShapeDtypeStruct((n_layers, bsz, n_out), F32),
        compiler_params=_params("parallel", "parallel"),
        name="ada",
    )(c, w_ada, b_ada.reshape(n_layers, 1, n_out))


def _inproj_kernel(x_ref, nw_ref, shift_ref, scale_ref, wzx_ref, wdu_ref, cw_ref, cb_ref,
                   z_ref, xa_ref, u_ref, dt_ref, tail_ref, *, n_z, n_dt, per_b):
    tm = x_ref.shape[0]
    tail = tail_ref.shape[0]
    n_conv = cw_ref.shape[0]

    @pl.when(pl.program_id(0) % per_b == 0)
    def _():
        tail_ref[...] = jnp.zeros(tail_ref.shape, F32)

    x = x_ref[...]
    h = x * lax.rsqrt(jnp.mean(x * x, axis=-1, keepdims=True) + EPS) * nw_ref[...]
    h = (h * (1.0 + scale_ref[0]) + shift_ref[0]).astype(BF16)
    u_new = _dot(h, wzx_ref[:, n_z:])
    z_ref[...] = _dot(h, wzx_ref[:, :n_z])
    ext = jnp.concatenate([tail_ref[...], u_new], axis=0)
    conv = cb_ref[...] + cw_ref[n_conv - 1:n_conv, :] * u_new
    for k in range(n_conv - 1):
        conv = conv + cw_ref[k:k + 1, :] * pltpu.roll(ext, n_conv - 1 - k, axis=0)[tail:, :]
    tail_ref[...] = u_new[tm - tail:, :]
    xa_ref[...] = jax.nn.silu(conv)
    q = _dot(h, wdu_ref[...])
    dt_ref[...] = q[:, :LANES]
    u_ref[...] = q[:, n_dt:n_dt + u_ref.shape[1]]


def _inproj(x2, nw, ada3, w_zx, w_du, conv_w, conv_b, n_u, layer, bsz, seqlen, n_z, n_dt):
    t, d = x2.shape
    n_xbc = w_zx.shape[1] - n_z
    tm = min(512, seqlen)
    per_b = seqlen // tm
    row = lambda i: (i, 0)
    const = lambda a: pl.BlockSpec(a.shape, lambda i: (0, 0))
    ada_blk = lambda k: pl.BlockSpec((1, 1, d), lambda i: (layer * bsz + i // per_b, 0, k))
    return pl.pallas_call(
        functools.partial(_inproj_kernel, n_z=n_z, n_dt=n_dt, per_b=per_b),
        grid=(t // tm,),
        in_specs=[pl.BlockSpec((tm, d), row),
                  pl.BlockSpec((1, d), lambda i: (0, 0)),
                  ada_blk(0), ada_blk(1), const(w_zx), const(w_du), const(conv_w), const(conv_b)],
        out_specs=[pl.BlockSpec((tm, n_z), row), pl.BlockSpec((tm, n_xbc), row),
                   pl.BlockSpec((tm, n_u), row), pl.BlockSpec((tm, LANES), row)],
        out_shape=[jax.ShapeDtypeStruct((t, n_z), F32), jax.ShapeDtypeStruct((t, n_xbc), F32),
                   jax.ShapeDtypeStruct((t, n_u), F32), jax.ShapeDtypeStruct((t, LANES), F32)],
        scratch_shapes=[pltpu.VMEM((8, n_xbc), F32)],
        compiler_params=_params("arbitrary"),
        name="inproj",
    )(x2, nw, ada3, ada3, w_zx, w_du, conv_w, conv_b)


def _ssd_kernel(z_ref, xa_ref, dt_ref, dtb_ref, alog_ref, dexp_ref, nw_ref,
                tri_ref, e_ref, y_ref, st_ref, *, inner, heads, chunks):
    @pl.when(pl.program_id(1) == 0)
    def _():
        st_ref[...] = jnp.zeros(st_ref.shape, F32)

    for ci in range(chunks):
        rows = pl.ds(ci * SSD_CHUNK, SSD_CHUNK)
        _ssd_chunk(z_ref[rows, :], xa_ref[rows, :], dt_ref[rows, :], dtb_ref, alog_ref, dexp_ref, nw_ref,
                   tri_ref, e_ref, y_ref.at[rows, :], st_ref, inner=inner, heads=heads)


def _ssd_chunk(z, xa, dt_raw, dtb_ref, alog_ref, dexp_ref, nw_ref, tri_ref, e_ref, y_ref, st_ref, *, inner, heads):
    L = SSD_CHUNK
    N = SSD_STATE
    hd = inner // heads
    hpg = heads // SSD_GROUPS
    gw = inner // SSD_GROUPS
    xs = xa[:, :inner]
    bm = xa[:, inner:inner + SSD_GROUPS * N]
    cm = xa[:, inner + SSD_GROUPS * N:]

    pre = dt_raw + dtb_ref[...]
    dt = jnp.maximum(pre, 0.0) + jnp.log1p(jnp.exp(-jnp.abs(pre)))
    ad = dt * (-jnp.exp(alog_ref[...]))
    tri = tri_ref[...]
    a1, a2, a3 = _split3(ad)
    acs = _dot(tri, a1) + _dot(tri, a2) + _dot(tri, a3)
    acs_t = acs.T
    dt_t = dt.T
    w = dt * jnp.exp(acs[L - 1:L, :] - acs)
    eacs = jnp.exp(acs)
    e_mat = e_ref[...]
    w1, w2 = _split2(w)
    w_exp = _dot(w1, e_mat) + _dot(w2, e_mat)
    q1, q2 = _split2(eacs)
    eacs_exp = _dot(q1, e_mat) + _dot(q2, e_mat)

    row = lax.broadcasted_iota(jnp.int32, (L, L), 0)
    col = lax.broadcasted_iota(jnp.int32, (L, L), 1)
    causal = row >= col
    lane = lax.broadcasted_iota(jnp.int32, (L, 2 * hd), 1)
    h_prev = st_ref[...]

    y_diag, y_off, st_new = [], [], []
    for g in range(SSD_GROUPS):
        bg = bm[:, g * N:(g + 1) * N]
        cg = cm[:, g * N:(g + 1) * N].astype(BF16)
        cb = lax.dot_general(cg, bg.astype(BF16), (((1,), (1,)), ((), ())), preferred_element_type=F32)
        for j in range(hpg // 2):
            h0 = g * hpg + 2 * j
            lms = []
            for h in (h0, h0 + 1):
                seg = jnp.where(causal, acs[:, h:h + 1] - acs_t[h:h + 1, :], -1e30)
                lms.append((cb * jnp.exp(seg) * dt_t[h:h + 1, :]).astype(BF16))
            pair = xs[:, h0 * hd:(h0 + 2) * hd]
            w_bd = jnp.concatenate([jnp.where(lane < hd, pair, 0.0), jnp.where(lane >= hd, pair, 0.0)],
                                   axis=0).astype(BF16)
            y_diag.append(_dot(jnp.concatenate(lms, axis=1), w_bd))
        sl = slice(g * gw, (g + 1) * gw)
        y_off.append(_dot(cg, h_prev[:, sl].astype(BF16)))
        st_new.append(_dot(bg.T.astype(BF16), (xs[:, sl] * w_exp[:, sl]).astype(BF16)))

    y = jnp.concatenate(y_diag, axis=1) + jnp.concatenate(y_off, axis=1) * eacs_exp + dexp_ref[...] * xs
    st_ref[...] = h_prev * eacs_exp[L - 1:L, :] + jnp.concatenate(st_new, axis=1)
    y = y * jax.nn.silu(z)
    outs = []
    for g in range(SSD_GROUPS):
        yg = y[:, g * gw:(g + 1) * gw]
        outs.append(yg * lax.rsqrt(jnp.mean(yg * yg, axis=-1, keepdims=True) + EPS))
    y_ref[...] = (jnp.concatenate(outs, axis=1) * nw_ref[...]).astype(y_ref.dtype)


def _ssd(z, xa, dtp, dtb, alog, dexp, nw, tri, e_mat, bsz, seqlen, heads):
    t, inner = z.shape
    cch = xa.shape[1]
    chunks = 2
    rows = chunks * SSD_CHUNK
    nb = seqlen // rows
    row = lambda b, c: (b * nb + c, 0)
    const = lambda shape: pl.BlockSpec(shape, lambda b, c: (0, 0))
    return pl.pallas_call(
        functools.partial(_ssd_kernel, inner=inner, heads=heads, chunks=chunks),
        grid=(bsz, nb),
        in_specs=[pl.BlockSpec((rows, inner), row), pl.BlockSpec((rows, cch), row), pl.BlockSpec((rows, LANES), row),
                  const((1, LANES)), const((1, LANES)), const((1, inner)), const((1, inner)),
                  const((SSD_CHUNK, SSD_CHUNK)), const((LANES, inner))],
        out_specs=pl.BlockSpec((rows, inner), row),
        out_shape=jax.ShapeDtypeStruct((t, inner), BF16),
        scratch_shapes=[pltpu.VMEM((SSD_STATE, inner), F32)],
        compiler_params=_params("parallel", "arbitrary"),
        name="ssd",
    )(z, xa, dtp, dtb, alog, dexp, nw, tri, e_mat)


def _s5_kernel(*refs, groups, gch):
    n_t = groups * gch // LANES
    u_refs = refs[:n_t]
    toep_ref, wsr_ref, wsi_ref, wor_ref, woi_ref, ar_ref, ai_ref = refs[n_t:n_t + 7]
    y_refs = refs[n_t + 7:2 * n_t + 7]
    fold, yfold, sre, sim, pre, pim = refs[2 * n_t + 7:]
    lc = S5_CHUNK
    nc = u_refs[0].shape[0] // lc
    per_tile = LANES // gch
    n_tiles = groups // per_tile
    halves = lc * gch // LANES
    w2 = halves * LANES
    rb = 16
    piece = lax.broadcasted_iota(jnp.int32, (rb, LANES), 1) // gch

    def merge(srcs, shift_of):
        acc = None
        for q, src in enumerate(srcs):
            sh = shift_of(q) * gch
            r = pltpu.roll(src, sh, axis=1) if sh else src
            acc = r if acc is None else jnp.where(piece == q, r, acc)
        return acc

    def fold_rows(b, _):
        r0 = pl.multiple_of(b * rb, rb)
        for j in range(n_tiles):
            tiles = [u_refs[j][pl.ds(b * (rb * lc) + t, rb, stride=lc), :] for t in range(lc)]
            for gp in range(per_tile):
                g = j * per_tile + gp
                parts = [merge(tiles[hv * per_tile:(hv + 1) * per_tile], lambda tt: (tt - gp) % per_tile)
                         for hv in range(halves)]
                fold[g // 2, pl.ds(r0, rb), (g % 2) * w2:(g % 2 + 1) * w2] = (
                    jnp.concatenate(parts, axis=1).astype(BF16))
        return 0

    lax.fori_loop(0, nc // rb, fold_rows, 0)

    n_pairs = groups // 2
    for k in range(n_pairs):
        sre[:, k * LANES:(k + 1) * LANES] = _dot(fold[k], wsr_ref[k])
        sim[:, k * LANES:(k + 1) * LANES] = _dot(fold[k], wsi_ref[k])

    ar = ar_ref[...]
    ai = ai_ref[...]

    def body(c, carry):
        r, i = carry
        pre[pl.ds(c, 1), :] = r
        pim[pl.ds(c, 1), :] = i
        return ar * r - ai * i + sre[pl.ds(c, 1), :], ar * i + ai * r + sim[pl.ds(c, 1), :]

    zero = jnp.zeros((1, sre.shape[1]), F32)
    lax.fori_loop(0, nc, body, (zero, zero))

    for k in range(n_pairs):
        a = fold[k]
        intra = jnp.concatenate([_dot(a[:, :w2], toep_ref[2 * k]), _dot(a[:, w2:], toep_ref[2 * k + 1])], axis=1)
        yfold[k] = (intra + _dot(pre[:, k * LANES:(k + 1) * LANES].astype(BF16), wor_ref[k])
                    + _dot(pim[:, k * LANES:(k + 1) * LANES].astype(BF16), woi_ref[k]))

    def unfold_rows(b, _):
        r0 = pl.multiple_of(b * rb, rb)
        for j in range(n_tiles):
            for hv in range(halves):
                srcs = []
                for gp in range(per_tile):
                    g = j * per_tile + gp
                    srcs.append(yfold[g // 2, pl.ds(r0, rb), pl.ds((g % 2) * w2 + hv * LANES, LANES)])
                for tt in range(per_tile):
                    y_refs[j][pl.ds(b * (rb * lc) + hv * per_tile + tt, rb, stride=lc), :] = merge(
                        srcs, lambda gp: (gp - tt) % per_tile)
        return 0

    lax.fori_loop(0, nc // rb, unfold_rows, 0)


def _s5_core(u5, toep, wsr, wsi, wor, woi, a_re, a_im, layer, bsz, seqlen, gch):
    t, n5 = u5.shape
    groups = n5 // gch
    nc = seqlen // S5_CHUNK
    n_t = n5 // LANES
    const = lambda a: pl.BlockSpec((None,) + a.shape[1:], lambda b: (layer,) + (0,) * (a.ndim - 1),
                                   pipeline_mode=pl.Buffered(1))
    col = lambda j: pl.BlockSpec((seqlen, LANES), lambda b: (b, j))
    return pl.pallas_call(
        functools.partial(_s5_kernel, groups=groups, gch=gch),
        grid=(bsz,),
        in_specs=[col(j) for j in range(n_t)] + [const(toep), const(wsr), const(wsi), const(wor), const(woi),
                                                 const(a_re), const(a_im)],
        out_specs=[pl.BlockSpec((seqlen, LANES), lambda b: (b, 0))] * n_t,
        out_shape=[jax.ShapeDtypeStruct((t, LANES), F32)] * n_t,
        scratch_shapes=[pltpu.VMEM((groups // 2, nc, 2 * S5_CHUNK * gch), BF16),
                        pltpu.VMEM((groups // 2, nc, 2 * S5_CHUNK * gch), F32)]
        + [pltpu.VMEM((nc, a_re.shape[2]), F32)] * 4,
        compiler_params=_params("parallel"),
        name="s5_core",
    )(*([u5] * n_t), toep, wsr, wsi, wor, woi, a_re, a_im)


def _s5_weights(lam_re, lam_im, log_dt, b_re, b_im, c_re, c_im):
    lc = S5_CHUNK
    nl, g, p, h = b_re.shape
    step = jnp.exp(log_dt)[..., None]
    mag = jnp.exp(lam_re * step)
    ab_re, ab_im = mag * jnp.cos(lam_im * step), mag * jnp.sin(lam_im * step)
    den = lam_re * lam_re + lam_im * lam_im
    nr = ab_re - 1.0
    coef_re = (nr * lam_re + ab_im * lam_im) / den
    coef_im = (ab_im * lam_re - nr * lam_im) / den
    swap = lambda a: a.transpose(0, 1, 3, 2)
    bt_re, bt_im = swap(b_re), swap(b_im)
    bbt_re = coef_re[:, :, None, :] * bt_re - coef_im[:, :, None, :] * bt_im
    bbt_im = coef_re[:, :, None, :] * bt_im + coef_im[:, :, None, :] * bt_re
    jj = jnp.arange(lc + 1, dtype=F32)
    pmag = jnp.exp((lam_re * step)[..., None] * jj)
    pw_re = pmag * jnp.cos((lam_im * step)[..., None] * jj)
    pw_im = pmag * jnp.sin((lam_im * step)[..., None] * jj)
    ct_re, ct_im = swap(c_re), swap(c_im)
    wide = lambda a: a.reshape(nl, g, p, (lc + 1) * h)
    cpt_re = wide(ct_re[:, :, :, None, :] * pw_re[..., None] - ct_im[:, :, :, None, :] * pw_im[..., None])
    cpt_im = wide(ct_re[:, :, :, None, :] * pw_im[..., None] + ct_im[:, :, :, None, :] * pw_re[..., None])
    w = lc * h
    k2 = (jnp.einsum('lghp,lgpc->lghc', bbt_re, cpt_re[..., :w], precision='highest')
          - jnp.einsum('lghp,lgpc->lghc', bbt_im, cpt_im[..., :w], precision='highest'))
    toep = jnp.stack([jnp.pad(k2, ((0, 0), (0, 0), (0, 0), (ti * h, 0)))[..., :w] for ti in range(lc)], axis=2)
    toep = toep.reshape(nl, g, w, w)
    rev_re = swap(jnp.flip(pw_re[..., :lc], axis=-1))
    rev_im = swap(jnp.flip(pw_im[..., :lc], axis=-1))
    ws_re = (rev_re[:, :, :, None, :] * bbt_re[:, :, None] - rev_im[:, :, :, None, :] * bbt_im[:, :, None]
             ).reshape(nl, g, w, p)
    ws_im = (rev_re[:, :, :, None, :] * bbt_im[:, :, None] + rev_im[:, :, :, None, :] * bbt_re[:, :, None]
             ).reshape(nl, g, w, p)
    wo_re, wo_im = cpt_re[..., h:], -cpt_im[..., h:]

    def pair(a):
        a = a.reshape((nl, g // 2, 2) + a.shape[2:])
        z = jnp.zeros_like(a[:, :, 0])
        return jnp.concatenate([jnp.concatenate([a[:, :, 0], z], axis=-1),
                                jnp.concatenate([z, a[:, :, 1]], axis=-1)], axis=-2).astype(BF16)

    return (toep.astype(BF16), pair(ws_re), pair(ws_im), pair(wo_re), pair(wo_im),
            pw_re[..., lc].reshape(nl, 1, g * p), pw_im[..., lc].reshape(nl, 1, g * p))


def _outproj_kernel(x_ref, ys_ref, *refs):
    n_t = len(refs) - 17
    y5_refs = refs[:n_t]
    (u5_ref, d5_ref, wg_ref, bg_ref, n5_ref, wa_ref, wb_ref, gate_ref, nw_ref, shift_ref, scale_ref,
     wrh_ref, wrl_ref, br_ref, x1_ref, h2_ref, lg_ref) = refs[n_t:]
    y = jnp.concatenate([r[...] for r in y5_refs], axis=1) + d5_ref[...] * u5_ref[...]
    y = jax.nn.gelu(y)
    y = y * jax.nn.sigmoid(_dot(y.astype(BF16), wg_ref[...]) + bg_ref[...])
    y = y * lax.rsqrt(jnp.mean(y * y, axis=-1, keepdims=True) + EPS) * n5_ref[...]
    m = _dot(ys_ref[...], wa_ref[...]) + _dot(y.astype(BF16), wb_ref[...])
    x1 = x_ref[...] + gate_ref[0] * m
    x1_ref[...] = x1
    h = x1 * lax.rsqrt(jnp.mean(x1 * x1, axis=-1, keepdims=True) + EPS) * nw_ref[...]
    h = h * (1.0 + scale_ref[0]) + shift_ref[0]
    h2_ref[...] = _pack_rows(h)
    h_hi, h_lo = _split2(h)
    wrh = wrh_ref[...]
    lg = _dot(h_hi, wrh) + _dot(h_lo, wrh) + _dot(h_hi, wrl_ref[...]) + br_ref[...]
    lg_ref[...] = lg.T[:ROUTER_ROWS, :]


def _outproj(x2, y_ssd, y5, u5, d5, w_glu, b_glu, n5, w_a, w_b, nw, ada3, wr_hi, wr_lo, br,
             layer, bsz, seqlen):
    t, d = x2.shape
    n_s = y_ssd.shape[1]
    n_5 = u5.shape[1]
    tm = min(512, seqlen)
    per_b = seqlen // tm
    row = lambda i: (i, 0)
    const = lambda a: pl.BlockSpec(a.shape, lambda i: (0, 0))
    ada_blk = lambda k: pl.BlockSpec((1, 1, d), lambda i: (layer * bsz + i // per_b, 0, k))
    return pl.pallas_call(
        _outproj_kernel,
        grid=(t // tm,),
        in_specs=[pl.BlockSpec((tm, d), row), pl.BlockSpec((tm, n_s), row)]
        + [pl.BlockSpec((tm, LANES), row)] * len(y5)
        + [pl.BlockSpec((tm, n_5), row), const(d5), const(w_glu), const(b_glu), const(n5),
                  const(w_a), const(w_b), ada_blk(2), const(nw), ada_blk(3), ada_blk(4),
                  const(wr_hi), const(wr_lo), const(br)],
        out_specs=[pl.BlockSpec((tm, d), row), pl.BlockSpec((tm, d // 2), row),
                   pl.BlockSpec((ROUTER_ROWS, tm), lambda i: (0, i))],
        out_shape=[jax.ShapeDtypeStruct((t, d), F32), jax.ShapeDtypeStruct((t, d // 2), jnp.uint32),
                   jax.ShapeDtypeStruct((ROUTER_ROWS, t), F32)],
        compiler_params=_params("parallel"),
        name="outproj",
    )(x2, y_ssd, *y5, u5, d5, w_glu, b_glu, n5, w_a, w_b, ada3, nw, ada3, ada3, wr_hi, wr_lo, br)


def _route_kernel(lg_ref, upper_ref, dest_ref, gate_ref, cnt_ref, counts, carry, pstart,
                  *, n_experts, per_group, block_rows, sub):
    ph = pl.program_id(0)
    i = pl.program_id(1)
    tr = lg_ref.shape[1]
    lg = lg_ref[...]

    @pl.when((ph == 0) & (i == 0))
    def _():
        counts[...] = jnp.zeros(counts.shape, F32)

    gl = [lg[n_experts + k:n_experts + k + 1, :] for k in range(N_EXPERT_GROUPS)]
    gmax = functools.reduce(jnp.maximum, gl)
    gidx = jnp.full((1, tr), N_EXPERT_GROUPS - 1, jnp.int32)
    for k in range(N_EXPERT_GROUPS - 2, -1, -1):
        gidx = jnp.where(gl[k] == gmax, k, gidx)
    gsum = functools.reduce(lambda a, b: a + b, [jnp.exp(v - gmax) for v in gl])
    g_w = 1.0 / gsum
    el = lg[0:per_group, :]
    for k in range(1, N_EXPERT_GROUPS):
        el = jnp.where(gidx == k, lg[k * per_group:(k + 1) * per_group, :], el)
    ep = jnp.exp(el - jnp.max(el, axis=0, keepdims=True))
    prob = ep / jnp.sum(ep, axis=0, keepdims=True)
    jj = lax.broadcasted_iota(jnp.int32, (per_group, tr), 0).astype(F32)
    p1 = jnp.max(prob, axis=0, keepdims=True)
    i1 = jnp.min(jnp.where(prob == p1, jj, float(per_group)), axis=0, keepdims=True)
    prob2 = jnp.where(jj == i1, -1.0, prob)
    p2 = jnp.max(prob2, axis=0, keepdims=True)
    i2 = jnp.min(jnp.where(prob2 == p2, jj, float(per_group)), axis=0, keepdims=True)
    den = p1 + p2
    gate_ref[0] = jnp.concatenate([g_w * p1 / den, g_w * p2 / den], axis=0)
    e1 = gidx * per_group + i1.astype(jnp.int32)
    e2 = gidx * per_group + i2.astype(jnp.int32)
    rr = lax.broadcasted_iota(jnp.int32, (n_experts, tr), 0)
    oh1 = rr == e1
    oh2 = rr == e2
    member = jnp.where(oh1 | oh2, 1.0, 0.0)

    @pl.when(ph == 0)
    def _():
        counts[...] = counts[...] + jnp.sum(member, axis=1, keepdims=True)
        dest_ref[...] = jnp.zeros(dest_ref.shape, jnp.int32)

    @pl.when(ph == 1)
    def _():
        @pl.when(i == 0)
        def _():
            blocks = (counts[...].astype(jnp.int32) + (block_rows - 1)) >> int(math.log2(block_rows))
            hi = (blocks >> 4).astype(F32).astype(BF16)
            lo = (blocks & 15).astype(F32).astype(BF16)
            er = lax.broadcasted_iota(jnp.int32, (n_experts, n_experts), 0)
            ec = lax.broadcasted_iota(jnp.int32, (n_experts, n_experts), 1)
            lower = jnp.where(ec < er, 1.0, 0.0).astype(BF16)
            pstart[...] = (16.0 * _dot(lower, hi) + _dot(lower, lo)) * float(block_rows)
            carry[...] = jnp.zeros(carry.shape, F32)

        run = carry[...]
        upper = upper_ref[...]
        pieces = []
        for b in range(tr // sub):
            mb = member[:, b * sub:(b + 1) * sub]
            pieces.append(_dot(mb.astype(BF16), upper) + jnp.concatenate([run] * (sub // LANES), axis=1))
            run = run + jnp.sum(mb, axis=1, keepdims=True)
        carry[...] = run
        base = jnp.concatenate(pieces, axis=1) + jnp.concatenate([pstart[...]] * (tr // LANES), axis=1)
        d1 = jnp.sum(jnp.where(oh1, base, 0.0), axis=0, keepdims=True)
        d2 = jnp.sum(jnp.where(oh2, base, 0.0), axis=0, keepdims=True)
        dest_ref[0] = jnp.concatenate([d1, d2], axis=0).astype(jnp.int32)

    cnt_ref[...] = counts[...].astype(jnp.int32)


def _route(logits_t, upper, n_experts, block_rows):
    rows, t = logits_t.shape
    tr = 1024
    sub = upper.shape[0]
    return pl.pallas_call(
        functools.partial(_route_kernel, n_experts=n_experts, per_group=n_experts // N_EXPERT_GROUPS,
                          block_rows=block_rows, sub=sub),
        grid=(2, t // tr),
        in_specs=[pl.BlockSpec((rows, tr), lambda ph, i: (0, i)),
                  pl.BlockSpec(upper.shape, lambda ph, i: (0, 0))],
        out_specs=[pl.BlockSpec((1, TOP_K, tr), lambda ph, i: (ph, 0, i)),
                   pl.BlockSpec((1, TOP_K, tr), lambda ph, i: (ph, 0, i)),
                   pl.BlockSpec((n_experts, LANES), lambda ph, i: (0, 0))],
        out_shape=[jax.ShapeDtypeStruct((2, TOP_K, t), jnp.int32), jax.ShapeDtypeStruct((2, TOP_K, t), F32),
                   jax.ShapeDtypeStruct((n_experts, LANES), jnp.int32)],
        scratch_shapes=[pltpu.VMEM((n_experts, LANES), F32)] * 3,
        compiler_params=_params("arbitrary", "arbitrary"),
        name="route",
    )(logits_t, upper)


DMA_UNROLL = 8


def _dispatch_kernel(dest_ref, h_ref, init_ref, rows_ref, sem, *, tq, t):
    del init_ref
    base = pl.program_id(0) * tq

    def copy(j, k):
        return pltpu.make_async_copy(h_ref.at[pl.ds(j, 1)], rows_ref.at[pl.ds(dest_ref[k * t + base + j], 1)], sem)

    def issue(j, _):
        for k in range(TOP_K):
            copy(j, k).start(priority=k % 2)
        return 0

    def drain(j, _):
        for k in range(TOP_K):
            copy(j, k).wait()
        return 0

    lax.fori_loop(0, tq, issue, 0, unroll=DMA_UNROLL)
    lax.fori_loop(0, tq, drain, 0, unroll=DMA_UNROLL)


def _dispatch(dest_flat, h2, n_pad):
    t, d = h2.shape
    tq = 256
    init = jnp.zeros((n_pad, d), h2.dtype)
    return pl.pallas_call(
        functools.partial(_dispatch_kernel, tq=tq, t=t),
        grid_spec=pltpu.PrefetchScalarGridSpec(
            num_scalar_prefetch=1, grid=(t // tq,),
            in_specs=[pl.BlockSpec((tq, d), lambda i, dr: (i, 0)), pl.BlockSpec(memory_space=pl.ANY)],
            out_specs=pl.BlockSpec(memory_space=pl.ANY),
            scratch_shapes=[pltpu.SemaphoreType.DMA]),
        out_shape=jax.ShapeDtypeStruct((n_pad, d), h2.dtype),
        input_output_aliases={2: 0},
        compiler_params=_params("arbitrary"),
        name="dispatch",
    )(dest_flat, h2, init)


def _ffn_kernel(be_ref, first_ref, slot_ref, nxt_ref, hasnext_ref, nused_ref, x_ref, wg_hbm, wu_hbm, wd_hbm,
                y_ref, sg, su, sd, wg16, wu16, wd16, sems, *, layer):
    i = pl.program_id(0)

    def weight_copies(e, s):
        return (pltpu.make_async_copy(wg_hbm.at[layer, e], sg.at[s], sems.at[s, 0]),
                pltpu.make_async_copy(wu_hbm.at[layer, e], su.at[s], sems.at[s, 1]),
                pltpu.make_async_copy(wd_hbm.at[layer, e], sd.at[s], sems.at[s, 2]))

    @pl.when(first_ref[i] == 1)
    def _():
        s = slot_ref[i]
        e = be_ref[i]

        @pl.when(i == 0)
        def _():
            for cp in weight_copies(e, s):
                cp.start()

        for cp in weight_copies(e, s):
            cp.wait()
        wg16[...] = sg[s].astype(BF16)
        wu16[...] = su[s].astype(BF16)
        wd16[...] = sd[s].astype(BF16)

        @pl.when(hasnext_ref[i] == 1)
        def _():
            for cp in weight_copies(nxt_ref[i], 1 - s):
                cp.start()

    @pl.when(i < nused_ref[0])
    def _():
        xb = _unpack_rows(x_ref[...]).astype(BF16)
        a = jax.nn.silu(_dot(xb, wg16[...])) * _dot(xb, wu16[...])
        y_ref[...] = _pack_rows(_dot(a.astype(BF16), wd16[...]))

    @pl.when(i >= nused_ref[0])
    def _():
        y_ref[...] = jnp.zeros(y_ref.shape, y_ref.dtype)


def _ffn(counts, rows, w_eg, w_eu, w_ed, layer, bm):
    n_pad, dw = rows.shape
    _, n_experts, d, ff = w_eg.shape
    n_blocks = n_pad // bm
    ends = jnp.cumsum((counts + bm - 1) // bm).astype(jnp.int32)
    n_used = ends[-1]
    bidx = jnp.arange(n_blocks, dtype=jnp.int32)
    be = jnp.minimum(jnp.sum(ends[None, :] <= bidx[:, None], axis=1), n_experts - 1).astype(jnp.int32)
    be = jnp.where(bidx < n_used, be, be[n_used - 1])
    first = jnp.concatenate([jnp.ones((1,), jnp.int32), (be[1:] != be[:-1]).astype(jnp.int32)])
    slot = ((jnp.cumsum(first) - 1) & 1).astype(jnp.int32)
    seg_end = ends[be]
    has_next = (seg_end < n_used).astype(jnp.int32)
    nxt = be[jnp.minimum(seg_end, n_blocks - 1)]
    row = lambda i, *_: (i, 0)
    return pl.pallas_call(
        functools.partial(_ffn_kernel, layer=layer),
        grid_spec=pltpu.PrefetchScalarGridSpec(
            num_scalar_prefetch=6, grid=(n_blocks,),
            in_specs=[pl.BlockSpec((bm, dw), row)] + [pl.BlockSpec(memory_space=pl.ANY)] * 3,
            out_specs=pl.BlockSpec((bm, dw), row),
            scratch_shapes=[pltpu.VMEM((2, d, ff), F32), pltpu.VMEM((2, d, ff), F32), pltpu.VMEM((2, ff, d), F32),
                            pltpu.VMEM((d, ff), BF16), pltpu.VMEM((d, ff), BF16), pltpu.VMEM((ff, d), BF16),
                            pltpu.SemaphoreType.DMA((2, 3))]),
        out_shape=jax.ShapeDtypeStruct((n_pad, dw), jnp.uint32),
        compiler_params=_params("arbitrary"),
        name="ffn",
    )(be, first, slot, nxt, has_next, n_used[None], rows, w_eg, w_eu, w_ed)


def _combine_kernel(dest_ref, yrows_ref, x_ref, g_ref, gate_ref, fw_ref, o_ref, buf, sems, *, tq, t, final):
    i = pl.program_id(0)
    n = pl.num_programs(0)

    def copy(tile, j, k):
        slot = tile % 2
        return pltpu.make_async_copy(yrows_ref.at[pl.ds(dest_ref[k * t + tile * tq + j], 1)],
                                     buf.at[slot, k, pl.ds(j, 1)], sems.at[slot])

    def issue(tile):
        def body(j, _):
            for k in range(TOP_K):
                copy(tile, j, k).start(priority=k % 2)
            return 0
        lax.fori_loop(0, tq, body, 0, unroll=DMA_UNROLL)

    @pl.when(i == 0)
    def _():
        issue(i)

    @pl.when(i + 1 < n)
    def _():
        issue(i + 1)

    def drain(j, _):
        for k in range(TOP_K):
            copy(i, j, k).wait()
        return 0

    lax.fori_loop(0, tq, drain, 0, unroll=DMA_UNROLL)
    g = g_ref[...]
    rows = buf[i % 2]
    f = g[:, 0:1] * _unpack_rows(rows[0]) + g[:, 1:2] * _unpack_rows(rows[1])
    x2 = x_ref[...] + gate_ref[0] * f
    if final:
        x2 = x2 * lax.rsqrt(jnp.mean(x2 * x2, axis=-1, keepdims=True) + EPS) * fw_ref[...]
    o_ref[...] = x2


def _combine(dest_flat, y_rows, x1, gates_t, ada3, fw, layer, bsz, seqlen, final):
    t, d = x1.shape
    tq = 256
    per_b = seqlen // tq
    return pl.pallas_call(
        functools.partial(_combine_kernel, tq=tq, t=t, final=final),
        grid_spec=pltpu.PrefetchScalarGridSpec(
            num_scalar_prefetch=1, grid=(t // tq,),
            in_specs=[pl.BlockSpec(memory_space=pl.ANY),
                      pl.BlockSpec((tq, d), lambda i, dr: (i, 0)),
                      pl.BlockSpec((tq, TOP_K), lambda i, dr: (i, 0)),
                      pl.BlockSpec((1, 1, d), lambda i, dr: (layer * bsz + i // per_b, 0, 5)),
                      pl.BlockSpec((1, d), lambda i, dr: (0, 0))],
            out_specs=pl.BlockSpec((tq, d), lambda i, dr: (i, 0)),
            scratch_shapes=[pltpu.VMEM((2, TOP_K, tq, d // 2), jnp.uint32), pltpu.SemaphoreType.DMA((2,))]),
        out_shape=jax.ShapeDtypeStruct((t, d), F32),
        compiler_params=_params("arbitrary"),
        name="combine",
    )(dest_flat, y_rows, x1, gates_t, ada3, fw)


def kernel(x, c, w_ada, b_ada, norm1_w, w_in, conv_w, conv_b, dt_bias, a_log, d_ssd, ssd_norm_w,
           s5_lam_re, s5_lam_im, s5_log_dt, s5_b_re, s5_b_im, s5_c_re, s5_c_im, s5_d, w_glu, b_glu,
           s5_norm_w, w_out, norm2_w, w_rg, b_rg, w_re, b_re, w_eg, w_eu, w_ed, final_norm_w):
    bsz, seqlen, d = x.shape
    t = bsz * seqlen
    depth = w_in.shape[0]
    heads = dt_bias.shape[1]
    inner = ssd_norm_w.shape[1]
    cch = conv_w.shape[2]
    n5 = s5_d.shape[1]
    s5_groups, _, s5_h = s5_b_re.shape[1:]
    n_experts = w_re.shape[2]
    moe_block = 256
    n_rows = t * TOP_K
    n_pad = n_rows + n_experts * moe_block
    n_blocks = n_pad // moe_block

    i0, i1, i2 = inner, inner + cch, inner + cch + heads
    w_zx = w_in[..., :i1].astype(BF16)
    n_du = w_in.shape[2] - i1
    w_du = jnp.pad(w_in[..., i1:].astype(BF16), ((0, 0), (0, 0), (0, -n_du % LANES)))
    lane_pad = lambda a: jnp.pad(a, ((0, 0), (0, LANES - a.shape[1])))[:, None, :]
    dtb, alog = lane_pad(dt_bias), lane_pad(a_log)
    dexp = jnp.repeat(d_ssd, inner // heads, axis=1)[:, None, :]
    li = jnp.arange(SSD_CHUNK)
    tri = (li[None, :] <= li[:, None]).astype(BF16)
    e_mat = (jnp.arange(LANES)[:, None] == (jnp.arange(inner) // (inner // heads))[None, :]).astype(BF16)
    w_out_a = w_out[:, :inner].astype(BF16)
    w_out_b = w_out[:, inner:].astype(BF16)
    w_glu_b = w_glu.astype(BF16)
    wr = jnp.concatenate([w_re, w_rg, jnp.zeros((depth, d, LANES - n_experts - N_EXPERT_GROUPS), F32)], axis=-1)
    wr_hi = wr.astype(BF16)
    wr_lo = (wr - wr_hi.astype(F32)).astype(BF16)
    br = jnp.concatenate([b_re, b_rg, jnp.zeros((depth, LANES - n_experts - N_EXPERT_GROUPS), F32)], axis=-1)[:, None, :]
    sub = 256
    si = jnp.arange(sub)
    upper = (si[:, None] < si[None, :]).astype(BF16)

    ada3 = _ada(c, w_ada, b_ada).reshape(depth * bsz, 1, 6 * d)
    x2 = x.reshape(t, d)
    s5w = _s5_weights(s5_lam_re, s5_lam_im, s5_log_dt, s5_b_re, s5_b_im, s5_c_re, s5_c_im)
    for l in range(depth):
        z, xa, u5, dtp = _inproj(x2, norm1_w[l][None], ada3, w_zx[l], w_du[l], conv_w[l], conv_b[l][None], n5, l,
                                 bsz, seqlen, inner, heads)
        y_ssd = _ssd(z, xa, dtp, dtb[l], alog[l], dexp[l], ssd_norm_w[l][None], tri, e_mat, bsz, seqlen, heads)
        y5 = _s5_core(u5, *s5w, l, bsz, seqlen, s5_h)
        x1, h2, logits_t = _outproj(x2, y_ssd, y5, u5, s5_d[l][None], w_glu_b[l], b_glu[l][None],
                                    s5_norm_w[l][None], w_out_a[l], w_out_b[l], norm2_w[l][None], ada3,
                                    wr_hi[l], wr_lo[l], br[l], l, bsz, seqlen)
        dest, gates, counts = _route(logits_t, upper, n_experts, moe_block)
        dest_flat = dest[1].reshape(-1)
        rows = _dispatch(dest_flat, h2, n_pad)
        y_rows = _ffn(counts[:, 0], rows, w_eg, w_eu, w_ed, l, moe_block)
        x2 = _combine(dest_flat, y_rows, x1, gates[1].T, ada3, final_norm_w[None], l, bsz, seqlen, l == depth - 1)
    return x2.reshape(bsz, seqlen, d)
```

```python
import functools
import math

import jax
import jax.numpy as jnp
from jax import lax
from jax.experimental import pallas as pl
from jax.experimental.pallas import tpu as pltpu

F32 = jnp.float32
BF16 = jnp.bfloat16
EPS = 1e-6

SSD_GROUPS = 2
SSD_STATE = 128
SSD_CHUNK = 128
S5_CHUNK = 16
N_EXPERT_GROUPS = 4
TOP_K = 2
LANES = 128
ROUTER_ROWS = 40
VMEM_LIMIT = 48 * 1024 * 1024


def _dot(a, b):
    return jnp.dot(a, b, preferred_element_type=F32)


def _split2(a):
    hi = a.astype(BF16)
    lo = (a - hi.astype(F32)).astype(BF16)
    return hi, lo


def _split3(a):
    hi = a.astype(BF16)
    r = a - hi.astype(F32)
    mid = r.astype(BF16)
    lo = (r - mid.astype(F32)).astype(BF16)
    return hi, mid, lo


def _pack_rows(x):
    w = x.shape[1] // 2
    bits = lambda v: lax.bitcast_convert_type(v.astype(BF16).astype(F32), jnp.uint32)
    return (bits(x[:, :w]) >> 16) | bits(x[:, w:])


def _unpack_rows(words):
    lo = lax.bitcast_convert_type(words << 16, F32)
    hi = lax.bitcast_convert_type(words & jnp.uint32(0xFFFF0000), F32)
    return jnp.concatenate([lo, hi], axis=1)


def _params(*sem):
    return pltpu.CompilerParams(dimension_semantics=sem, vmem_limit_bytes=VMEM_LIMIT)


def _ada_kernel(c_ref, w_ref, b_ref, o_ref):
    ca = jax.nn.silu(c_ref[...])
    c_hi, c_lo = _split2(ca)
    w_hi, w_lo = _split2(w_ref[0])
    o_ref[0] = _dot(c_hi, w_hi) + _dot(c_lo, w_hi) + _dot(c_hi, w_lo) + b_ref[0]


def _ada(c, w_ada, b_ada):
    n_layers, d, n_out = w_ada.shape
    bsz = c.shape[0]
    tn = 1536
    return pl.pallas_call(
        _ada_kernel,
        grid=(n_layers, n_out // tn),
        in_specs=[pl.BlockSpec((bsz, d), lambda l, j: (0, 0)),
                  pl.BlockSpec((1, d, tn), lambda l, j: (l, 0, j)),
                  pl.BlockSpec((1, 1, tn), lambda l, j: (l, 0, j))],
        out_specs=pl.BlockSpec((1, bsz, tn), lambda l, j: (l, 0, j)),
        out_shape=jax.ShapeDtypeStruct((n_layers, bsz, n_out), F32),
        compiler_params=_params("parallel", "parallel"),
        name="ada",
    )(c, w_ada, b_ada.reshape(n_layers, 1, n_out))


def _inproj_kernel(x_ref, nw_ref, shift_ref, scale_ref, wzx_ref, wdu_ref, cw_ref, cb_ref,
                   z_ref, xa_ref, u_ref, dt_ref, tail_ref, *, n_z, n_dt, per_b):
    tm = x_ref.shape[0]
    tail = tail_ref.shape[0]
    n_conv = cw_ref.shape[0]

    @pl.when(pl.program_id(0) % per_b == 0)
    def _():
        tail_ref[...] = jnp.zeros(tail_ref.shape, F32)

    x = x_ref[...]
    h = x * lax.rsqrt(jnp.mean(x * x, axis=-1, keepdims=True) + EPS) * nw_ref[...]
    h = (h * (1.0 + scale_ref[0]) + shift_ref[0]).astype(BF16)
    u_new = _dot(h, wzx_ref[:, n_z:])
    z_ref[...] = _dot(h, wzx_ref[:, :n_z])
    ext = jnp.concatenate([tail_ref[...], u_new], axis=0)
    conv = cb_ref[...] + cw_ref[n_conv - 1:n_conv, :] * u_new
    for k in range(n_conv - 1):
        conv = conv + cw_ref[k:k + 1, :] * pltpu.roll(ext, n_conv - 1 - k, axis=0)[tail:, :]
    tail_ref[...] = u_new[tm - tail:, :]
    xa_ref[...] = jax.nn.silu(conv)
    q = _dot(h, wdu_ref[...])
    dt_ref[...] = q[:, :LANES]
    u_ref[...] = q[:, n_dt:n_dt + u_ref.shape[1]]


def _inproj(x2, nw, ada3, w_zx, w_du, conv_w, conv_b, n_u, layer, bsz, seqlen, n_z, n_dt):
    t, d = x2.shape
    n_xbc = w_zx.shape[1] - n_z
    tm = min(512, seqlen)
    per_b = seqlen // tm
    row = lambda i: (i, 0)
    const = lambda a: pl.BlockSpec(a.shape, lambda i: (0, 0))
    ada_blk = lambda k: pl.BlockSpec((1, 1, d), lambda i: (layer * bsz + i // per_b, 0, k))
    return pl.pallas_call(
        functools.partial(_inproj_kernel, n_z=n_z, n_dt=n_dt, per_b=per_b),
        grid=(t // tm,),
        in_specs=[pl.BlockSpec((tm, d), row),
                  pl.BlockSpec((1, d), lambda i: (0, 0)),
                  ada_blk(0), ada_blk(1), const(w_zx), const(w_du), const(conv_w), const(conv_b)],
        out_specs=[pl.BlockSpec((tm, n_z), row), pl.BlockSpec((tm, n_xbc), row),
                   pl.BlockSpec((tm, n_u), row), pl.BlockSpec((tm, LANES), row)],
        out_shape=[jax.ShapeDtypeStruct((t, n_z), F32), jax.ShapeDtypeStruct((t, n_xbc), F32),
                   jax.ShapeDtypeStruct((t, n_u), F32), jax.ShapeDtypeStruct((t, LANES), F32)],
        scratch_shapes=[pltpu.VMEM((8, n_xbc), F32)],
        compiler_params=_params("arbitrary"),
        name="inproj",
    )(x2, nw, ada3, ada3, w_zx, w_du, conv_w, conv_b)


def _ssd_kernel(z_ref, xa_ref, dt_ref, dtb_ref, alog_ref, dexp_ref, nw_ref,
                tri_ref, e_ref, y_ref, st_ref, *, inner, heads, chunks):
    @pl.when(pl.program_id(1) == 0)
    def _():
        st_ref[...] = jnp.zeros(st_ref.shape, F32)

    for ci in range(chunks):
        rows = pl.ds(ci * SSD_CHUNK, SSD_CHUNK)
        _ssd_chunk(z_ref[rows, :], xa_ref[rows, :], dt_ref[rows, :], dtb_ref, alog_ref, dexp_ref, nw_ref,
                   tri_ref, e_ref, y_ref.at[rows, :], st_ref, inner=inner, heads=heads)


def _ssd_chunk(z, xa, dt_raw, dtb_ref, alog_ref, dexp_ref, nw_ref, tri_ref, e_ref, y_ref, st_ref, *, inner, heads):
    L = SSD_CHUNK
    N = SSD_STATE
    hd = inner // heads
    hpg = heads // SSD_GROUPS
    gw = inner // SSD_GROUPS
    xs = xa[:, :inner]
    bm = xa[:, inner:inner + SSD_GROUPS * N]
    cm = xa[:, inner + SSD_GROUPS * N:]

    pre = dt_raw + dtb_ref[...]
    dt = jnp.maximum(pre, 0.0) + jnp.log1p(jnp.exp(-jnp.abs(pre)))
    ad = dt * (-jnp.exp(alog_ref[...]))
    tri = tri_ref[...]
    a1, a2, a3 = _split3(ad)
    acs = _dot(tri, a1) + _dot(tri, a2) + _dot(tri, a3)
    acs_t = acs.T
    dt_t = dt.T
    w = dt * jnp.exp(acs[L - 1:L, :] - acs)
    eacs = jnp.exp(acs)
    e_mat = e_ref[...]
    w1, w2 = _split2(w)
    w_exp = _dot(w1, e_mat) + _dot(w2, e_mat)
    q1, q2 = _split2(eacs)
    eacs_exp = _dot(q1, e_mat) + _dot(q2, e_mat)

    row = lax.broadcasted_iota(jnp.int32, (L, L), 0)
    col = lax.broadcasted_iota(jnp.int32, (L, L), 1)
    causal = row >= col
    lane = lax.broadcasted_iota(jnp.int32, (L, 2 * hd), 1)
    h_prev = st_ref[...]

    y_diag, y_off, st_new = [], [], []
    for g in range(SSD_GROUPS):
        bg = bm[:, g * N:(g + 1) * N]
        cg = cm[:, g * N:(g + 1) * N].astype(BF16)
        cb = lax.dot_general(cg, bg.astype(BF16), (((1,), (1,)), ((), ())), preferred_element_type=F32)
        for j in range(hpg // 2):
            h0 = g * hpg + 2 * j
            lms = []
            for h in (h0, h0 + 1):
                seg = jnp.where(causal, acs[:, h:h + 1] - acs_t[h:h + 1, :], -1e30)
                lms.append((cb * jnp.exp(seg) * dt_t[h:h + 1, :]).astype(BF16))
            pair = xs[:, h0 * hd:(h0 + 2) * hd]
            w_bd = jnp.concatenate([jnp.where(lane < hd, pair, 0.0), jnp.where(lane >= hd, pair, 0.0)],
                                   axis=0).astype(BF16)
            y_diag.append(_dot(jnp.concatenate(lms, axis=1), w_bd))
        sl = slice(g * gw, (g + 1) * gw)
        y_off.append(_dot(cg, h_prev[:, sl].astype(BF16)))
        st_new.append(_dot(bg.T.astype(BF16), (xs[:, sl] * w_exp[:, sl]).astype(BF16)))

    y = jnp.concatenate(y_diag, axis=1) + jnp.concatenate(y_off, axis=1) * eacs_exp + dexp_ref[...] * xs
    st_ref[...] = h_prev * eacs_exp[L - 1:L, :] + jnp.concatenate(st_new, axis=1)
    y = y * jax.nn.silu(z)
    outs = []
    for g in range(SSD_GROUPS):
        yg = y[:, g * gw:(g + 1) * gw]
        outs.append(yg * lax.rsqrt(jnp.mean(yg * yg, axis=-1, keepdims=True) + EPS))
    y_ref[...] = (jnp.concatenate(outs, axis=1) * nw_ref[...]).astype(y_ref.dtype)


def _ssd(z, xa, dtp, dtb, alog, dexp, nw, tri, e_mat, bsz, seqlen, heads):
    t, inner = z.shape
    cch = xa.shape[1]
    chunks = 2
    rows = chunks * SSD_CHUNK
    nb = seqlen // rows
    row = lambda b, c: (b * nb + c, 0)
    const = lambda shape: pl.BlockSpec(shape, lambda b, c: (0, 0))
    return pl.pallas_call(
        functools.partial(_ssd_kernel, inner=inner, heads=heads, chunks=chunks),
        grid=(bsz, nb),
        in_specs=[pl.BlockSpec((rows, inner), row), pl.BlockSpec((rows, cch), row), pl.BlockSpec((rows, LANES), row),
                  const((1, LANES)), const((1, LANES)), const((1, inner)), const((1, inner)),
                  const((SSD_CHUNK, SSD_CHUNK)), const((LANES, inner))],
        out_specs=pl.BlockSpec((rows, inner), row),
        out_shape=jax.ShapeDtypeStruct((t, inner), BF16),
        scratch_shapes=[pltpu.VMEM((SSD_STATE, inner), F32)],
        compiler_params=_params("parallel", "arbitrary"),
        name="ssd",
    )(z, xa, dtp, dtb, alog, dexp, nw, tri, e_mat)


def _s5_kernel(*refs, groups, gch):
    n_t = groups * gch // LANES
    u_refs = refs[:n_t]
    toep_ref, wsr_ref, wsi_ref, wor_ref, woi_ref, ar_ref, ai_ref = refs[n_t:n_t + 7]
    y_refs = refs[n_t + 7:2 * n_t + 7]
    fold, yfold, sre, sim, pre, pim = refs[2 * n_t + 7:]
    lc = S5_CHUNK
    nc = u_refs[0].shape[0] // lc
    per_tile = LANES // gch
    n_tiles = groups // per_tile
    halves = lc * gch // LANES
    w2 = halves * LANES
    rb = 16
    piece = lax.broadcasted_iota(jnp.int32, (rb, LANES), 1) // gch

    def merge(srcs, shift_of):
        acc = None
        for q, src in enumerate(srcs):
            sh = shift_of(q) * gch
            r = pltpu.roll(src, sh, axis=1) if sh else src
            acc = r if acc is None else jnp.where(piece == q, r, acc)
        return acc

    def fold_rows(b, _):
        r0 = pl.multiple_of(b * rb, rb)
        for j in range(n_tiles):
            tiles = [u_refs[j][pl.ds(b * (rb * lc) + t, rb, stride=lc), :] for t in range(lc)]
            for gp in range(per_tile):
                g = j * per_tile + gp
                parts = [merge(tiles[hv * per_tile:(hv + 1) * per_tile], lambda tt: (tt - gp) % per_tile)
                         for hv in range(halves)]
                fold[g // 2, pl.ds(r0, rb), (g % 2) * w2:(g % 2 + 1) * w2] = (
                    jnp.concatenate(parts, axis=1).astype(BF16))
        return 0

    lax.fori_loop(0, nc // rb, fold_rows, 0)

    n_pairs = groups // 2
    for k in range(n_pairs):
        sre[:, k * LANES:(k + 1) * LANES] = _dot(fold[k], wsr_ref[k])
        sim[:, k * LANES:(k + 1) * LANES] = _dot(fold[k], wsi_ref[k])

    ar = ar_ref[...]
    ai = ai_ref[...]

    def body(c, carry):
        r, i = carry
        pre[pl.ds(c, 1), :] = r
        pim[pl.ds(c, 1), :] = i
        return ar * r - ai * i + sre[pl.ds(c, 1), :], ar * i + ai * r + sim[pl.ds(c, 1), :]

    zero = jnp.zeros((1, sre.shape[1]), F32)
    lax.fori_loop(0, nc, body, (zero, zero))

    for k in range(n_pairs):
        a = fold[k]
        intra = jnp.concatenate([_dot(a[:, :w2], toep_ref[2 * k]), _dot(a[:, w2:], toep_ref[2 * k + 1])], axis=1)
        yfold[k] = (intra + _dot(pre[:, k * LANES:(k + 1) * LANES].astype(BF16), wor_ref[k])
                    + _dot(pim[:, k * LANES:(k + 1) * LANES].astype(BF16), woi_ref[k]))

    def unfold_rows(b, _):
        r0 = pl.multiple_of(b * rb, rb)
        for j in range(n_tiles):
            for hv in range(halves):
                srcs = []
                for gp in range(per_tile):
                    g = j * per_tile + gp
                    srcs.append(yfold[g // 2, pl.ds(r0, rb), pl.ds((g % 2) * w2 + hv * LANES, LANES)])
                for tt in range(per_tile):
                    y_refs[j][pl.ds(b * (rb * lc) + hv * per_tile + tt, rb, stride=lc), :] = merge(
                        srcs, lambda gp: (gp - tt) % per_tile)
        return 0

    lax.fori_loop(0, nc // rb, unfold_rows, 0)


def _s5_core(u5, toep, wsr, wsi, wor, woi, a_re, a_im, layer, bsz, seqlen, gch):
    t, n5 = u5.shape
    groups = n5 // gch
    nc = seqlen // S5_CHUNK
    n_t = n5 // LANES
    const = lambda a: pl.BlockSpec((None,) + a.shape[1:], lambda b: (layer,) + (0,) * (a.ndim - 1),
                                   pipeline_mode=pl.Buffered(1))
    col = lambda j: pl.BlockSpec((seqlen, LANES), lambda b: (b, j))
    return pl.pallas_call(
        functools.partial(_s5_kernel, groups=groups, gch=gch),
        grid=(bsz,),
        in_specs=[col(j) for j in range(n_t)] + [const(toep), const(wsr), const(wsi), const(wor), const(woi),
                                                 const(a_re), const(a_im)],
        out_specs=[pl.BlockSpec((seqlen, LANES), lambda b: (b, 0))] * n_t,
        out_shape=[jax.ShapeDtypeStruct((t, LANES), F32)] * n_t,
        scratch_shapes=[pltpu.VMEM((groups // 2, nc, 2 * S5_CHUNK * gch), BF16),
                        pltpu.VMEM((groups // 2, nc, 2 * S5_CHUNK * gch), F32)]
        + [pltpu.VMEM((nc, a_re.shape[2]), F32)] * 4,
        compiler_params=_params("parallel"),
        name="s5_core",
    )(*([u5] * n_t), toep, wsr, wsi, wor, woi, a_re, a_im)


def _s5_weights(lam_re, lam_im, log_dt, b_re, b_im, c_re, c_im):
    lc = S5_CHUNK
    nl, g, p, h = b_re.shape
    step = jnp.exp(log_dt)[..., None]
    mag = jnp.exp(lam_re * step)
    ab_re, ab_im = mag * jnp.cos(lam_im * step), mag * jnp.sin(lam_im * step)
    den = lam_re * lam_re + lam_im * lam_im
    nr = ab_re - 1.0
    coef_re = (nr * lam_re + ab_im * lam_im) / den
    coef_im = (ab_im * lam_re - nr * lam_im) / den
    swap = lambda a: a.transpose(0, 1, 3, 2)
    bt_re, bt_im = swap(b_re), swap(b_im)
    bbt_re = coef_re[:, :, None, :] * bt_re - coef_im[:, :, None, :] * bt_im
    bbt_im = coef_re[:, :, None, :] * bt_im + coef_im[:, :, None, :] * bt_re
    jj = jnp.arange(lc + 1, dtype=F32)
    pmag = jnp.exp((lam_re * step)[..., None] * jj)
    pw_re = pmag * jnp.cos((lam_im * step)[..., None] * jj)
    pw_im = pmag * jnp.sin((lam_im * step)[..., None] * jj)
    ct_re, ct_im = swap(c_re), swap(c_im)
    wide = lambda a: a.reshape(nl, g, p, (lc + 1) * h)
    cpt_re = wide(ct_re[:, :, :, None, :] * pw_re[..., None] - ct_im[:, :, :, None, :] * pw_im[..., None])
    cpt_im = wide(ct_re[:, :, :, None, :] * pw_im[..., None] + ct_im[:, :, :, None, :] * pw_re[..., None])
    w = lc * h
    k2 = (jnp.einsum('lghp,lgpc->lghc', bbt_re, cpt_re[..., :w], precision='highest')
          - jnp.einsum('lghp,lgpc->lghc', bbt_im, cpt_im[..., :w], precision='highest'))
    toep = jnp.stack([jnp.pad(k2, ((0, 0), (0, 0), (0, 0), (ti * h, 0)))[..., :w] for ti in range(lc)], axis=2)
    toep = toep.reshape(nl, g, w, w)
    rev_re = swap(jnp.flip(pw_re[..., :lc], axis=-1))
    rev_im = swap(jnp.flip(pw_im[..., :lc], axis=-1))
    ws_re = (rev_re[:, :, :, None, :] * bbt_re[:, :, None] - rev_im[:, :, :, None, :] * bbt_im[:, :, None]
             ).reshape(nl, g, w, p)
    ws_im = (rev_re[:, :, :, None, :] * bbt_im[:, :, None] + rev_im[:, :, :, None, :] * bbt_re[:, :, None]
             ).reshape(nl, g, w, p)
    wo_re, wo_im = cpt_re[..., h:], -cpt_im[..., h:]

    def pair(a):
        a = a.reshape((nl, g // 2, 2) + a.shape[2:])
        z = jnp.zeros_like(a[:, :, 0])
        return jnp.concatenate([jnp.concatenate([a[:, :, 0], z], axis=-1),
                                jnp.concatenate([z, a[:, :, 1]], axis=-1)], axis=-2).astype(BF16)

    return (toep.astype(BF16), pair(ws_re), pair(ws_im), pair(wo_re), pair(wo_im),
            pw_re[..., lc].reshape(nl, 1, g * p), pw_im[..., lc].reshape(nl, 1, g * p))


def _outproj_kernel(x_ref, ys_ref, *refs):
    n_t = len(refs) - 17
    y5_refs = refs[:n_t]
    (u5_ref, d5_ref, wg_ref, bg_ref, n5_ref, wa_ref, wb_ref, gate_ref, nw_ref, shift_ref, scale_ref,
     wrh_ref, wrl_ref, br_ref, x1_ref, h2_ref, lg_ref) = refs[n_t:]
    y = jnp.concatenate([r[...] for r in y5_refs], axis=1) + d5_ref[...] * u5_ref[...]
    y = jax.nn.gelu(y)
    y = y * jax.nn.sigmoid(_dot(y.astype(BF16), wg_ref[...]) + bg_ref[...])
    y = y * lax.rsqrt(jnp.mean(y * y, axis=-1, keepdims=True) + EPS) * n5_ref[...]
    m = _dot(ys_ref[...], wa_ref[...]) + _dot(y.astype(BF16), wb_ref[...])
    x1 = x_ref[...] + gate_ref[0] * m
    x1_ref[...] = x1
    h = x1 * lax.rsqrt(jnp.mean(x1 * x1, axis=-1, keepdims=True) + EPS) * nw_ref[...]
    h = h * (1.0 + scale_ref[0]) + shift_ref[0]
    h2_ref[...] = _pack_rows(h)
    h_hi, h_lo = _split2(h)
    wrh = wrh_ref[...]
    lg = _dot(h_hi, wrh) + _dot(h_lo, wrh) + _dot(h_hi, wrl_ref[...]) + br_ref[...]
    lg_ref[...] = lg.T[:ROUTER_ROWS, :]


def _outproj(x2, y_ssd, y5, u5, d5, w_glu, b_glu, n5, w_a, w_b, nw, ada3, wr_hi, wr_lo, br,
             layer, bsz, seqlen):
    t, d = x2.shape
    n_s = y_ssd.shape[1]
    n_5 = u5.shape[1]
    tm = min(512, seqlen)
    per_b = seqlen // tm
    row = lambda i: (i, 0)
    const = lambda a: pl.BlockSpec(a.shape, lambda i: (0, 0))
    ada_blk = lambda k: pl.BlockSpec((1, 1, d), lambda i: (layer * bsz + i // per_b, 0, k))
    return pl.pallas_call(
        _outproj_kernel,
        grid=(t // tm,),
        in_specs=[pl.BlockSpec((tm, d), row), pl.BlockSpec((tm, n_s), row)]
        + [pl.BlockSpec((tm, LANES), row)] * len(y5)
        + [pl.BlockSpec((tm, n_5), row), const(d5), const(w_glu), const(b_glu), const(n5),
                  const(w_a), const(w_b), ada_blk(2), const(nw), ada_blk(3), ada_blk(4),
                  const(wr_hi), const(wr_lo), const(br)],
        out_specs=[pl.BlockSpec((tm, d), row), pl.BlockSpec((tm, d // 2), row),
                   pl.BlockSpec((ROUTER_ROWS, tm), lambda i: (0, i))],
        out_shape=[jax.ShapeDtypeStruct((t, d), F32), jax.ShapeDtypeStruct((t, d // 2), jnp.uint32),
                   jax.ShapeDtypeStruct((ROUTER_ROWS, t), F32)],
        compiler_params=_params("parallel"),
        name="outproj",
    )(x2, y_ssd, *y5, u5, d5, w_glu, b_glu, n5, w_a, w_b, ada3, nw, ada3, ada3, wr_hi, wr_lo, br)


def _route_kernel(lg_ref, upper_ref, dest_ref, gate_ref, cnt_ref, counts, carry, pstart,
                  *, n_experts, per_group, block_rows, sub):
    ph = pl.program_id(0)
    i = pl.program_id(1)
    tr = lg_ref.shape[1]
    lg = lg_ref[...]

    @pl.when((ph == 0) & (i == 0))
    def _():
        counts[...] = jnp.zeros(counts.shape, F32)

    gl = [lg[n_experts + k:n_experts + k + 1, :] for k in range(N_EXPERT_GROUPS)]
    gmax = functools.reduce(jnp.maximum, gl)
    gidx = jnp.full((1, tr), N_EXPERT_GROUPS - 1, jnp.int32)
    for k in range(N_EXPERT_GROUPS - 2, -1, -1):
        gidx = jnp.where(gl[k] == gmax, k, gidx)
    gsum = functools.reduce(lambda a, b: a + b, [jnp.exp(v - gmax) for v in gl])
    g_w = 1.0 / gsum
    el = lg[0:per_group, :]
    for k in range(1, N_EXPERT_GROUPS):
        el = jnp.where(gidx == k, lg[k * per_group:(k + 1) * per_group, :], el)
    ep = jnp.exp(el - jnp.max(el, axis=0, keepdims=True))
    prob = ep / jnp.sum(ep, axis=0, keepdims=True)
    jj = lax.broadcasted_iota(jnp.int32, (per_group, tr), 0).astype(F32)
    p1 = jnp.max(prob, axis=0, keepdims=True)
    i1 = jnp.min(jnp.where(prob == p1, jj, float(per_group)), axis=0, keepdims=True)
    prob2 = jnp.where(jj == i1, -1.0, prob)
    p2 = jnp.max(prob2, axis=0, keepdims=True)
    i2 = jnp.min(jnp.where(prob2 == p2, jj, float(per_group)), axis=0, keepdims=True)
    den = p1 + p2
    gate_ref[0] = jnp.concatenate([g_w * p1 / den, g_w * p2 / den], axis=0)
    e1 = gidx * per_group + i1.astype(jnp.int32)
    e2 = gidx * per_group + i2.astype(jnp.int32)
    rr = lax.broadcasted_iota(jnp.int32, (n_experts, tr), 0)
    oh1 = rr == e1
    oh2 = rr == e2
    member = jnp.where(oh1 | oh2, 1.0, 0.0)

    @pl.when(ph == 0)
    def _():
        counts[...] = counts[...] + jnp.sum(member, axis=1, keepdims=True)
        dest_ref[...] = jnp.zeros(dest_ref.shape, jnp.int32)

    @pl.when(ph == 1)
    def _():
        @pl.when(i == 0)
        def _():
            blocks = (counts[...].astype(jnp.int32) + (block_rows - 1)) >> int(math.log2(block_rows))
            hi = (blocks >> 4).astype(F32).astype(BF16)
            lo = (blocks & 15).astype(F32).astype(BF16)
            er = lax.broadcasted_iota(jnp.int32, (n_experts, n_experts), 0)
            ec = lax.broadcasted_iota(jnp.int32, (n_experts, n_experts), 1)
            lower = jnp.where(ec < er, 1.0, 0.0).astype(BF16)
            pstart[...] = (16.0 * _dot(lower, hi) + _dot(lower, lo)) * float(block_rows)
            carry[...] = jnp.zeros(carry.shape, F32)

        run = carry[...]
        upper = upper_ref[...]
        pieces = []
        for b in range(tr // sub):
            mb = member[:, b * sub:(b + 1) * sub]
            pieces.append(_dot(mb.astype(BF16), upper) + jnp.concatenate([run] * (sub // LANES), axis=1))
            run = run + jnp.sum(mb, axis=1, keepdims=True)
        carry[...] = run
        base = jnp.concatenate(pieces, axis=1) + jnp.concatenate([pstart[...]] * (tr // LANES), axis=1)
        d1 = jnp.sum(jnp.where(oh1, base, 0.0), axis=0, keepdims=True)
        d2 = jnp.sum(jnp.where(oh2, base, 0.0), axis=0, keepdims=True)
        dest_ref[0] = jnp.concatenate([d1, d2], axis=0).astype(jnp.int32)

    cnt_ref[...] = counts[...].astype(jnp.int32)


def _route(logits_t, upper, n_experts, block_rows):
    rows, t = logits_t.shape
    tr = 1024
    sub = upper.shape[0]
    return pl.pallas_call(
        functools.partial(_route_kernel, n_experts=n_experts, per_group=n_experts // N_EXPERT_GROUPS,
                          block_rows=block_rows, sub=sub),
        grid=(2, t // tr),
        in_specs=[pl.BlockSpec((rows, tr), lambda ph, i: (0, i)),
                  pl.BlockSpec(upper.shape, lambda ph, i: (0, 0))],
        out_specs=[pl.BlockSpec((1, TOP_K, tr), lambda ph, i: (ph, 0, i)),
                   pl.BlockSpec((1, TOP_K, tr), lambda ph, i: (ph, 0, i)),
                   pl.BlockSpec((n_experts, LANES), lambda ph, i: (0, 0))],
        out_shape=[jax.ShapeDtypeStruct((2, TOP_K, t), jnp.int32), jax.ShapeDtypeStruct((2, TOP_K, t), F32),
                   jax.ShapeDtypeStruct((n_experts, LANES), jnp.int32)],
        scratch_shapes=[pltpu.VMEM((n_experts, LANES), F32)] * 3,
        compiler_params=_params("arbitrary", "arbitrary"),
        name="route",
    )(logits_t, upper)


DMA_UNROLL = 8


def _dispatch_kernel(dest_ref, h_ref, init_ref, rows_ref, sem, *, tq, t):
    del init_ref
    base = pl.program_id(0) * tq

    def issue(jb, _):
        j0 = pl.multiple_of(jb * DMA_UNROLL, DMA_UNROLL)
        for u in range(DMA_UNROLL):
            for k in range(TOP_K):
                pltpu.make_async_copy(h_ref.at[pl.ds(j0 + u, 1)],
                                      rows_ref.at[pl.ds(dest_ref[k * t + base + j0 + u], 1)], sem).start(priority=k % 2)
        return 0

    lax.fori_loop(0, tq // DMA_UNROLL, issue, 0)
    for k in range(TOP_K):
        pltpu.make_async_copy(h_ref, rows_ref.at[pl.ds(0, tq)], sem).wait()


def _dispatch(dest_flat, h2, n_pad):
    t, d = h2.shape
    tq = 256
    init = jnp.zeros((n_pad, d), h2.dtype)
    return pl.pallas_call(
        functools.partial(_dispatch_kernel, tq=tq, t=t),
        grid_spec=pltpu.PrefetchScalarGridSpec(
            num_scalar_prefetch=1, grid=(t // tq,),
            in_specs=[pl.BlockSpec((tq, d), lambda i, dr: (i, 0)), pl.BlockSpec(memory_space=pl.ANY)],
            out_specs=pl.BlockSpec(memory_space=pl.ANY),
            scratch_shapes=[pltpu.SemaphoreType.DMA]),
        out_shape=jax.ShapeDtypeStruct((n_pad, d), h2.dtype),
        input_output_aliases={2: 0},
        compiler_params=_params("arbitrary"),
        name="dispatch",
    )(dest_flat, h2, init)


def _ffn_kernel(be_ref, first_ref, slot_ref, nxt_ref, hasnext_ref, nused_ref, x_ref, wg_hbm, wu_hbm, wd_hbm,
                y_ref, sg, su, sd, wg16, wu16, wd16, sems, *, layer):
    i = pl.program_id(0)

    def weight_copies(e, s):
        return (pltpu.make_async_copy(wg_hbm.at[layer, e], sg.at[s], sems.at[s, 0]),
                pltpu.make_async_copy(wu_hbm.at[layer, e], su.at[s], sems.at[s, 1]),
                pltpu.make_async_copy(wd_hbm.at[layer, e], sd.at[s], sems.at[s, 2]))

    @pl.when(first_ref[i] == 1)
    def _():
        s = slot_ref[i]
        e = be_ref[i]

        @pl.when(i == 0)
        def _():
            for cp in weight_copies(e, s):
                cp.start()

        for cp in weight_copies(e, s):
            cp.wait()
        wg16[...] = sg[s].astype(BF16)
        wu16[...] = su[s].astype(BF16)
        wd16[...] = sd[s].astype(BF16)

        @pl.when(hasnext_ref[i] == 1)
        def _():
            for cp in weight_copies(nxt_ref[i], 1 - s):
                cp.start()

    @pl.when(i < nused_ref[0])
    def _():
        xb = _unpack_rows(x_ref[...]).astype(BF16)
        a = jax.nn.silu(_dot(xb, wg16[...])) * _dot(xb, wu16[...])
        y_ref[...] = _pack_rows(_dot(a.astype(BF16), wd16[...]))

    @pl.when(i >= nused_ref[0])
    def _():
        y_ref[...] = jnp.zeros(y_ref.shape, y_ref.dtype)


def _ffn(counts, rows, w_eg, w_eu, w_ed, layer, bm):
    n_pad, dw = rows.shape
    _, n_experts, d, ff = w_eg.shape
    n_blocks = n_pad // bm
    ends = jnp.cumsum((counts + bm - 1) // bm).astype(jnp.int32)
    n_used = ends[-1]
    bidx = jnp.arange(n_blocks, dtype=jnp.int32)
    be = jnp.minimum(jnp.sum(ends[None, :] <= bidx[:, None], axis=1), n_experts - 1).astype(jnp.int32)
    be = jnp.where(bidx < n_used, be, be[n_used - 1])
    first = jnp.concatenate([jnp.ones((1,), jnp.int32), (be[1:] != be[:-1]).astype(jnp.int32)])
    slot = ((jnp.cumsum(first) - 1) & 1).astype(jnp.int32)
    seg_end = ends[be]
    has_next = (seg_end < n_used).astype(jnp.int32)
    nxt = be[jnp.minimum(seg_end, n_blocks - 1)]
    row = lambda i, *_: (i, 0)
    return pl.pallas_call(
        functools.partial(_ffn_kernel, layer=layer),
        grid_spec=pltpu.PrefetchScalarGridSpec(
            num_scalar_prefetch=6, grid=(n_blocks,),
            in_specs=[pl.BlockSpec((bm, dw), row)] + [pl.BlockSpec(memory_space=pl.ANY)] * 3,
            out_specs=pl.BlockSpec((bm, dw), row),
            scratch_shapes=[pltpu.VMEM((2, d, ff), F32), pltpu.VMEM((2, d, ff), F32), pltpu.VMEM((2, ff, d), F32),
                            pltpu.VMEM((d, ff), BF16), pltpu.VMEM((d, ff), BF16), pltpu.VMEM((ff, d), BF16),
                            pltpu.SemaphoreType.DMA((2, 3))]),
        out_shape=jax.ShapeDtypeStruct((n_pad, dw), jnp.uint32),
        compiler_params=_params("arbitrary"),
        name="ffn",
    )(be, first, slot, nxt, has_next, n_used[None], rows, w_eg, w_eu, w_ed)


def _combine_kernel(dest_ref, yrows_ref, x_ref, g_ref, gate_ref, fw_ref, o_ref, buf, sems, *, tq, t, final):
    i = pl.program_id(0)
    n = pl.num_programs(0)

    def issue(tile):
        slot = tile % 2

        def body(jb, _):
            j0 = pl.multiple_of(jb * DMA_UNROLL, DMA_UNROLL)
            for u in range(DMA_UNROLL):
                for k in range(TOP_K):
                    pltpu.make_async_copy(yrows_ref.at[pl.ds(dest_ref[k * t + tile * tq + j0 + u], 1)],
                                          buf.at[slot, k, pl.ds(j0 + u, 1)], sems.at[slot]).start(priority=k % 2)
            return 0
        lax.fori_loop(0, tq // DMA_UNROLL, body, 0)

    @pl.when(i == 0)
    def _():
        issue(i)

    @pl.when(i + 1 < n)
    def _():
        issue(i + 1)

    for k in range(TOP_K):
        pltpu.make_async_copy(yrows_ref.at[pl.ds(0, tq)], buf.at[i % 2, k], sems.at[i % 2]).wait()
    g = g_ref[...]
    rows = buf[i % 2]
    f = g[:, 0:1] * _unpack_rows(rows[0]) + g[:, 1:2] * _unpack_rows(rows[1])
    x2 = x_ref[...] + gate_ref[0] * f
    if final:
        x2 = x2 * lax.rsqrt(jnp.mean(x2 * x2, axis=-1, keepdims=True) + EPS) * fw_ref[...]
    o_ref[...] = x2


def _combine(dest_flat, y_rows, x1, gates_t, ada3, fw, layer, bsz, seqlen, final):
    t, d = x1.shape
    tq = 256
    per_b = seqlen // tq
    return pl.pallas_call(
        functools.partial(_combine_kernel, tq=tq, t=t, final=final),
        grid_spec=pltpu.PrefetchScalarGridSpec(
            num_scalar_prefetch=1, grid=(t // tq,),
            in_specs=[pl.BlockSpec(memory_space=pl.ANY),
                      pl.BlockSpec((tq, d), lambda i, dr: (i, 0)),
                      pl.BlockSpec((tq, TOP_K), lambda i, dr: (i, 0)),
                      pl.BlockSpec((1, 1, d), lambda i, dr: (layer * bsz + i // per_b, 0, 5)),
                      pl.BlockSpec((1, d), lambda i, dr: (0, 0))],
            out_specs=pl.BlockSpec((tq, d), lambda i, dr: (i, 0)),
            scratch_shapes=[pltpu.VMEM((2, TOP_K, tq, d // 2), jnp.uint32), pltpu.SemaphoreType.DMA((2,))]),
        out_shape=jax.ShapeDtypeStruct((t, d), F32),
        compiler_params=_params("arbitrary"),
        name="combine",
    )(dest_flat, y_rows, x1, gates_t, ada3, fw)


def kernel(x, c, w_ada, b_ada, norm1_w, w_in, conv_w, conv_b, dt_bias, a_log, d_ssd, ssd_norm_w,
           s5_lam_re, s5_lam_im, s5_log_dt, s5_b_re, s5_b_im, s5_c_re, s5_c_im, s5_d, w_glu, b_glu,
           s5_norm_w, w_out, norm2_w, w_rg, b_rg, w_re, b_re, w_eg, w_eu, w_ed, final_norm_w):
    bsz, seqlen, d = x.shape
    t = bsz * seqlen
    depth = w_in.shape[0]
    heads = dt_bias.shape[1]
    inner = ssd_norm_w.shape[1]
    cch = conv_w.shape[2]
    n5 = s5_d.shape[1]
    s5_groups, _, s5_h = s5_b_re.shape[1:]
    n_experts = w_re.shape[2]
    moe_block = 256
    n_rows = t * TOP_K
    n_pad = n_rows + n_experts * moe_block
    n_blocks = n_pad // moe_block

    i0, i1, i2 = inner, inner + cch, inner + cch + heads
    w_zx = w_in[..., :i1].astype(BF16)
    n_du = w_in.shape[2] - i1
    w_du = jnp.pad(w_in[..., i1:].astype(BF16), ((0, 0), (0, 0), (0, -n_du % LANES)))
    lane_pad = lambda a: jnp.pad(a, ((0, 0), (0, LANES - a.shape[1])))[:, None, :]
    dtb, alog = lane_pad(dt_bias), lane_pad(a_log)
    dexp = jnp.repeat(d_ssd, inner // heads, axis=1)[:, None, :]
    li = jnp.arange(SSD_CHUNK)
    tri = (li[None, :] <= li[:, None]).astype(BF16)
    e_mat = (jnp.arange(LANES)[:, None] == (jnp.arange(inner) // (inner // heads))[None, :]).astype(BF16)
    w_out_a = w_out[:, :inner].astype(BF16)
    w_out_b = w_out[:, inner:].astype(BF16)
    w_glu_b = w_glu.astype(BF16)
    wr = jnp.concatenate([w_re, w_rg, jnp.zeros((depth, d, LANES - n_experts - N_EXPERT_GROUPS), F32)], axis=-1)
    wr_hi = wr.astype(BF16)
    wr_lo = (wr - wr_hi.astype(F32)).astype(BF16)
    br = jnp.concatenate([b_re, b_rg, jnp.zeros((depth, LANES - n_experts - N_EXPERT_GROUPS), F32)], axis=-1)[:, None, :]
    sub = 256
    si = jnp.arange(sub)
    upper = (si[:, None] < si[None, :]).astype(BF16)

    ada3 = _ada(c, w_ada, b_ada).reshape(depth * bsz, 1, 6 * d)
    x2 = x.reshape(t, d)
    s5w = _s5_weights(s5_lam_re, s5_lam_im, s5_log_dt, s5_b_re, s5_b_im, s5_c_re, s5_c_im)
    for l in range(depth):
        z, xa, u5, dtp = _inproj(x2, norm1_w[l][None], ada3, w_zx[l], w_du[l], conv_w[l], conv_b[l][None], n5, l,
                                 bsz, seqlen, inner, heads)
        y_ssd = _ssd(z, xa, dtp, dtb[l], alog[l], dexp[l], ssd_norm_w[l][None], tri, e_mat, bsz, seqlen, heads)
        y5 = _s5_core(u5, *s5w, l, bsz, seqlen, s5_h)
        x1, h2, logits_t = _outproj(x2, y_ssd, y5, u5, s5_d[l][None], w_glu_b[l], b_glu[l][None],
                                    s5_norm_w[l][None], w_out_a[l], w_out_b[l], norm2_w[l][None], ada3,
                                    wr_hi[l], wr_lo[l], br[l], l, bsz, seqlen)
        dest, gates, counts = _route(logits_t, upper, n_experts, moe_block)
        dest_flat = dest[1].reshape(-1)
        rows = _dispatch(dest_flat, h2, n_pad)
        y_rows = _ffn(counts[:, 0], rows, w_eg, w_eu, w_ed, l, moe_block)
        x2 = _combine(dest_flat, y_rows, x1, gates[1].T, ada3, final_norm_w[None], l, bsz, seqlen, l == depth - 1)
    return x2.reshape(bsz, seqlen, d)
```

```python
import functools
import math

import jax
import jax.numpy as jnp
from jax import lax
from jax.experimental import pallas as pl
from jax.experimental.pallas import tpu as pltpu

F32 = jnp.float32
BF16 = jnp.bfloat16
EPS = 1e-6

SSD_GROUPS = 2
SSD_STATE = 128
SSD_CHUNK = 128
S5_CHUNK = 16
N_EXPERT_GROUPS = 4
TOP_K = 2
LANES = 128
ROUTER_ROWS = 40
VMEM_LIMIT = 48 * 1024 * 1024


def _dot(a, b):
    return jnp.dot(a, b, preferred_element_type=F32)


def _split2(a):
    hi = a.astype(BF16)
    lo = (a - hi.astype(F32)).astype(BF16)
    return hi, lo


def _split3(a):
    hi = a.astype(BF16)
    r = a - hi.astype(F32)
    mid = r.astype(BF16)
    lo = (r - mid.astype(F32)).astype(BF16)
    return hi, mid, lo


def _pack_rows(x):
    w = x.shape[1] // 2
    bits = lambda v: lax.bitcast_convert_type(v.astype(BF16).astype(F32), jnp.uint32)
    return (bits(x[:, :w]) >> 16) | bits(x[:, w:])


def _unpack_rows(words):
    lo = lax.bitcast_convert_type(words << 16, F32)
    hi = lax.bitcast_convert_type(words & jnp.uint32(0xFFFF0000), F32)
    return jnp.concatenate([lo, hi], axis=1)


def _params(*sem):
    return pltpu.CompilerParams(dimension_semantics=sem, vmem_limit_bytes=VMEM_LIMIT)


def _ada_kernel(c_ref, w_ref, b_ref, o_ref):
    ca = jax.nn.silu(c_ref[...])
    c_hi, c_lo = _split2(ca)
    w_hi, w_lo = _split2(w_ref[0])
    o_ref[0] = _dot(c_hi, w_hi) + _dot(c_lo, w_hi) + _dot(c_hi, w_lo) + b_ref[0]


def _ada(c, w_ada, b_ada):
    n_layers, d, n_out = w_ada.shape
    bsz = c.shape[0]
    tn = 1536
    return pl.pallas_call(
        _ada_kernel,
        grid=(n_layers, n_out // tn),
        in_specs=[pl.BlockSpec((bsz, d), lambda l, j: (0, 0)),
                  pl.BlockSpec((1, d, tn), lambda l, j: (l, 0, j)),
                  pl.BlockSpec((1, 1, tn), lambda l, j: (l, 0, j))],
        out_specs=pl.BlockSpec((1, bsz, tn), lambda l, j: (l, 0, j)),
        out_shape=jax.ShapeDtypeStruct((n_layers, bsz, n_out), F32),
        compiler_params=_params("parallel", "parallel"),
        name="ada",
    )(c, w_ada, b_ada.reshape(n_layers, 1, n_out))


def _inproj_kernel(x_ref, nw_ref, shift_ref, scale_ref, wzx_ref, wdu_ref, cw_ref, cb_ref,
                   z_ref, xa_ref, u_ref, dt_ref, tail_ref, *, n_z, n_dt, per_b):
    tm = x_ref.shape[0]
    tail = tail_ref.shape[0]
    n_conv = cw_ref.shape[0]

    @pl.when(pl.program_id(0) % per_b == 0)
    def _():
        tail_ref[...] = jnp.zeros(tail_ref.shape, F32)

    x = x_ref[...]
    h = x * lax.rsqrt(jnp.mean(x * x, axis=-1, keepdims=True) + EPS) * nw_ref[...]
    h = (h * (1.0 + scale_ref[0]) + shift_ref[0]).astype(BF16)
    u_new = _dot(h, wzx_ref[:, n_z:])
    z_ref[...] = _dot(h, wzx_ref[:, :n_z])
    ext = jnp.concatenate([tail_ref[...], u_new], axis=0)
    conv = cb_ref[...] + cw_ref[n_conv - 1:n_conv, :] * u_new
    for k in range(n_conv - 1):
        conv = conv + cw_ref[k:k + 1, :] * pltpu.roll(ext, n_conv - 1 - k, axis=0)[tail:, :]
    tail_ref[...] = u_new[tm - tail:, :]
    xa_ref[...] = jax.nn.silu(conv)
    q = _dot(h, wdu_ref[...])
    dt_ref[...] = q[:, :LANES]
    u_ref[...] = q[:, n_dt:n_dt + u_ref.shape[1]]


def _inproj(x2, nw, ada3, w_zx, w_du, conv_w, conv_b, n_u, layer, bsz, seqlen, n_z, n_dt):
    t, d = x2.shape
    n_xbc = w_zx.shape[1] - n_z
    tm = min(512, seqlen)
    per_b = seqlen // tm
    row = lambda i: (i, 0)
    const = lambda a: pl.BlockSpec(a.shape, lambda i: (0, 0))
    ada_blk = lambda k: pl.BlockSpec((1, 1, d), lambda i: (layer * bsz + i // per_b, 0, k))
    return pl.pallas_call(
        functools.partial(_inproj_kernel, n_z=n_z, n_dt=n_dt, per_b=per_b),
        grid=(t // tm,),
        in_specs=[pl.BlockSpec((tm, d), row),
                  pl.BlockSpec((1, d), lambda i: (0, 0)),
                  ada_blk(0), ada_blk(1), const(w_zx), const(w_du), const(conv_w), const(conv_b)],
        out_specs=[pl.BlockSpec((tm, n_z), row), pl.BlockSpec((tm, n_xbc), row),
                   pl.BlockSpec((tm, n_u), row), pl.BlockSpec((tm, LANES), row)],
        out_shape=[jax.ShapeDtypeStruct((t, n_z), F32), jax.ShapeDtypeStruct((t, n_xbc), F32),
                   jax.ShapeDtypeStruct((t, n_u), F32), jax.ShapeDtypeStruct((t, LANES), F32)],
        scratch_shapes=[pltpu.VMEM((8, n_xbc), F32)],
        compiler_params=_params("arbitrary"),
        name="inproj",
    )(x2, nw, ada3, ada3, w_zx, w_du, conv_w, conv_b)


def _ssd_kernel(z_ref, xa_ref, dt_ref, dtb_ref, alog_ref, dexp_ref, nw_ref,
                tri_ref, e_ref, y_ref, st_ref, *, inner, heads, chunks):
    @pl.when(pl.program_id(1) == 0)
    def _():
        st_ref[...] = jnp.zeros(st_ref.shape, F32)

    for ci in range(chunks):
        rows = pl.ds(ci * SSD_CHUNK, SSD_CHUNK)
        _ssd_chunk(z_ref[rows, :], xa_ref[rows, :], dt_ref[rows, :], dtb_ref, alog_ref, dexp_ref, nw_ref,
                   tri_ref, e_ref, y_ref.at[rows, :], st_ref, inner=inner, heads=heads)


def _ssd_chunk(z, xa, dt_raw, dtb_ref, alog_ref, dexp_ref, nw_ref, tri_ref, e_ref, y_ref, st_ref, *, inner, heads):
    L = SSD_CHUNK
    N = SSD_STATE
    hd = inner // heads
    hpg = heads // SSD_GROUPS
    gw = inner // SSD_GROUPS
    xs = xa[:, :inner]
    bm = xa[:, inner:inner + SSD_GROUPS * N]
    cm = xa[:, inner + SSD_GROUPS * N:]

    pre = dt_raw + dtb_ref[...]
    dt = jnp.maximum(pre, 0.0) + jnp.log1p(jnp.exp(-jnp.abs(pre)))
    ad = dt * (-jnp.exp(alog_ref[...]))
    tri = tri_ref[...]
    a1, a2, a3 = _split3(ad)
    acs = _dot(tri, a1) + _dot(tri, a2) + _dot(tri, a3)
    acs_t = acs.T
    dt_t = dt.T
    w = dt * jnp.exp(acs[L - 1:L, :] - acs)
    eacs = jnp.exp(acs)
    e_mat = e_ref[...]
    w1, w2 = _split2(w)
    w_exp = _dot(w1, e_mat) + _dot(w2, e_mat)
    q1, q2 = _split2(eacs)
    eacs_exp = _dot(q1, e_mat) + _dot(q2, e_mat)

    row = lax.broadcasted_iota(jnp.int32, (L, L), 0)
    col = lax.broadcasted_iota(jnp.int32, (L, L), 1)
    causal = row >= col
    lane = lax.broadcasted_iota(jnp.int32, (L, 2 * hd), 1)
    h_prev = st_ref[...]

    y_diag, y_off, st_new = [], [], []
    for g in range(SSD_GROUPS):
        bg = bm[:, g * N:(g + 1) * N]
        cg = cm[:, g * N:(g + 1) * N].astype(BF16)
        cb = lax.dot_general(cg, bg.astype(BF16), (((1,), (1,)), ((), ())), preferred_element_type=F32)
        for j in range(hpg // 2):
            h0 = g * hpg + 2 * j
            lms = []
            for h in (h0, h0 + 1):
                seg = jnp.where(causal, acs[:, h:h + 1] - acs_t[h:h + 1, :], -1e30)
                lms.append((cb * jnp.exp(seg) * dt_t[h:h + 1, :]).astype(BF16))
            pair = xs[:, h0 * hd:(h0 + 2) * hd]
            w_bd = jnp.concatenate([jnp.where(lane < hd, pair, 0.0), jnp.where(lane >= hd, pair, 0.0)],
                                   axis=0).astype(BF16)
            y_diag.append(_dot(jnp.concatenate(lms, axis=1), w_bd))
        sl = slice(g * gw, (g + 1) * gw)
        y_off.append(_dot(cg, h_prev[:, sl].astype(BF16)))
        st_new.append(_dot(bg.T.astype(BF16), (xs[:, sl] * w_exp[:, sl]).astype(BF16)))

    y = jnp.concatenate(y_diag, axis=1) + jnp.concatenate(y_off, axis=1) * eacs_exp + dexp_ref[...] * xs
    st_ref[...] = h_prev * eacs_exp[L - 1:L, :] + jnp.concatenate(st_new, axis=1)
    y = y * jax.nn.silu(z)
    outs = []
    for g in range(SSD_GROUPS):
        yg = y[:, g * gw:(g + 1) * gw]
        outs.append(yg * lax.rsqrt(jnp.mean(yg * yg, axis=-1, keepdims=True) + EPS))
    y_ref[...] = (jnp.concatenate(outs, axis=1) * nw_ref[...]).astype(y_ref.dtype)


def _ssd(z, xa, dtp, dtb, alog, dexp, nw, tri, e_mat, bsz, seqlen, heads):
    t, inner = z.shape
    cch = xa.shape[1]
    chunks = 2
    rows = chunks * SSD_CHUNK
    nb = seqlen // rows
    row = lambda b, c: (b * nb + c, 0)
    const = lambda shape: pl.BlockSpec(shape, lambda b, c: (0, 0))
    return pl.pallas_call(
        functools.partial(_ssd_kernel, inner=inner, heads=heads, chunks=chunks),
        grid=(bsz, nb),
        in_specs=[pl.BlockSpec((rows, inner), row), pl.BlockSpec((rows, cch), row), pl.BlockSpec((rows, LANES), row),
                  const((1, LANES)), const((1, LANES)), const((1, inner)), const((1, inner)),
                  const((SSD_CHUNK, SSD_CHUNK)), const((LANES, inner))],
        out_specs=pl.BlockSpec((rows, inner), row),
        out_shape=jax.ShapeDtypeStruct((t, inner), BF16),
        scratch_shapes=[pltpu.VMEM((SSD_STATE, inner), F32)],
        compiler_params=_params("parallel", "arbitrary"),
        name="ssd",
    )(z, xa, dtp, dtb, alog, dexp, nw, tri, e_mat)


def _s5_kernel(*refs, groups, gch):
    n_t = groups * gch // LANES
    u_refs = refs[:n_t]
    toep_ref, ws_ref, wo_ref, ar_ref, ai_ref = refs[n_t:n_t + 5]
    y_refs = refs[n_t + 5:2 * n_t + 5]
    fold, yfold, sre, sim, pre, pim = refs[2 * n_t + 5:]
    lc = S5_CHUNK
    nc = u_refs[0].shape[0] // lc
    per_tile = LANES // gch
    n_tiles = groups // per_tile
    halves = lc * gch // LANES
    w2 = halves * LANES
    rb = 16
    piece = lax.broadcasted_iota(jnp.int32, (rb, LANES), 1) // gch

    def merge(srcs, shift_of):
        acc = None
        for q, src in enumerate(srcs):
            sh = shift_of(q) * gch
            r = pltpu.roll(src, sh, axis=1) if sh else src
            acc = r if acc is None else jnp.where(piece == q, r, acc)
        return acc

    def fold_rows(b, _):
        r0 = pl.multiple_of(b * rb, rb)
        for j in range(n_tiles):
            tiles = [u_refs[j][pl.ds(b * (rb * lc) + t, rb, stride=lc), :] for t in range(lc)]
            for gp in range(per_tile):
                g = j * per_tile + gp
                parts = [merge(tiles[hv * per_tile:(hv + 1) * per_tile], lambda tt: (tt - gp) % per_tile)
                         for hv in range(halves)]
                fold[g // 2, pl.ds(r0, rb), (g % 2) * w2:(g % 2 + 1) * w2] = (
                    jnp.concatenate(parts, axis=1).astype(BF16))
        return 0

    lax.fori_loop(0, nc // rb, fold_rows, 0)

    n_pairs = groups // 2
    for k in range(n_pairs):
        inc = _dot(fold[k], ws_ref[k])
        sre[:, k * LANES:(k + 1) * LANES] = inc[:, :LANES]
        sim[:, k * LANES:(k + 1) * LANES] = inc[:, LANES:]

    ar = ar_ref[...]
    ai = ai_ref[...]

    def body(c, carry):
        r, i = carry
        pre[pl.ds(c, 1), :] = r
        pim[pl.ds(c, 1), :] = i
        return ar * r - ai * i + sre[pl.ds(c, 1), :], ar * i + ai * r + sim[pl.ds(c, 1), :]

    zero = jnp.zeros((1, sre.shape[1]), F32)
    lax.fori_loop(0, nc, body, (zero, zero))

    for k in range(n_pairs):
        a = fold[k]
        intra = jnp.concatenate([_dot(a[:, :w2], toep_ref[2 * k]), _dot(a[:, w2:], toep_ref[2 * k + 1])], axis=1)
        state = jnp.concatenate([pre[:, k * LANES:(k + 1) * LANES], pim[:, k * LANES:(k + 1) * LANES]], axis=1)
        yfold[k] = intra + _dot(state.astype(BF16), wo_ref[k])

    def unfold_rows(b, _):
        r0 = pl.multiple_of(b * rb, rb)
        for j in range(n_tiles):
            for hv in range(halves):
                srcs = []
                for gp in range(per_tile):
                    g = j * per_tile + gp
                    srcs.append(yfold[g // 2, pl.ds(r0, rb), pl.ds((g % 2) * w2 + hv * LANES, LANES)])
                for tt in range(per_tile):
                    y_refs[j][pl.ds(b * (rb * lc) + hv * per_tile + tt, rb, stride=lc), :] = merge(
                        srcs, lambda gp: (gp - tt) % per_tile)
        return 0

    lax.fori_loop(0, nc // rb, unfold_rows, 0)


def _s5_core(u5, toep, ws, wo, a_re, a_im, layer, bsz, seqlen, gch):
    t, n5 = u5.shape
    groups = n5 // gch
    nc = seqlen // S5_CHUNK
    n_t = n5 // LANES
    const = lambda a: pl.BlockSpec((None,) + a.shape[1:], lambda b: (layer,) + (0,) * (a.ndim - 1),
                                   pipeline_mode=pl.Buffered(1))
    col = lambda j: pl.BlockSpec((seqlen, LANES), lambda b: (b, j))
    return pl.pallas_call(
        functools.partial(_s5_kernel, groups=groups, gch=gch),
        grid=(bsz,),
        in_specs=[col(j) for j in range(n_t)] + [const(toep), const(ws), const(wo), const(a_re), const(a_im)],
        out_specs=[pl.BlockSpec((seqlen, LANES), lambda b: (b, 0))] * n_t,
        out_shape=[jax.ShapeDtypeStruct((t, LANES), F32)] * n_t,
        scratch_shapes=[pltpu.VMEM((groups // 2, nc, 2 * S5_CHUNK * gch), BF16),
                        pltpu.VMEM((groups // 2, nc, 2 * S5_CHUNK * gch), F32)]
        + [pltpu.VMEM((nc, a_re.shape[2]), F32)] * 4,
        compiler_params=_params("parallel"),
        name="s5_core",
    )(*([u5] * n_t), toep, ws, wo, a_re, a_im)


def _s5_weights(lam_re, lam_im, log_dt, b_re, b_im, c_re, c_im):
    lc = S5_CHUNK
    nl, g, p, h = b_re.shape
    step = jnp.exp(log_dt)[..., None]
    mag = jnp.exp(lam_re * step)
    ab_re, ab_im = mag * jnp.cos(lam_im * step), mag * jnp.sin(lam_im * step)
    den = lam_re * lam_re + lam_im * lam_im
    nr = ab_re - 1.0
    coef_re = (nr * lam_re + ab_im * lam_im) / den
    coef_im = (ab_im * lam_re - nr * lam_im) / den
    swap = lambda a: a.transpose(0, 1, 3, 2)
    bt_re, bt_im = swap(b_re), swap(b_im)
    bbt_re = coef_re[:, :, None, :] * bt_re - coef_im[:, :, None, :] * bt_im
    bbt_im = coef_re[:, :, None, :] * bt_im + coef_im[:, :, None, :] * bt_re
    jj = jnp.arange(lc + 1, dtype=F32)
    pmag = jnp.exp((lam_re * step)[..., None] * jj)
    pw_re = pmag * jnp.cos((lam_im * step)[..., None] * jj)
    pw_im = pmag * jnp.sin((lam_im * step)[..., None] * jj)
    ct_re, ct_im = swap(c_re), swap(c_im)
    wide = lambda a: a.reshape(nl, g, p, (lc + 1) * h)
    cpt_re = wide(ct_re[:, :, :, None, :] * pw_re[..., None] - ct_im[:, :, :, None, :] * pw_im[..., None])
    cpt_im = wide(ct_re[:, :, :, None, :] * pw_im[..., None] + ct_im[:, :, :, None, :] * pw_re[..., None])
    w = lc * h
    k2 = (jnp.einsum('lghp,lgpc->lghc', bbt_re, cpt_re[..., :w], precision='highest')
          - jnp.einsum('lghp,lgpc->lghc', bbt_im, cpt_im[..., :w], precision='highest'))
    toep = jnp.stack([jnp.pad(k2, ((0, 0), (0, 0), (0, 0), (ti * h, 0)))[..., :w] for ti in range(lc)], axis=2)
    toep = toep.reshape(nl, g, w, w)
    rev_re = swap(jnp.flip(pw_re[..., :lc], axis=-1))
    rev_im = swap(jnp.flip(pw_im[..., :lc], axis=-1))
    ws_re = (rev_re[:, :, :, None, :] * bbt_re[:, :, None] - rev_im[:, :, :, None, :] * bbt_im[:, :, None]
             ).reshape(nl, g, w, p)
    ws_im = (rev_re[:, :, :, None, :] * bbt_im[:, :, None] + rev_im[:, :, :, None, :] * bbt_re[:, :, None]
             ).reshape(nl, g, w, p)
    wo_re, wo_im = cpt_re[..., h:], -cpt_im[..., h:]

    def pair(a):
        a = a.reshape((nl, g // 2, 2) + a.shape[2:])
        z = jnp.zeros_like(a[:, :, 0])
        return jnp.concatenate([jnp.concatenate([a[:, :, 0], z], axis=-1),
                                jnp.concatenate([z, a[:, :, 1]], axis=-1)], axis=-2).astype(BF16)

    ws = jnp.concatenate([pair(ws_re), pair(ws_im)], axis=-1)
    wo = jnp.concatenate([pair(wo_re), pair(wo_im)], axis=-2)
    return (toep.astype(BF16), ws, wo, pw_re[..., lc].reshape(nl, 1, g * p), pw_im[..., lc].reshape(nl, 1, g * p))


def _outproj_kernel(x_ref, ys_ref, *refs):
    n_t = len(refs) - 17
    y5_refs = refs[:n_t]
    (u5_ref, d5_ref, wg_ref, bg_ref, n5_ref, wa_ref, wb_ref, gate_ref, nw_ref, shift_ref, scale_ref,
     wrh_ref, wrl_ref, br_ref, x1_ref, h2_ref, lg_ref) = refs[n_t:]
    y = jnp.concatenate([r[...] for r in y5_refs], axis=1) + d5_ref[...] * u5_ref[...]
    y = jax.nn.gelu(y)
    y = y * jax.nn.sigmoid(_dot(y.astype(BF16), wg_ref[...]) + bg_ref[...])
    y = y * lax.rsqrt(jnp.mean(y * y, axis=-1, keepdims=True) + EPS) * n5_ref[...]
    m = _dot(ys_ref[...], wa_ref[...]) + _dot(y.astype(BF16), wb_ref[...])
    x1 = x_ref[...] + gate_ref[0] * m
    x1_ref[...] = x1
    h = x1 * lax.rsqrt(jnp.mean(x1 * x1, axis=-1, keepdims=True) + EPS) * nw_ref[...]
    h = h * (1.0 + scale_ref[0]) + shift_ref[0]
    h2_ref[...] = _pack_rows(h)
    h_hi, h_lo = _split2(h)
    wrh = wrh_ref[...]
    lg = _dot(h_hi, wrh) + _dot(h_lo, wrh) + _dot(h_hi, wrl_ref[...]) + br_ref[...]
    lg_ref[...] = lg.T[:ROUTER_ROWS, :]


def _outproj(x2, y_ssd, y5, u5, d5, w_glu, b_glu, n5, w_a, w_b, nw, ada3, wr_hi, wr_lo, br,
             layer, bsz, seqlen):
    t, d = x2.shape
    n_s = y_ssd.shape[1]
    n_5 = u5.shape[1]
    tm = min(512, seqlen)
    per_b = seqlen // tm
    row = lambda i: (i, 0)
    const = lambda a: pl.BlockSpec(a.shape, lambda i: (0, 0))
    ada_blk = lambda k: pl.BlockSpec((1, 1, d), lambda i: (layer * bsz + i // per_b, 0, k))
    return pl.pallas_call(
        _outproj_kernel,
        grid=(t // tm,),
        in_specs=[pl.BlockSpec((tm, d), row), pl.BlockSpec((tm, n_s), row)]
        + [pl.BlockSpec((tm, LANES), row)] * len(y5)
        + [pl.BlockSpec((tm, n_5), row), const(d5), const(w_glu), const(b_glu), const(n5),
                  const(w_a), const(w_b), ada_blk(2), const(nw), ada_blk(3), ada_blk(4),
                  const(wr_hi), const(wr_lo), const(br)],
        out_specs=[pl.BlockSpec((tm, d), row), pl.BlockSpec((tm, d // 2), row),
                   pl.BlockSpec((ROUTER_ROWS, tm), lambda i: (0, i))],
        out_shape=[jax.ShapeDtypeStruct((t, d), F32), jax.ShapeDtypeStruct((t, d // 2), jnp.uint32),
                   jax.ShapeDtypeStruct((ROUTER_ROWS, t), F32)],
        compiler_params=_params("parallel"),
        name="outproj",
    )(x2, y_ssd, *y5, u5, d5, w_glu, b_glu, n5, w_a, w_b, ada3, nw, ada3, ada3, wr_hi, wr_lo, br)


def _route_kernel(lg_ref, upper_ref, dest_ref, gate_ref, cnt_ref, counts, carry, pstart,
                  *, n_experts, per_group, block_rows, sub):
    ph = pl.program_id(0)
    i = pl.program_id(1)
    tr = lg_ref.shape[1]
    lg = lg_ref[...]

    @pl.when((ph == 0) & (i == 0))
    def _():
        counts[...] = jnp.zeros(counts.shape, F32)

    gl = [lg[n_experts + k:n_experts + k + 1, :] for k in range(N_EXPERT_GROUPS)]
    gmax = functools.reduce(jnp.maximum, gl)
    gidx = jnp.full((1, tr), N_EXPERT_GROUPS - 1, jnp.int32)
    for k in range(N_EXPERT_GROUPS - 2, -1, -1):
        gidx = jnp.where(gl[k] == gmax, k, gidx)
    gsum = functools.reduce(lambda a, b: a + b, [jnp.exp(v - gmax) for v in gl])
    g_w = 1.0 / gsum
    el = lg[0:per_group, :]
    for k in range(1, N_EXPERT_GROUPS):
        el = jnp.where(gidx == k, lg[k * per_group:(k + 1) * per_group, :], el)
    ep = jnp.exp(el - jnp.max(el, axis=0, keepdims=True))
    prob = ep / jnp.sum(ep, axis=0, keepdims=True)
    jj = lax.broadcasted_iota(jnp.int32, (per_group, tr), 0).astype(F32)
    p1 = jnp.max(prob, axis=0, keepdims=True)
    i1 = jnp.min(jnp.where(prob == p1, jj, float(per_group)), axis=0, keepdims=True)
    prob2 = jnp.where(jj == i1, -1.0, prob)
    p2 = jnp.max(prob2, axis=0, keepdims=True)
    i2 = jnp.min(jnp.where(prob2 == p2, jj, float(per_group)), axis=0, keepdims=True)
    den = p1 + p2
    gate_ref[0] = jnp.concatenate([g_w * p1 / den, g_w * p2 / den], axis=0)
    e1 = gidx * per_group + i1.astype(jnp.int32)
    e2 = gidx * per_group + i2.astype(jnp.int32)
    rr = lax.broadcasted_iota(jnp.int32, (n_experts, tr), 0)
    oh1 = rr == e1
    oh2 = rr == e2
    member = jnp.where(oh1 | oh2, 1.0, 0.0)

    @pl.when(ph == 0)
    def _():
        counts[...] = counts[...] + jnp.sum(member, axis=1, keepdims=True)
        dest_ref[...] = jnp.zeros(dest_ref.shape, jnp.int32)

    @pl.when(ph == 1)
    def _():
        @pl.when(i == 0)
        def _():
            blocks = (counts[...].astype(jnp.int32) + (block_rows - 1)) >> int(math.log2(block_rows))
            hi = (blocks >> 4).astype(F32).astype(BF16)
            lo = (blocks & 15).astype(F32).astype(BF16)
            er = lax.broadcasted_iota(jnp.int32, (n_experts, n_experts), 0)
            ec = lax.broadcasted_iota(jnp.int32, (n_experts, n_experts), 1)
            lower = jnp.where(ec < er, 1.0, 0.0).astype(BF16)
            pstart[...] = (16.0 * _dot(lower, hi) + _dot(lower, lo)) * float(block_rows)
            carry[...] = jnp.zeros(carry.shape, F32)

        run = carry[...]
        upper = upper_ref[...]
        pieces = []
        for b in range(tr // sub):
            mb = member[:, b * sub:(b + 1) * sub]
            pieces.append(_dot(mb.astype(BF16), upper) + jnp.concatenate([run] * (sub // LANES), axis=1))
            run = run + jnp.sum(mb, axis=1, keepdims=True)
        carry[...] = run
        base = jnp.concatenate(pieces, axis=1) + jnp.concatenate([pstart[...]] * (tr // LANES), axis=1)
        d1 = jnp.sum(jnp.where(oh1, base, 0.0), axis=0, keepdims=True)
        d2 = jnp.sum(jnp.where(oh2, base, 0.0), axis=0, keepdims=True)
        dest_ref[0] = jnp.concatenate([d1, d2], axis=0).astype(jnp.int32)

    cnt_ref[...] = counts[...].astype(jnp.int32)


def _route(logits_t, upper, n_experts, block_rows):
    rows, t = logits_t.shape
    tr = 1024
    sub = upper.shape[0]
    return pl.pallas_call(
        functools.partial(_route_kernel, n_experts=n_experts, per_group=n_experts // N_EXPERT_GROUPS,
                          block_rows=block_rows, sub=sub),
        grid=(2, t // tr),
        in_specs=[pl.BlockSpec((rows, tr), lambda ph, i: (0, i)),
                  pl.BlockSpec(upper.shape, lambda ph, i: (0, 0))],
        out_specs=[pl.BlockSpec((1, TOP_K, tr), lambda ph, i: (ph, 0, i)),
                   pl.BlockSpec((1, TOP_K, tr), lambda ph, i: (ph, 0, i)),
                   pl.BlockSpec((n_experts, LANES), lambda ph, i: (0, 0))],
        out_shape=[jax.ShapeDtypeStruct((2, TOP_K, t), jnp.int32), jax.ShapeDtypeStruct((2, TOP_K, t), F32),
                   jax.ShapeDtypeStruct((n_experts, LANES), jnp.int32)],
        scratch_shapes=[pltpu.VMEM((n_experts, LANES), F32)] * 3,
        compiler_params=_params("arbitrary", "arbitrary"),
        name="route",
    )(logits_t, upper)


DMA_UNROLL = 8


def _dispatch_kernel(dest_ref, h_ref, init_ref, rows_ref, sem, *, tq, t):
    del init_ref
    base = pl.program_id(0) * tq

    def issue(jb, _):
        j0 = pl.multiple_of(jb * DMA_UNROLL, DMA_UNROLL)
        for u in range(DMA_UNROLL):
            for k in range(TOP_K):
                pltpu.make_async_copy(h_ref.at[pl.ds(j0 + u, 1)],
                                      rows_ref.at[pl.ds(dest_ref[k * t + base + j0 + u], 1)], sem).start(priority=k % 2)
        return 0

    lax.fori_loop(0, tq // DMA_UNROLL, issue, 0)
    for k in range(TOP_K):
        pltpu.make_async_copy(h_ref, rows_ref.at[pl.ds(0, tq)], sem).wait()


def _dispatch(dest_flat, h2, n_pad):
    t, d = h2.shape
    tq = 512
    init = jnp.zeros((n_pad, d), h2.dtype)
    return pl.pallas_call(
        functools.partial(_dispatch_kernel, tq=tq, t=t),
        grid_spec=pltpu.PrefetchScalarGridSpec(
            num_scalar_prefetch=1, grid=(t // tq,),
            in_specs=[pl.BlockSpec((tq, d), lambda i, dr: (i, 0)), pl.BlockSpec(memory_space=pl.ANY)],
            out_specs=pl.BlockSpec(memory_space=pl.ANY),
            scratch_shapes=[pltpu.SemaphoreType.DMA]),
        out_shape=jax.ShapeDtypeStruct((n_pad, d), h2.dtype),
        input_output_aliases={2: 0},
        compiler_params=_params("arbitrary"),
        name="dispatch",
    )(dest_flat, h2, init)


def _ffn_kernel(be_ref, first_ref, slot_ref, nxt_ref, hasnext_ref, nused_ref, x_ref, wg_hbm, wu_hbm, wd_hbm,
                y_ref, sg, su, sd, wg16, wu16, wd16, sems, *, layer):
    i = pl.program_id(0)

    def weight_copies(e, s):
        return (pltpu.make_async_copy(wg_hbm.at[layer, e], sg.at[s], sems.at[s, 0]),
                pltpu.make_async_copy(wu_hbm.at[layer, e], su.at[s], sems.at[s, 1]),
                pltpu.make_async_copy(wd_hbm.at[layer, e], sd.at[s], sems.at[s, 2]))

    @pl.when(first_ref[i] == 1)
    def _():
        s = slot_ref[i]
        e = be_ref[i]

        @pl.when(i == 0)
        def _():
            for cp in weight_copies(e, s):
                cp.start()

        for cp in weight_copies(e, s):
            cp.wait()
        wg16[...] = sg[s].astype(BF16)
        wu16[...] = su[s].astype(BF16)
        wd16[...] = sd[s].astype(BF16)

        @pl.when(hasnext_ref[i] == 1)
        def _():
            for cp in weight_copies(nxt_ref[i], 1 - s):
                cp.start()

    @pl.when(i < nused_ref[0])
    def _():
        xb = _unpack_rows(x_ref[...]).astype(BF16)
        a = jax.nn.silu(_dot(xb, wg16[...])) * _dot(xb, wu16[...])
        y_ref[...] = _pack_rows(_dot(a.astype(BF16), wd16[...]))

    @pl.when(i >= nused_ref[0])
    def _():
        y_ref[...] = jnp.zeros(y_ref.shape, y_ref.dtype)


def _ffn(counts, rows, w_eg, w_eu, w_ed, layer, bm):
    n_pad, dw = rows.shape
    _, n_experts, d, ff = w_eg.shape
    n_blocks = n_pad // bm
    ends = jnp.cumsum((counts + bm - 1) // bm).astype(jnp.int32)
    n_used = ends[-1]
    bidx = jnp.arange(n_blocks, dtype=jnp.int32)
    be = jnp.minimum(jnp.sum(ends[None, :] <= bidx[:, None], axis=1), n_experts - 1).astype(jnp.int32)
    be = jnp.where(bidx < n_used, be, be[n_used - 1])
    first = jnp.concatenate([jnp.ones((1,), jnp.int32), (be[1:] != be[:-1]).astype(jnp.int32)])
    slot = ((jnp.cumsum(first) - 1) & 1).astype(jnp.int32)
    seg_end = ends[be]
    has_next = (seg_end < n_used).astype(jnp.int32)
    nxt = be[jnp.minimum(seg_end, n_blocks - 1)]
    row = lambda i, *_: (i, 0)
    return pl.pallas_call(
        functools.partial(_ffn_kernel, layer=layer),
        grid_spec=pltpu.PrefetchScalarGridSpec(
            num_scalar_prefetch=6, grid=(n_blocks,),
            in_specs=[pl.BlockSpec((bm, dw), row)] + [pl.BlockSpec(memory_space=pl.ANY)] * 3,
            out_specs=pl.BlockSpec((bm, dw), row),
            scratch_shapes=[pltpu.VMEM((2, d, ff), F32), pltpu.VMEM((2, d, ff), F32), pltpu.VMEM((2, ff, d), F32),
                            pltpu.VMEM((d, ff), BF16), pltpu.VMEM((d, ff), BF16), pltpu.VMEM((ff, d), BF16),
                            pltpu.SemaphoreType.DMA((2, 3))]),
        out_shape=jax.ShapeDtypeStruct((n_pad, dw), jnp.uint32),
        compiler_params=_params("arbitrary"),
        name="ffn",
    )(be, first, slot, nxt, has_next, n_used[None], rows, w_eg, w_eu, w_ed)


def _combine_kernel(dest_ref, yrows_ref, x_ref, g_ref, gate_ref, fw_ref, o_ref, buf, sems, *, tq, t, final):
    i = pl.program_id(0)
    n = pl.num_programs(0)

    def issue(tile):
        slot = tile % 2

        def body(jb, _):
            j0 = pl.multiple_of(jb * DMA_UNROLL, DMA_UNROLL)
            for u in range(DMA_UNROLL):
                for k in range(TOP_K):
                    pltpu.make_async_copy(yrows_ref.at[pl.ds(dest_ref[k * t + tile * tq + j0 + u], 1)],
                                          buf.at[slot, k, pl.ds(j0 + u, 1)], sems.at[slot]).start(priority=k % 2)
            return 0
        lax.fori_loop(0, tq // DMA_UNROLL, body, 0)

    @pl.when(i == 0)
    def _():
        issue(i)

    @pl.when(i + 1 < n)
    def _():
        issue(i + 1)

    for k in range(TOP_K):
        pltpu.make_async_copy(yrows_ref.at[pl.ds(0, tq)], buf.at[i % 2, k], sems.at[i % 2]).wait()
    g = g_ref[...]
    rows = buf[i % 2]
    f = g[:, 0:1] * _unpack_rows(rows[0]) + g[:, 1:2] * _unpack_rows(rows[1])
    x2 = x_ref[...] + gate_ref[0] * f
    if final:
        x2 = x2 * lax.rsqrt(jnp.mean(x2 * x2, axis=-1, keepdims=True) + EPS) * fw_ref[...]
    o_ref[...] = x2


def _combine(dest_flat, y_rows, x1, gates_t, ada3, fw, layer, bsz, seqlen, final):
    t, d = x1.shape
    tq = 512
    per_b = seqlen // tq
    return pl.pallas_call(
        functools.partial(_combine_kernel, tq=tq, t=t, final=final),
        grid_spec=pltpu.PrefetchScalarGridSpec(
            num_scalar_prefetch=1, grid=(t // tq,),
            in_specs=[pl.BlockSpec(memory_space=pl.ANY),
                      pl.BlockSpec((tq, d), lambda i, dr: (i, 0)),
                      pl.BlockSpec((tq, TOP_K), lambda i, dr: (i, 0)),
                      pl.BlockSpec((1, 1, d), lambda i, dr: (layer * bsz + i // per_b, 0, 5)),
                      pl.BlockSpec((1, d), lambda i, dr: (0, 0))],
            out_specs=pl.BlockSpec((tq, d), lambda i, dr: (i, 0)),
            scratch_shapes=[pltpu.VMEM((2, TOP_K, tq, d // 2), jnp.uint32), pltpu.SemaphoreType.DMA((2,))]),
        out_shape=jax.ShapeDtypeStruct((t, d), F32),
        compiler_params=_params("arbitrary"),
        name="combine",
    )(dest_flat, y_rows, x1, gates_t, ada3, fw)


def kernel(x, c, w_ada, b_ada, norm1_w, w_in, conv_w, conv_b, dt_bias, a_log, d_ssd, ssd_norm_w,
           s5_lam_re, s5_lam_im, s5_log_dt, s5_b_re, s5_b_im, s5_c_re, s5_c_im, s5_d, w_glu, b_glu,
           s5_norm_w, w_out, norm2_w, w_rg, b_rg, w_re, b_re, w_eg, w_eu, w_ed, final_norm_w):
    bsz, seqlen, d = x.shape
    t = bsz * seqlen
    depth = w_in.shape[0]
    heads = dt_bias.shape[1]
    inner = ssd_norm_w.shape[1]
    cch = conv_w.shape[2]
    n5 = s5_d.shape[1]
    s5_groups, _, s5_h = s5_b_re.shape[1:]
    n_experts = w_re.shape[2]
    moe_block = 512
    n_rows = t * TOP_K
    n_pad = n_rows + n_experts * moe_block
    n_blocks = n_pad // moe_block

    i0, i1, i2 = inner, inner + cch, inner + cch + heads
    w_zx = w_in[..., :i1].astype(BF16)
    n_du = w_in.shape[2] - i1
    w_du = jnp.pad(w_in[..., i1:].astype(BF16), ((0, 0), (0, 0), (0, -n_du % LANES)))
    lane_pad = lambda a: jnp.pad(a, ((0, 0), (0, LANES - a.shape[1])))[:, None, :]
    dtb, alog = lane_pad(dt_bias), lane_pad(a_log)
    dexp = jnp.repeat(d_ssd, inner // heads, axis=1)[:, None, :]
    li = jnp.arange(SSD_CHUNK)
    tri = (li[None, :] <= li[:, None]).astype(BF16)
    e_mat = (jnp.arange(LANES)[:, None] == (jnp.arange(inner) // (inner // heads))[None, :]).astype(BF16)
    w_out_a = w_out[:, :inner].astype(BF16)
    w_out_b = w_out[:, inner:].astype(BF16)
    w_glu_b = w_glu.astype(BF16)
    wr = jnp.concatenate([w_re, w_rg, jnp.zeros((depth, d, LANES - n_experts - N_EXPERT_GROUPS), F32)], axis=-1)
    wr_hi = wr.astype(BF16)
    wr_lo = (wr - wr_hi.astype(F32)).astype(BF16)
    br = jnp.concatenate([b_re, b_rg, jnp.zeros((depth, LANES - n_experts - N_EXPERT_GROUPS), F32)], axis=-1)[:, None, :]
    sub = 256
    si = jnp.arange(sub)
    upper = (si[:, None] < si[None, :]).astype(BF16)

    ada3 = _ada(c, w_ada, b_ada).reshape(depth * bsz, 1, 6 * d)
    x2 = x.reshape(t, d)
    s5w = _s5_weights(s5_lam_re, s5_lam_im, s5_log_dt, s5_b_re, s5_b_im, s5_c_re, s5_c_im)
    for l in range(depth):
        z, xa, u5, dtp = _inproj(x2, norm1_w[l][None], ada3, w_zx[l], w_du[l], conv_w[l], conv_b[l][None], n5, l,
                                 bsz, seqlen, inner, heads)
        y_ssd = _ssd(z, xa, dtp, dtb[l], alog[l], dexp[l], ssd_norm_w[l][None], tri, e_mat, bsz, seqlen, heads)
        y5 = _s5_core(u5, *s5w, l, bsz, seqlen, s5_h)
        x1, h2, logits_t = _outproj(x2, y_ssd, y5, u5, s5_d[l][None], w_glu_b[l], b_glu[l][None],
                                    s5_norm_w[l][None], w_out_a[l], w_out_b[l], norm2_w[l][None], ada3,
                                    wr_hi[l], wr_lo[l], br[l], l, bsz, seqlen)
        dest, gates, counts = _route(logits_t, upper, n_experts, moe_block)
        dest_flat = dest[1].reshape(-1)
        rows = _dispatch(dest_flat, h2, n_pad)
        y_rows = _ffn(counts[:, 0], rows, w_eg, w_eu, w_ed, l, moe_block)
        x2 = _combine(dest_flat, y_rows, x1, gates[1].T, ada3, final_norm_w[None], l, bsz, seqlen, l == depth - 1)
    return x2.reshape(bsz, seqlen, d)
```

```python
import functools
import math

import jax
import jax.numpy as jnp
from jax import lax
from jax.experimental import pallas as pl
from jax.experimental.pallas import tpu as pltpu

F32 = jnp.float32
BF16 = jnp.bfloat16
EPS = 1e-6

SSD_GROUPS = 2
SSD_STATE = 128
SSD_CHUNK = 128
S5_CHUNK = 16
N_EXPERT_GROUPS = 4
TOP_K = 2
LANES = 128
ROUTER_ROWS = 40
VMEM_LIMIT = 48 * 1024 * 1024


def _dot(a, b):
    return jnp.dot(a, b, preferred_element_type=F32)


def _split2(a):
    hi = a.astype(BF16)
    lo = (a - hi.astype(F32)).astype(BF16)
    return hi, lo


def _split3(a):
    hi = a.astype(BF16)
    r = a - hi.astype(F32)
    mid = r.astype(BF16)
    lo = (r - mid.astype(F32)).astype(BF16)
    return hi, mid, lo


def _pack_rows(x):
    w = x.shape[1] // 2
    bits = lambda v: lax.bitcast_convert_type(v.astype(BF16).astype(F32), jnp.uint32)
    return (bits(x[:, :w]) >> 16) | bits(x[:, w:])


def _unpack_rows(words):
    lo = lax.bitcast_convert_type(words << 16, F32)
    hi = lax.bitcast_convert_type(words & jnp.uint32(0xFFFF0000), F32)
    return jnp.concatenate([lo, hi], axis=1)


def _params(*sem):
    return pltpu.CompilerParams(dimension_semantics=sem, vmem_limit_bytes=VMEM_LIMIT)


def _ada_kernel(c_ref, w_ref, b_ref, o_ref):
    ca = jax.nn.silu(c_ref[...])
    c_hi, c_lo = _split2(ca)
    w_hi, w_lo = _split2(w_ref[0])
    o_ref[0] = _dot(c_hi, w_hi) + _dot(c_lo, w_hi) + _dot(c_hi, w_lo) + b_ref[0]


def _ada(c, w_ada, b_ada):
    n_layers, d, n_out = w_ada.shape
    bsz = c.shape[0]
    tn = 1536
    return pl.pallas_call(
        _ada_kernel,
        grid=(n_layers, n_out // tn),
        in_specs=[pl.BlockSpec((bsz, d), lambda l, j: (0, 0)),
                  pl.BlockSpec((1, d, tn), lambda l, j: (l, 0, j)),
                  pl.BlockSpec((1, 1, tn), lambda l, j: (l, 0, j))],
        out_specs=pl.BlockSpec((1, bsz, tn), lambda l, j: (l, 0, j)),
        out_shape=jax.ShapeDtypeStruct((n_layers, bsz, n_out), F32),
        compiler_params=_params("parallel", "parallel"),
        name="ada",
    )(c, w_ada, b_ada.reshape(n_layers, 1, n_out))


def _inproj_kernel(x_ref, nw_ref, shift_ref, scale_ref, wzx_ref, wdu_ref, cw_ref, cb_ref,
                   z_ref, xa_ref, u_ref, dt_ref, tail_ref, *, n_z, n_dt, per_b):
    tm = x_ref.shape[0]
    tail = tail_ref.shape[0]
    n_conv = cw_ref.shape[0]

    @pl.when(pl.program_id(0) % per_b == 0)
    def _():
        tail_ref[...] = jnp.zeros(tail_ref.shape, F32)

    x = x_ref[...]
    h = x * lax.rsqrt(jnp.mean(x * x, axis=-1, keepdims=True) + EPS) * nw_ref[...]
    h = (h * (1.0 + scale_ref[0]) + shift_ref[0]).astype(BF16)
    u_new = _dot(h, wzx_ref[:, n_z:])
    z_ref[...] = _dot(h, wzx_ref[:, :n_z])
    ext = jnp.concatenate([tail_ref[...], u_new], axis=0)
    conv = cb_ref[...] + cw_ref[n_conv - 1:n_conv, :] * u_new
    for k in range(n_conv - 1):
        conv = conv + cw_ref[k:k + 1, :] * pltpu.roll(ext, n_conv - 1 - k, axis=0)[tail:, :]
    tail_ref[...] = u_new[tm - tail:, :]
    xa_ref[...] = jax.nn.silu(conv)
    q = _dot(h, wdu_ref[...])
    dt_ref[...] = q[:, :LANES]
    u_ref[...] = q[:, n_dt:n_dt + u_ref.shape[1]]


def _inproj(x2, nw, ada3, w_zx, w_du, conv_w, conv_b, n_u, layer, bsz, seqlen, n_z, n_dt):
    t, d = x2.shape
    n_xbc = w_zx.shape[1] - n_z
    tm = min(512, seqlen)
    per_b = seqlen // tm
    row = lambda i: (i, 0)
    const = lambda a: pl.BlockSpec(a.shape, lambda i: (0, 0))
    ada_blk = lambda k: pl.BlockSpec((1, 1, d), lambda i: (layer * bsz + i // per_b, 0, k))
    return pl.pallas_call(
        functools.partial(_inproj_kernel, n_z=n_z, n_dt=n_dt, per_b=per_b),
        grid=(t // tm,),
        in_specs=[pl.BlockSpec((tm, d), row),
                  pl.BlockSpec((1, d), lambda i: (0, 0)),
                  ada_blk(0), ada_blk(1), const(w_zx), const(w_du), const(conv_w), const(conv_b)],
        out_specs=[pl.BlockSpec((tm, n_z), row), pl.BlockSpec((tm, n_xbc), row),
                   pl.BlockSpec((tm, n_u), row), pl.BlockSpec((tm, LANES), row)],
        out_shape=[jax.ShapeDtypeStruct((t, n_z), F32), jax.ShapeDtypeStruct((t, n_xbc), F32),
                   jax.ShapeDtypeStruct((t, n_u), F32), jax.ShapeDtypeStruct((t, LANES), F32)],
        scratch_shapes=[pltpu.VMEM((8, n_xbc), F32)],
        compiler_params=_params("arbitrary"),
        name="inproj",
    )(x2, nw, ada3, ada3, w_zx, w_du, conv_w, conv_b)


def _ssd_kernel(z_ref, xa_ref, dt_ref, dtb_ref, alog_ref, dexp_ref, nw_ref,
                tri_ref, e_ref, y_ref, st_ref, *, inner, heads, chunks):
    @pl.when(pl.program_id(1) == 0)
    def _():
        st_ref[...] = jnp.zeros(st_ref.shape, F32)

    for ci in range(chunks):
        rows = pl.ds(ci * SSD_CHUNK, SSD_CHUNK)
        _ssd_chunk(z_ref[rows, :], xa_ref[rows, :], dt_ref[rows, :], dtb_ref, alog_ref, dexp_ref, nw_ref,
                   tri_ref, e_ref, y_ref.at[rows, :], st_ref, inner=inner, heads=heads)


def _ssd_chunk(z, xa, dt_raw, dtb_ref, alog_ref, dexp_ref, nw_ref, tri_ref, e_ref, y_ref, st_ref, *, inner, heads):
    L = SSD_CHUNK
    N = SSD_STATE
    hd = inner // heads
    hpg = heads // SSD_GROUPS
    gw = inner // SSD_GROUPS
    xs = xa[:, :inner]
    bm = xa[:, inner:inner + SSD_GROUPS * N]
    cm = xa[:, inner + SSD_GROUPS * N:]

    pre = dt_raw + dtb_ref[...]
    dt = jnp.maximum(pre, 0.0) + jnp.log1p(jnp.exp(-jnp.abs(pre)))
    ad = dt * (-jnp.exp(alog_ref[...]))
    tri = tri_ref[...]
    a1, a2, a3 = _split3(ad)
    acs = _dot(tri, a1) + _dot(tri, a2) + _dot(tri, a3)
    acs_t = acs.T
    dt_t = dt.T
    w = dt * jnp.exp(acs[L - 1:L, :] - acs)
    eacs = jnp.exp(acs)
    e_mat = e_ref[...]
    w1, w2 = _split2(w)
    w_exp = _dot(w1, e_mat) + _dot(w2, e_mat)
    q1, q2 = _split2(eacs)
    eacs_exp = _dot(q1, e_mat) + _dot(q2, e_mat)

    row = lax.broadcasted_iota(jnp.int32, (L, L), 0)
    col = lax.broadcasted_iota(jnp.int32, (L, L), 1)
    causal = row >= col
    lane = lax.broadcasted_iota(jnp.int32, (L, 2 * hd), 1)
    h_prev = st_ref[...]

    y_diag, y_off, st_new = [], [], []
    for g in range(SSD_GROUPS):
        bg = bm[:, g * N:(g + 1) * N]
        cg = cm[:, g * N:(g + 1) * N].astype(BF16)
        cb = lax.dot_general(cg, bg.astype(BF16), (((1,), (1,)), ((), ())), preferred_element_type=F32)
        for j in range(hpg // 2):
            h0 = g * hpg + 2 * j
            lms = []
            for h in (h0, h0 + 1):
                seg = jnp.where(causal, acs[:, h:h + 1] - acs_t[h:h + 1, :], -1e30)
                lms.append((cb * jnp.exp(seg) * dt_t[h:h + 1, :]).astype(BF16))
            pair = xs[:, h0 * hd:(h0 + 2) * hd]
            w_bd = jnp.concatenate([jnp.where(lane < hd, pair, 0.0), jnp.where(lane >= hd, pair, 0.0)],
                                   axis=0).astype(BF16)
            y_diag.append(_dot(jnp.concatenate(lms, axis=1), w_bd))
        sl = slice(g * gw, (g + 1) * gw)
        y_off.append(_dot(cg, h_prev[:, sl].astype(BF16)))
        st_new.append(_dot(bg.T.astype(BF16), (xs[:, sl] * w_exp[:, sl]).astype(BF16)))

    y = jnp.concatenate(y_diag, axis=1) + jnp.concatenate(y_off, axis=1) * eacs_exp + dexp_ref[...] * xs
    st_ref[...] = h_prev * eacs_exp[L - 1:L, :] + jnp.concatenate(st_new, axis=1)
    y = y * jax.nn.silu(z)
    outs = []
    for g in range(SSD_GROUPS):
        yg = y[:, g * gw:(g + 1) * gw]
        outs.append(yg * lax.rsqrt(jnp.mean(yg * yg, axis=-1, keepdims=True) + EPS))
    y_ref[...] = (jnp.concatenate(outs, axis=1) * nw_ref[...]).astype(y_ref.dtype)


def _ssd(z, xa, dtp, dtb, alog, dexp, nw, tri, e_mat, bsz, seqlen, heads):
    t, inner = z.shape
    cch = xa.shape[1]
    chunks = 4
    rows = chunks * SSD_CHUNK
    nb = seqlen // rows
    row = lambda b, c: (b * nb + c, 0)
    const = lambda shape: pl.BlockSpec(shape, lambda b, c: (0, 0))
    return pl.pallas_call(
        functools.partial(_ssd_kernel, inner=inner, heads=heads, chunks=chunks),
        grid=(bsz, nb),
        in_specs=[pl.BlockSpec((rows, inner), row), pl.BlockSpec((rows, cch), row), pl.BlockSpec((rows, LANES), row),
                  const((1, LANES)), const((1, LANES)), const((1, inner)), const((1, inner)),
                  const((SSD_CHUNK, SSD_CHUNK)), const((LANES, inner))],
        out_specs=pl.BlockSpec((rows, inner), row),
        out_shape=jax.ShapeDtypeStruct((t, inner), BF16),
        scratch_shapes=[pltpu.VMEM((SSD_STATE, inner), F32)],
        compiler_params=_params("parallel", "arbitrary"),
        name="ssd",
    )(z, xa, dtp, dtb, alog, dexp, nw, tri, e_mat)


def _s5_kernel(*refs, groups, gch):
    n_t = groups * gch // LANES
    u_refs = refs[:n_t]
    toep_ref, ws_ref, wo_ref, ar_ref, ai_ref = refs[n_t:n_t + 5]
    y_refs = refs[n_t + 5:2 * n_t + 5]
    fold, yfold, sre, sim, pre, pim = refs[2 * n_t + 5:]
    lc = S5_CHUNK
    nc = u_refs[0].shape[0] // lc
    per_tile = LANES // gch
    n_tiles = groups // per_tile
    halves = lc * gch // LANES
    w2 = halves * LANES
    rb = 16
    piece = lax.broadcasted_iota(jnp.int32, (rb, LANES), 1) // gch

    def merge(srcs, shift_of):
        acc = None
        for q, src in enumerate(srcs):
            sh = shift_of(q) * gch
            r = pltpu.roll(src, sh, axis=1) if sh else src
            acc = r if acc is None else jnp.where(piece == q, r, acc)
        return acc

    def fold_rows(b, _):
        r0 = pl.multiple_of(b * rb, rb)
        for j in range(n_tiles):
            tiles = [u_refs[j][pl.ds(b * (rb * lc) + t, rb, stride=lc), :] for t in range(lc)]
            for gp in range(per_tile):
                g = j * per_tile + gp
                parts = [merge(tiles[hv * per_tile:(hv + 1) * per_tile], lambda tt: (tt - gp) % per_tile)
                         for hv in range(halves)]
                fold[g // 2, pl.ds(r0, rb), (g % 2) * w2:(g % 2 + 1) * w2] = (
                    jnp.concatenate(parts, axis=1).astype(BF16))
        return 0

    lax.fori_loop(0, nc // rb, fold_rows, 0)

    n_pairs = groups // 2
    for k in range(n_pairs):
        inc = _dot(fold[k], ws_ref[k])
        sre[:, k * LANES:(k + 1) * LANES] = inc[:, :LANES]
        sim[:, k * LANES:(k + 1) * LANES] = inc[:, LANES:]

    ar = ar_ref[...]
    ai = ai_ref[...]

    def body(c, carry):
        r, i = carry
        pre[pl.ds(c, 1), :] = r
        pim[pl.ds(c, 1), :] = i
        return ar * r - ai * i + sre[pl.ds(c, 1), :], ar * i + ai * r + sim[pl.ds(c, 1), :]

    zero = jnp.zeros((1, sre.shape[1]), F32)
    lax.fori_loop(0, nc, body, (zero, zero))

    for k in range(n_pairs):
        a = fold[k]
        intra = jnp.concatenate([_dot(a[:, :w2], toep_ref[2 * k]), _dot(a[:, w2:], toep_ref[2 * k + 1])], axis=1)
        state = jnp.concatenate([pre[:, k * LANES:(k + 1) * LANES], pim[:, k * LANES:(k + 1) * LANES]], axis=1)
        yfold[k] = intra + _dot(state.astype(BF16), wo_ref[k])

    def unfold_rows(b, _):
        r0 = pl.multiple_of(b * rb, rb)
        for j in range(n_tiles):
            for hv in range(halves):
                srcs = []
                for gp in range(per_tile):
                    g = j * per_tile + gp
                    srcs.append(yfold[g // 2, pl.ds(r0, rb), pl.ds((g % 2) * w2 + hv * LANES, LANES)])
                for tt in range(per_tile):
                    y_refs[j][pl.ds(b * (rb * lc) + hv * per_tile + tt, rb, stride=lc), :] = merge(
                        srcs, lambda gp: (gp - tt) % per_tile)
        return 0

    lax.fori_loop(0, nc // rb, unfold_rows, 0)


def _s5_core(u5, toep, ws, wo, a_re, a_im, layer, bsz, seqlen, gch):
    t, n5 = u5.shape
    groups = n5 // gch
    nc = seqlen // S5_CHUNK
    n_t = n5 // LANES
    const = lambda a: pl.BlockSpec((None,) + a.shape[1:], lambda b: (layer,) + (0,) * (a.ndim - 1),
                                   pipeline_mode=pl.Buffered(1))
    col = lambda j: pl.BlockSpec((seqlen, LANES), lambda b: (b, j))
    return pl.pallas_call(
        functools.partial(_s5_kernel, groups=groups, gch=gch),
        grid=(bsz,),
        in_specs=[col(j) for j in range(n_t)] + [const(toep), const(ws), const(wo), const(a_re), const(a_im)],
        out_specs=[pl.BlockSpec((seqlen, LANES), lambda b: (b, 0))] * n_t,
        out_shape=[jax.ShapeDtypeStruct((t, LANES), F32)] * n_t,
        scratch_shapes=[pltpu.VMEM((groups // 2, nc, 2 * S5_CHUNK * gch), BF16),
                        pltpu.VMEM((groups // 2, nc, 2 * S5_CHUNK * gch), F32)]
        + [pltpu.VMEM((nc, a_re.shape[2]), F32)] * 4,
        compiler_params=_params("parallel"),
        name="s5_core",
    )(*([u5] * n_t), toep, ws, wo, a_re, a_im)


def _s5_weights(lam_re, lam_im, log_dt, b_re, b_im, c_re, c_im):
    lc = S5_CHUNK
    nl, g, p, h = b_re.shape
    step = jnp.exp(log_dt)[..., None]
    mag = jnp.exp(lam_re * step)
    ab_re, ab_im = mag * jnp.cos(lam_im * step), mag * jnp.sin(lam_im * step)
    den = lam_re * lam_re + lam_im * lam_im
    nr = ab_re - 1.0
    coef_re = (nr * lam_re + ab_im * lam_im) / den
    coef_im = (ab_im * lam_re - nr * lam_im) / den
    swap = lambda a: a.transpose(0, 1, 3, 2)
    bt_re, bt_im = swap(b_re), swap(b_im)
    bbt_re = coef_re[:, :, None, :] * bt_re - coef_im[:, :, None, :] * bt_im
    bbt_im = coef_re[:, :, None, :] * bt_im + coef_im[:, :, None, :] * bt_re
    jj = jnp.arange(lc + 1, dtype=F32)
    pmag = jnp.exp((lam_re * step)[..., None] * jj)
    pw_re = pmag * jnp.cos((lam_im * step)[..., None] * jj)
    pw_im = pmag * jnp.sin((lam_im * step)[..., None] * jj)
    ct_re, ct_im = swap(c_re), swap(c_im)
    wide = lambda a: a.reshape(nl, g, p, (lc + 1) * h)
    cpt_re = wide(ct_re[:, :, :, None, :] * pw_re[..., None] - ct_im[:, :, :, None, :] * pw_im[..., None])
    cpt_im = wide(ct_re[:, :, :, None, :] * pw_im[..., None] + ct_im[:, :, :, None, :] * pw_re[..., None])
    w = lc * h
    k2 = (jnp.einsum('lghp,lgpc->lghc', bbt_re, cpt_re[..., :w], precision='highest')
          - jnp.einsum('lghp,lgpc->lghc', bbt_im, cpt_im[..., :w], precision='highest'))
    toep = jnp.stack([jnp.pad(k2, ((0, 0), (0, 0), (0, 0), (ti * h, 0)))[..., :w] for ti in range(lc)], axis=2)
    toep = toep.reshape(nl, g, w, w)
    rev_re = swap(jnp.flip(pw_re[..., :lc], axis=-1))
    rev_im = swap(jnp.flip(pw_im[..., :lc], axis=-1))
    ws_re = (rev_re[:, :, :, None, :] * bbt_re[:, :, None] - rev_im[:, :, :, None, :] * bbt_im[:, :, None]
             ).reshape(nl, g, w, p)
    ws_im = (rev_re[:, :, :, None, :] * bbt_im[:, :, None] + rev_im[:, :, :, None, :] * bbt_re[:, :, None]
             ).reshape(nl, g, w, p)
    wo_re, wo_im = cpt_re[..., h:], -cpt_im[..., h:]

    def pair(a):
        a = a.reshape((nl, g // 2, 2) + a.shape[2:])
        z = jnp.zeros_like(a[:, :, 0])
        return jnp.concatenate([jnp.concatenate([a[:, :, 0], z], axis=-1),
                                jnp.concatenate([z, a[:, :, 1]], axis=-1)], axis=-2).astype(BF16)

    ws = jnp.concatenate([pair(ws_re), pair(ws_im)], axis=-1)
    wo = jnp.concatenate([pair(wo_re), pair(wo_im)], axis=-2)
    return (toep.astype(BF16), ws, wo, pw_re[..., lc].reshape(nl, 1, g * p), pw_im[..., lc].reshape(nl, 1, g * p))


def _outproj_kernel(x_ref, ys_ref, *refs):
    n_t = len(refs) - 17
    y5_refs = refs[:n_t]
    (u5_ref, d5_ref, wg_ref, bg_ref, n5_ref, wa_ref, wb_ref, gate_ref, nw_ref, shift_ref, scale_ref,
     wrh_ref, wrl_ref, br_ref, x1_ref, h2_ref, lg_ref) = refs[n_t:]
    y = jnp.concatenate([r[...] for r in y5_refs], axis=1) + d5_ref[...] * u5_ref[...]
    y = jax.nn.gelu(y)
    y = y * jax.nn.sigmoid(_dot(y.astype(BF16), wg_ref[...]) + bg_ref[...])
    y = y * lax.rsqrt(jnp.mean(y * y, axis=-1, keepdims=True) + EPS) * n5_ref[...]
    m = _dot(ys_ref[...], wa_ref[...]) + _dot(y.astype(BF16), wb_ref[...])
    x1 = x_ref[...] + gate_ref[0] * m
    x1_ref[...] = x1
    h = x1 * lax.rsqrt(jnp.mean(x1 * x1, axis=-1, keepdims=True) + EPS) * nw_ref[...]
    h = h * (1.0 + scale_ref[0]) + shift_ref[0]
    h2_ref[...] = _pack_rows(h)
    h_hi, h_lo = _split2(h)
    wrh = wrh_ref[...]
    lg = _dot(h_hi, wrh) + _dot(h_lo, wrh) + _dot(h_hi, wrl_ref[...]) + br_ref[...]
    lg_ref[...] = lg.T[:ROUTER_ROWS, :]


def _outproj(x2, y_ssd, y5, u5, d5, w_glu, b_glu, n5, w_a, w_b, nw, ada3, wr_hi, wr_lo, br,
             layer, bsz, seqlen):
    t, d = x2.shape
    n_s = y_ssd.shape[1]
    n_5 = u5.shape[1]
    tm = min(512, seqlen)
    per_b = seqlen // tm
    row = lambda i: (i, 0)
    const = lambda a: pl.BlockSpec(a.shape, lambda i: (0, 0))
    ada_blk = lambda k: pl.BlockSpec((1, 1, d), lambda i: (layer * bsz + i // per_b, 0, k))
    return pl.pallas_call(
        _outproj_kernel,
        grid=(t // tm,),
        in_specs=[pl.BlockSpec((tm, d), row), pl.BlockSpec((tm, n_s), row)]
        + [pl.BlockSpec((tm, LANES), row)] * len(y5)
        + [pl.BlockSpec((tm, n_5), row), const(d5), const(w_glu), const(b_glu), const(n5),
                  const(w_a), const(w_b), ada_blk(2), const(nw), ada_blk(3), ada_blk(4),
                  const(wr_hi), const(wr_lo), const(br)],
        out_specs=[pl.BlockSpec((tm, d), row), pl.BlockSpec((tm, d // 2), row),
                   pl.BlockSpec((ROUTER_ROWS, tm), lambda i: (0, i))],
        out_shape=[jax.ShapeDtypeStruct((t, d), F32), jax.ShapeDtypeStruct((t, d // 2), jnp.uint32),
                   jax.ShapeDtypeStruct((ROUTER_ROWS, t), F32)],
        compiler_params=_params("parallel"),
        name="outproj",
    )(x2, y_ssd, *y5, u5, d5, w_glu, b_glu, n5, w_a, w_b, ada3, nw, ada3, ada3, wr_hi, wr_lo, br)


def _route_kernel(lg_ref, upper_ref, dest_ref, gate_ref, cnt_ref, counts, carry, pstart,
                  *, n_experts, per_group, block_rows, sub):
    ph = pl.program_id(0)
    i = pl.program_id(1)
    tr = lg_ref.shape[1]
    lg = lg_ref[...]

    @pl.when((ph == 0) & (i == 0))
    def _():
        counts[...] = jnp.zeros(counts.shape, F32)

    gl = [lg[n_experts + k:n_experts + k + 1, :] for k in range(N_EXPERT_GROUPS)]
    gmax = functools.reduce(jnp.maximum, gl)
    gidx = jnp.full((1, tr), N_EXPERT_GROUPS - 1, jnp.int32)
    for k in range(N_EXPERT_GROUPS - 2, -1, -1):
        gidx = jnp.where(gl[k] == gmax, k, gidx)
    gsum = functools.reduce(lambda a, b: a + b, [jnp.exp(v - gmax) for v in gl])
    g_w = 1.0 / gsum
    el = lg[0:per_group, :]
    for k in range(1, N_EXPERT_GROUPS):
        el = jnp.where(gidx == k, lg[k * per_group:(k + 1) * per_group, :], el)
    ep = jnp.exp(el - jnp.max(el, axis=0, keepdims=True))
    prob = ep / jnp.sum(ep, axis=0, keepdims=True)
    jj = lax.broadcasted_iota(jnp.int32, (per_group, tr), 0).astype(F32)
    p1 = jnp.max(prob, axis=0, keepdims=True)
    i1 = jnp.min(jnp.where(prob == p1, jj, float(per_group)), axis=0, keepdims=True)
    prob2 = jnp.where(jj == i1, -1.0, prob)
    p2 = jnp.max(prob2, axis=0, keepdims=True)
    i2 = jnp.min(jnp.where(prob2 == p2, jj, float(per_group)), axis=0, keepdims=True)
    den = p1 + p2
    gate_ref[0] = jnp.concatenate([g_w * p1 / den, g_w * p2 / den], axis=0)
    e1 = gidx * per_group + i1.astype(jnp.int32)
    e2 = gidx * per_group + i2.astype(jnp.int32)
    rr = lax.broadcasted_iota(jnp.int32, (n_experts, tr), 0)
    oh1 = rr == e1
    oh2 = rr == e2
    member = jnp.where(oh1 | oh2, 1.0, 0.0)

    @pl.when(ph == 0)
    def _():
        counts[...] = counts[...] + jnp.sum(member, axis=1, keepdims=True)
        dest_ref[...] = jnp.zeros(dest_ref.shape, jnp.int32)

    @pl.when(ph == 1)
    def _():
        @pl.when(i == 0)
        def _():
            blocks = (counts[...].astype(jnp.int32) + (block_rows - 1)) >> int(math.log2(block_rows))
            hi = (blocks >> 4).astype(F32).astype(BF16)
            lo = (blocks & 15).astype(F32).astype(BF16)
            er = lax.broadcasted_iota(jnp.int32, (n_experts, n_experts), 0)
            ec = lax.broadcasted_iota(jnp.int32, (n_experts, n_experts), 1)
            lower = jnp.where(ec < er, 1.0, 0.0).astype(BF16)
            pstart[...] = (16.0 * _dot(lower, hi) + _dot(lower, lo)) * float(block_rows)
            carry[...] = jnp.zeros(carry.shape, F32)

        run = carry[...]
        upper = upper_ref[...]
        pieces = []
        for b in range(tr // sub):
            mb = member[:, b * sub:(b + 1) * sub]
            pieces.append(_dot(mb.astype(BF16), upper) + jnp.concatenate([run] * (sub // LANES), axis=1))
            run = run + jnp.sum(mb, axis=1, keepdims=True)
        carry[...] = run
        base = jnp.concatenate(pieces, axis=1) + jnp.concatenate([pstart[...]] * (tr // LANES), axis=1)
        d1 = jnp.sum(jnp.where(oh1, base, 0.0), axis=0, keepdims=True)
        d2 = jnp.sum(jnp.where(oh2, base, 0.0), axis=0, keepdims=True)
        dest_ref[0] = jnp.concatenate([d1, d2], axis=0).astype(jnp.int32)

    cnt_ref[...] = counts[...].astype(jnp.int32)


def _route(logits_t, upper, n_experts, block_rows):
    rows, t = logits_t.shape
    tr = 1024
    sub = upper.shape[0]
    return pl.pallas_call(
        functools.partial(_route_kernel, n_experts=n_experts, per_group=n_experts // N_EXPERT_GROUPS,
                          block_rows=block_rows, sub=sub),
        grid=(2, t // tr),
        in_specs=[pl.BlockSpec((rows, tr), lambda ph, i: (0, i)),
                  pl.BlockSpec(upper.shape, lambda ph, i: (0, 0))],
        out_specs=[pl.BlockSpec((1, TOP_K, tr), lambda ph, i: (ph, 0, i)),
                   pl.BlockSpec((1, TOP_K, tr), lambda ph, i: (ph, 0, i)),
                   pl.BlockSpec((n_experts, LANES), lambda ph, i: (0, 0))],
        out_shape=[jax.ShapeDtypeStruct((2, TOP_K, t), jnp.int32), jax.ShapeDtypeStruct((2, TOP_K, t), F32),
                   jax.ShapeDtypeStruct((n_experts, LANES), jnp.int32)],
        scratch_shapes=[pltpu.VMEM((n_experts, LANES), F32)] * 3,
        compiler_params=_params("arbitrary", "arbitrary"),
        name="route",
    )(logits_t, upper)


DMA_UNROLL = 8


def _dispatch_kernel(dest_ref, h_ref, init_ref, rows_ref, sem, *, tq, t):
    del init_ref
    base = pl.program_id(0) * tq

    def issue(jb, _):
        j0 = pl.multiple_of(jb * DMA_UNROLL, DMA_UNROLL)
        for u in range(DMA_UNROLL):
            for k in range(TOP_K):
                pltpu.make_async_copy(h_ref.at[pl.ds(j0 + u, 1)],
                                      rows_ref.at[pl.ds(dest_ref[k * t + base + j0 + u], 1)], sem).start(priority=k % 2)
        return 0

    lax.fori_loop(0, tq // DMA_UNROLL, issue, 0)
    for k in range(TOP_K):
        pltpu.make_async_copy(h_ref, rows_ref.at[pl.ds(0, tq)], sem).wait()


def _dispatch(dest_flat, h2, n_pad):
    t, d = h2.shape
    tq = min(1024, t)
    init = jnp.zeros((n_pad, d), h2.dtype)
    return pl.pallas_call(
        functools.partial(_dispatch_kernel, tq=tq, t=t),
        grid_spec=pltpu.PrefetchScalarGridSpec(
            num_scalar_prefetch=1, grid=(t // tq,),
            in_specs=[pl.BlockSpec((tq, d), lambda i, dr: (i, 0)), pl.BlockSpec(memory_space=pl.ANY)],
            out_specs=pl.BlockSpec(memory_space=pl.ANY),
            scratch_shapes=[pltpu.SemaphoreType.DMA]),
        out_shape=jax.ShapeDtypeStruct((n_pad, d), h2.dtype),
        input_output_aliases={2: 0},
        compiler_params=_params("arbitrary"),
        name="dispatch",
    )(dest_flat, h2, init)


def _ffn_kernel(be_ref, first_ref, slot_ref, nxt_ref, hasnext_ref, nused_ref, x_ref, wg_hbm, wu_hbm, wd_hbm,
                y_ref, sg, su, sd, wg16, wu16, wd16, sems, *, layer):
    i = pl.program_id(0)

    def weight_copies(e, s):
        return (pltpu.make_async_copy(wg_hbm.at[layer, e], sg.at[s], sems.at[s, 0]),
                pltpu.make_async_copy(wu_hbm.at[layer, e], su.at[s], sems.at[s, 1]),
                pltpu.make_async_copy(wd_hbm.at[layer, e], sd.at[s], sems.at[s, 2]))

    @pl.when(first_ref[i] == 1)
    def _():
        s = slot_ref[i]
        e = be_ref[i]

        @pl.when(i == 0)
        def _():
            for cp in weight_copies(e, s):
                cp.start()

        for cp in weight_copies(e, s):
            cp.wait()
        wg16[...] = sg[s].astype(BF16)
        wu16[...] = su[s].astype(BF16)
        wd16[...] = sd[s].astype(BF16)

        @pl.when(hasnext_ref[i] == 1)
        def _():
            for cp in weight_copies(nxt_ref[i], 1 - s):
                cp.start()

    @pl.when(i < nused_ref[0])
    def _():
        xb = _unpack_rows(x_ref[...]).astype(BF16)
        a = jax.nn.silu(_dot(xb, wg16[...])) * _dot(xb, wu16[...])
        y_ref[...] = _pack_rows(_dot(a.astype(BF16), wd16[...]))

    @pl.when(i >= nused_ref[0])
    def _():
        y_ref[...] = jnp.zeros(y_ref.shape, y_ref.dtype)


def _ffn(counts, rows, w_eg, w_eu, w_ed, layer, bm):
    n_pad, dw = rows.shape
    _, n_experts, d, ff = w_eg.shape
    n_blocks = n_pad // bm
    ends = jnp.cumsum((counts + bm - 1) // bm).astype(jnp.int32)
    n_used = ends[-1]
    bidx = jnp.arange(n_blocks, dtype=jnp.int32)
    be = jnp.minimum(jnp.sum(ends[None, :] <= bidx[:, None], axis=1), n_experts - 1).astype(jnp.int32)
    be = jnp.where(bidx < n_used, be, be[n_used - 1])
    first = jnp.concatenate([jnp.ones((1,), jnp.int32), (be[1:] != be[:-1]).astype(jnp.int32)])
    slot = ((jnp.cumsum(first) - 1) & 1).astype(jnp.int32)
    seg_end = ends[be]
    has_next = (seg_end < n_used).astype(jnp.int32)
    nxt = be[jnp.minimum(seg_end, n_blocks - 1)]
    row = lambda i, *_: (i, 0)
    return pl.pallas_call(
        functools.partial(_ffn_kernel, layer=layer),
        grid_spec=pltpu.PrefetchScalarGridSpec(
            num_scalar_prefetch=6, grid=(n_blocks,),
            in_specs=[pl.BlockSpec((bm, dw), row)] + [pl.BlockSpec(memory_space=pl.ANY)] * 3,
            out_specs=pl.BlockSpec((bm, dw), row),
            scratch_shapes=[pltpu.VMEM((2, d, ff), F32), pltpu.VMEM((2, d, ff), F32), pltpu.VMEM((2, ff, d), F32),
                            pltpu.VMEM((d, ff), BF16), pltpu.VMEM((d, ff), BF16), pltpu.VMEM((ff, d), BF16),
                            pltpu.SemaphoreType.DMA((2, 3))]),
        out_shape=jax.ShapeDtypeStruct((n_pad, dw), jnp.uint32),
        compiler_params=_params("arbitrary"),
        name="ffn",
    )(be, first, slot, nxt, has_next, n_used[None], rows, w_eg, w_eu, w_ed)


def _combine_kernel(dest_ref, yrows_ref, x_ref, g_ref, gate_ref, fw_ref, o_ref, buf, sems, *, tq, t, final):
    i = pl.program_id(0)
    n = pl.num_programs(0)

    def issue(tile):
        slot = tile % 2

        def body(jb, _):
            j0 = pl.multiple_of(jb * DMA_UNROLL, DMA_UNROLL)
            for u in range(DMA_UNROLL):
                for k in range(TOP_K):
                    pltpu.make_async_copy(yrows_ref.at[pl.ds(dest_ref[k * t + tile * tq + j0 + u], 1)],
                                          buf.at[slot, k, pl.ds(j0 + u, 1)], sems.at[slot]).start(priority=k % 2)
            return 0
        lax.fori_loop(0, tq // DMA_UNROLL, body, 0)

    @pl.when(i == 0)
    def _():
        issue(i)

    @pl.when(i + 1 < n)
    def _():
        issue(i + 1)

    for k in range(TOP_K):
        pltpu.make_async_copy(yrows_ref.at[pl.ds(0, tq)], buf.at[i % 2, k], sems.at[i % 2]).wait()
    g = g_ref[...]
    rows = buf[i % 2]
    f = g[:, 0:1] * _unpack_rows(rows[0]) + g[:, 1:2] * _unpack_rows(rows[1])
    x2 = x_ref[...] + gate_ref[0] * f
    if final:
        x2 = x2 * lax.rsqrt(jnp.mean(x2 * x2, axis=-1, keepdims=True) + EPS) * fw_ref[...]
    o_ref[...] = x2


def _combine(dest_flat, y_rows, x1, gates_t, ada3, fw, layer, bsz, seqlen, final):
    t, d = x1.shape
    tq = min(1024, seqlen)
    per_b = seqlen // tq
    return pl.pallas_call(
        functools.partial(_combine_kernel, tq=tq, t=t, final=final),
        grid_spec=pltpu.PrefetchScalarGridSpec(
            num_scalar_prefetch=1, grid=(t // tq,),
            in_specs=[pl.BlockSpec(memory_space=pl.ANY),
                      pl.BlockSpec((tq, d), lambda i, dr: (i, 0)),
                      pl.BlockSpec((tq, TOP_K), lambda i, dr: (i, 0)),
                      pl.BlockSpec((1, 1, d), lambda i, dr: (layer * bsz + i // per_b, 0, 5)),
                      pl.BlockSpec((1, d), lambda i, dr: (0, 0))],
            out_specs=pl.BlockSpec((tq, d), lambda i, dr: (i, 0)),
            scratch_shapes=[pltpu.VMEM((2, TOP_K, tq, d // 2), jnp.uint32), pltpu.SemaphoreType.DMA((2,))]),
        out_shape=jax.ShapeDtypeStruct((t, d), F32),
        compiler_params=_params("arbitrary"),
        name="combine",
    )(dest_flat, y_rows, x1, gates_t, ada3, fw)


def kernel(x, c, w_ada, b_ada, norm1_w, w_in, conv_w, conv_b, dt_bias, a_log, d_ssd, ssd_norm_w,
           s5_lam_re, s5_lam_im, s5_log_dt, s5_b_re, s5_b_im, s5_c_re, s5_c_im, s5_d, w_glu, b_glu,
           s5_norm_w, w_out, norm2_w, w_rg, b_rg, w_re, b_re, w_eg, w_eu, w_ed, final_norm_w):
    bsz, seqlen, d = x.shape
    t = bsz * seqlen
    depth = w_in.shape[0]
    heads = dt_bias.shape[1]
    inner = ssd_norm_w.shape[1]
    cch = conv_w.shape[2]
    n5 = s5_d.shape[1]
    s5_groups, _, s5_h = s5_b_re.shape[1:]
    n_experts = w_re.shape[2]
    moe_block = 512
    n_rows = t * TOP_K
    n_pad = n_rows + n_experts * moe_block
    n_blocks = n_pad // moe_block

    i0, i1, i2 = inner, inner + cch, inner + cch + heads
    w_zx = w_in[..., :i1].astype(BF16)
    n_du = w_in.shape[2] - i1
    w_du = jnp.pad(w_in[..., i1:].astype(BF16), ((0, 0), (0, 0), (0, -n_du % LANES)))
    lane_pad = lambda a: jnp.pad(a, ((0, 0), (0, LANES - a.shape[1])))[:, None, :]
    dtb, alog = lane_pad(dt_bias), lane_pad(a_log)
    dexp = jnp.repeat(d_ssd, inner // heads, axis=1)[:, None, :]
    li = jnp.arange(SSD_CHUNK)
    tri = (li[None, :] <= li[:, None]).astype(BF16)
    e_mat = (jnp.arange(LANES)[:, None] == (jnp.arange(inner) // (inner // heads))[None, :]).astype(BF16)
    w_out_a = w_out[:, :inner].astype(BF16)
    w_out_b = w_out[:, inner:].astype(BF16)
    w_glu_b = w_glu.astype(BF16)
    wr = jnp.concatenate([w_re, w_rg, jnp.zeros((depth, d, LANES - n_experts - N_EXPERT_GROUPS), F32)], axis=-1)
    wr_hi = wr.astype(BF16)
    wr_lo = (wr - wr_hi.astype(F32)).astype(BF16)
    br = jnp.concatenate([b_re, b_rg, jnp.zeros((depth, LANES - n_experts - N_EXPERT_GROUPS), F32)], axis=-1)[:, None, :]
    sub = 256
    si = jnp.arange(sub)
    upper = (si[:, None] < si[None, :]).astype(BF16)

    ada3 = _ada(c, w_ada, b_ada).reshape(depth * bsz, 1, 6 * d)
    x2 = x.reshape(t, d)
    s5w = _s5_weights(s5_lam_re, s5_lam_im, s5_log_dt, s5_b_re, s5_b_im, s5_c_re, s5_c_im)
    for l in range(depth):
        z, xa, u5, dtp = _inproj(x2, norm1_w[l][None], ada3, w_zx[l], w_du[l], conv_w[l], conv_b[l][None], n5, l,
                                 bsz, seqlen, inner, heads)
        y_ssd = _ssd(z, xa, dtp, dtb[l], alog[l], dexp[l], ssd_norm_w[l][None], tri, e_mat, bsz, seqlen, heads)
        y5 = _s5_core(u5, *s5w, l, bsz, seqlen, s5_h)
        x1, h2, logits_t = _outproj(x2, y_ssd, y5, u5, s5_d[l][None], w_glu_b[l], b_glu[l][None],
                                    s5_norm_w[l][None], w_out_a[l], w_out_b[l], norm2_w[l][None], ada3,
                                    wr_hi[l], wr_lo[l], br[l], l, bsz, seqlen)
        dest, gates, counts = _route(logits_t, upper, n_experts, moe_block)
        dest_flat = dest[1].reshape(-1)
        rows = _dispatch(dest_flat, h2, n_pad)
        y_rows = _ffn(counts[:, 0], rows, w_eg, w_eu, w_ed, l, moe_block)
        x2 = _combine(dest_flat, y_rows, x1, gates[1].T, ada3, final_norm_w[None], l, bsz, seqlen, l == depth - 1)
    return x2.reshape(bsz, seqlen, d)
```

```python
import functools
import math

import jax
import jax.numpy as jnp
from jax import lax
from jax.experimental import pallas as pl
from jax.experimental.pallas import tpu as pltpu

F32 = jnp.float32
BF16 = jnp.bfloat16
EPS = 1e-6

SSD_GROUPS = 2
SSD_STATE = 128
SSD_CHUNK = 128
S5_CHUNK = 16
N_EXPERT_GROUPS = 4
TOP_K = 2
LANES = 128
ROUTER_ROWS = 40
VMEM_LIMIT = 48 * 1024 * 1024


def _dot(a, b):
    return jnp.dot(a, b, preferred_element_type=F32)


def _split2(a):
    hi = a.astype(BF16)
    lo = (a - hi.astype(F32)).astype(BF16)
    return hi, lo


def _split3(a):
    hi = a.astype(BF16)
    r = a - hi.astype(F32)
    mid = r.astype(BF16)
    lo = (r - mid.astype(F32)).astype(BF16)
    return hi, mid, lo


def _pack_rows(x):
    w = x.shape[1] // 2
    bits = lambda v: lax.bitcast_convert_type(v.astype(BF16).astype(F32), jnp.uint32)
    return (bits(x[:, :w]) >> 16) | bits(x[:, w:])


def _unpack_rows(words):
    lo = lax.bitcast_convert_type(words << 16, F32)
    hi = lax.bitcast_convert_type(words & jnp.uint32(0xFFFF0000), F32)
    return jnp.concatenate([lo, hi], axis=1)


def _params(*sem):
    return pltpu.CompilerParams(dimension_semantics=sem, vmem_limit_bytes=VMEM_LIMIT)


def _ada_kernel(c_ref, w_ref, b_ref, o_ref):
    ca = jax.nn.silu(c_ref[...])
    c_hi, c_lo = _split2(ca)
    w_hi, w_lo = _split2(w_ref[0])
    o_ref[0] = _dot(c_hi, w_hi) + _dot(c_lo, w_hi) + _dot(c_hi, w_lo) + b_ref[0]


def _ada(c, w_ada, b_ada):
    n_layers, d, n_out = w_ada.shape
    bsz = c.shape[0]
    tn = 1536
    return pl.pallas_call(
        _ada_kernel,
        grid=(n_layers, n_out // tn),
        in_specs=[pl.BlockSpec((bsz, d), lambda l, j: (0, 0)),
                  pl.BlockSpec((1, d, tn), lambda l, j: (l, 0, j)),
                  pl.BlockSpec((1, 1, tn), lambda l, j: (l, 0, j))],
        out_specs=pl.BlockSpec((1, bsz, tn), lambda l, j: (l, 0, j)),
        out_shape=jax.ShapeDtypeStruct((n_layers, bsz, n_out), F32),
        compiler_params=_params("parallel", "parallel"),
        name="ada",
    )(c, w_ada, b_ada.reshape(n_layers, 1, n_out))


def _inproj_kernel(x_ref, nw_ref, shift_ref, scale_ref, wzx_ref, wdu_ref, cw_ref, cb_ref,
                   z_ref, xa_ref, u_ref, dt_ref, tail_ref, *, n_z, n_dt, per_b):
    tm = x_ref.shape[0]
    tail = tail_ref.shape[0]
    n_conv = cw_ref.shape[0]

    @pl.when(pl.program_id(0) % per_b == 0)
    def _():
        tail_ref[...] = jnp.zeros(tail_ref.shape, F32)

    x = x_ref[...]
    h = x * lax.rsqrt(jnp.mean(x * x, axis=-1, keepdims=True) + EPS) * nw_ref[...]
    h = (h * (1.0 + scale_ref[0]) + shift_ref[0]).astype(BF16)
    u_new = _dot(h, wzx_ref[:, n_z:])
    z_ref[...] = _dot(h, wzx_ref[:, :n_z])
    ext = jnp.concatenate([tail_ref[...], u_new], axis=0)
    conv = cb_ref[...] + cw_ref[n_conv - 1:n_conv, :] * u_new
    for k in range(n_conv - 1):
        conv = conv + cw_ref[k:k + 1, :] * pltpu.roll(ext, n_conv - 1 - k, axis=0)[tail:, :]
    tail_ref[...] = u_new[tm - tail:, :]
    xa_ref[...] = jax.nn.silu(conv)
    q = _dot(h, wdu_ref[...])
    dt_ref[...] = q[:, :LANES]
    u_ref[...] = q[:, n_dt:n_dt + u_ref.shape[1]]


def _inproj(x2, nw, ada3, w_zx, w_du, conv_w, conv_b, n_u, layer, bsz, seqlen, n_z, n_dt):
    t, d = x2.shape
    n_xbc = w_zx.shape[1] - n_z
    tm = min(512, seqlen)
    per_b = seqlen // tm
    row = lambda i: (i, 0)
    const = lambda a: pl.BlockSpec(a.shape, lambda i: (0, 0))
    ada_blk = lambda k: pl.BlockSpec((1, 1, d), lambda i: (layer * bsz + i // per_b, 0, k))
    return pl.pallas_call(
        functools.partial(_inproj_kernel, n_z=n_z, n_dt=n_dt, per_b=per_b),
        grid=(t // tm,),
        in_specs=[pl.BlockSpec((tm, d), row),
                  pl.BlockSpec((1, d), lambda i: (0, 0)),
                  ada_blk(0), ada_blk(1), const(w_zx), const(w_du), const(conv_w), const(conv_b)],
        out_specs=[pl.BlockSpec((tm, n_z), row), pl.BlockSpec((tm, n_xbc), row),
                   pl.BlockSpec((tm, n_u), row), pl.BlockSpec((tm, LANES), row)],
        out_shape=[jax.ShapeDtypeStruct((t, n_z), F32), jax.ShapeDtypeStruct((t, n_xbc), F32),
                   jax.ShapeDtypeStruct((t, n_u), F32), jax.ShapeDtypeStruct((t, LANES), F32)],
        scratch_shapes=[pltpu.VMEM((8, n_xbc), F32)],
        compiler_params=_params("arbitrary"),
        name="inproj",
    )(x2, nw, ada3, ada3, w_zx, w_du, conv_w, conv_b)


def _ssd_kernel(z_ref, xa_ref, dt_ref, dtb_ref, alog_ref, dexp_ref, nw_ref,
                tri_ref, e_ref, y_ref, st_ref, *, inner, heads, chunks):
    @pl.when(pl.program_id(1) == 0)
    def _():
        st_ref[...] = jnp.zeros(st_ref.shape, F32)

    for ci in range(chunks):
        rows = pl.ds(ci * SSD_CHUNK, SSD_CHUNK)
        _ssd_chunk(z_ref[rows, :], xa_ref[rows, :], dt_ref[rows, :], dtb_ref, alog_ref, dexp_ref, nw_ref,
                   tri_ref, e_ref, y_ref.at[rows, :], st_ref, inner=inner, heads=heads)


def _ssd_chunk(z, xa, dt_raw, dtb_ref, alog_ref, dexp_ref, nw_ref, tri_ref, e_ref, y_ref, st_ref, *, inner, heads):
    L = SSD_CHUNK
    N = SSD_STATE
    hd = inner // heads
    hpg = heads // SSD_GROUPS
    gw = inner // SSD_GROUPS
    xs = xa[:, :inner]
    bm = xa[:, inner:inner + SSD_GROUPS * N]
    cm = xa[:, inner + SSD_GROUPS * N:]

    pre = dt_raw + dtb_ref[...]
    dt = jnp.maximum(pre, 0.0) + jnp.log1p(jnp.exp(-jnp.abs(pre)))
    ad = dt * (-jnp.exp(alog_ref[...]))
    tri = tri_ref[...]
    a1, a2, a3 = _split3(ad)
    acs = _dot(tri, a1) + _dot(tri, a2) + _dot(tri, a3)
    acs_t = acs.T
    dt_t = dt.T
    w = dt * jnp.exp(acs[L - 1:L, :] - acs)
    eacs = jnp.exp(acs)
    e_mat = e_ref[...]
    w1, w2 = _split2(w)
    w_exp = _dot(w1, e_mat) + _dot(w2, e_mat)
    q1, q2 = _split2(eacs)
    eacs_exp = _dot(q1, e_mat) + _dot(q2, e_mat)

    row = lax.broadcasted_iota(jnp.int32, (L, L), 0)
    col = lax.broadcasted_iota(jnp.int32, (L, L), 1)
    causal = row >= col
    lane = lax.broadcasted_iota(jnp.int32, (L, 2 * hd), 1)
    h_prev = st_ref[...]

    y_diag, y_off, st_new = [], [], []
    for g in range(SSD_GROUPS):
        bg = bm[:, g * N:(g + 1) * N]
        cg = cm[:, g * N:(g + 1) * N].astype(BF16)
        cb = lax.dot_general(cg, bg.astype(BF16), (((1,), (1,)), ((), ())), preferred_element_type=F32)
        for j in range(hpg // 2):
            h0 = g * hpg + 2 * j
            lms = []
            for h in (h0, h0 + 1):
                seg = jnp.where(causal, acs[:, h:h + 1] - acs_t[h:h + 1, :], -1e30)
                lms.append((cb * jnp.exp(seg) * dt_t[h:h + 1, :]).astype(BF16))
            pair = xs[:, h0 * hd:(h0 + 2) * hd]
            w_bd = jnp.concatenate([jnp.where(lane < hd, pair, 0.0), jnp.where(lane >= hd, pair, 0.0)],
                                   axis=0).astype(BF16)
            y_diag.append(_dot(jnp.concatenate(lms, axis=1), w_bd))
        sl = slice(g * gw, (g + 1) * gw)
        y_off.append(_dot(cg, h_prev[:, sl].astype(BF16)))
        st_new.append(_dot(bg.T.astype(BF16), (xs[:, sl] * w_exp[:, sl]).astype(BF16)))

    y = jnp.concatenate(y_diag, axis=1) + jnp.concatenate(y_off, axis=1) * eacs_exp + dexp_ref[...] * xs
    st_ref[...] = h_prev * eacs_exp[L - 1:L, :] + jnp.concatenate(st_new, axis=1)
    y = y * jax.nn.silu(z)
    outs = []
    for g in range(SSD_GROUPS):
        yg = y[:, g * gw:(g + 1) * gw]
        outs.append(yg * lax.rsqrt(jnp.mean(yg * yg, axis=-1, keepdims=True) + EPS))
    y_ref[...] = (jnp.concatenate(outs, axis=1) * nw_ref[...]).astype(y_ref.dtype)


def _ssd(z, xa, dtp, dtb, alog, dexp, nw, tri, e_mat, bsz, seqlen, heads):
    t, inner = z.shape
    cch = xa.shape[1]
    chunks = 4
    rows = chunks * SSD_CHUNK
    nb = seqlen // rows
    row = lambda b, c: (b * nb + c, 0)
    const = lambda shape: pl.BlockSpec(shape, lambda b, c: (0, 0))
    return pl.pallas_call(
        functools.partial(_ssd_kernel, inner=inner, heads=heads, chunks=chunks),
        grid=(bsz, nb),
        in_specs=[pl.BlockSpec((rows, inner), row), pl.BlockSpec((rows, cch), row), pl.BlockSpec((rows, LANES), row),
                  const((1, LANES)), const((1, LANES)), const((1, inner)), const((1, inner)),
                  const((SSD_CHUNK, SSD_CHUNK)), const((LANES, inner))],
        out_specs=pl.BlockSpec((rows, inner), row),
        out_shape=jax.ShapeDtypeStruct((t, inner), BF16),
        scratch_shapes=[pltpu.VMEM((SSD_STATE, inner), F32)],
        compiler_params=_params("parallel", "arbitrary"),
        name="ssd",
    )(z, xa, dtp, dtb, alog, dexp, nw, tri, e_mat)


def _s5_kernel(*refs, groups, gch):
    n_t = groups * gch // LANES
    u_refs = refs[:n_t]
    toep_ref, ws_ref, wo_ref, ar_ref, ai_ref = refs[n_t:n_t + 5]
    y_refs = refs[n_t + 5:2 * n_t + 5]
    fold, yfold, sre, sim, pre, pim = refs[2 * n_t + 5:]
    lc = S5_CHUNK
    nc = u_refs[0].shape[0] // lc
    per_tile = LANES // gch
    n_tiles = groups // per_tile
    halves = lc * gch // LANES
    w2 = halves * LANES
    rb = 16
    piece = lax.broadcasted_iota(jnp.int32, (rb, LANES), 1) // gch

    def merge(srcs, shift_of):
        acc = None
        for q, src in enumerate(srcs):
            sh = shift_of(q) * gch
            r = pltpu.roll(src, sh, axis=1) if sh else src
            acc = r if acc is None else jnp.where(piece == q, r, acc)
        return acc

    def fold_rows(b, _):
        r0 = pl.multiple_of(b * rb, rb)
        for j in range(n_tiles):
            tiles = [u_refs[j][pl.ds(b * (rb * lc) + t, rb, stride=lc), :] for t in range(lc)]
            for gp in range(per_tile):
                g = j * per_tile + gp
                parts = [merge(tiles[hv * per_tile:(hv + 1) * per_tile], lambda tt: (tt - gp) % per_tile)
                         for hv in range(halves)]
                fold[g // 2, pl.ds(r0, rb), (g % 2) * w2:(g % 2 + 1) * w2] = (
                    jnp.concatenate(parts, axis=1).astype(BF16))
        return 0

    lax.fori_loop(0, nc // rb, fold_rows, 0)

    n_pairs = groups // 2
    for k in range(n_pairs):
        inc = _dot(fold[k], ws_ref[k])
        sre[:, k * LANES:(k + 1) * LANES] = inc[:, :LANES]
        sim[:, k * LANES:(k + 1) * LANES] = inc[:, LANES:]

    ar = ar_ref[...]
    ai = ai_ref[...]

    def body(c, carry):
        r, i = carry
        pre[pl.ds(c, 1), :] = r
        pim[pl.ds(c, 1), :] = i
        return ar * r - ai * i + sre[pl.ds(c, 1), :], ar * i + ai * r + sim[pl.ds(c, 1), :]

    zero = jnp.zeros((1, sre.shape[1]), F32)
    lax.fori_loop(0, nc, body, (zero, zero))

    for k in range(n_pairs):
        a = fold[k]
        intra = jnp.concatenate([_dot(a[:, :w2], toep_ref[2 * k]), _dot(a[:, w2:], toep_ref[2 * k + 1])], axis=1)
        state = jnp.concatenate([pre[:, k * LANES:(k + 1) * LANES], pim[:, k * LANES:(k + 1) * LANES]], axis=1)
        yfold[k] = intra + _dot(state.astype(BF16), wo_ref[k])

    def unfold_rows(b, _):
        r0 = pl.multiple_of(b * rb, rb)
        for j in range(n_tiles):
            for hv in range(halves):
                srcs = []
                for gp in range(per_tile):
                    g = j * per_tile + gp
                    srcs.append(yfold[g // 2, pl.ds(r0, rb), pl.ds((g % 2) * w2 + hv * LANES, LANES)])
                for tt in range(per_tile):
                    y_refs[j][pl.ds(b * (rb * lc) + hv * per_tile + tt, rb, stride=lc), :] = merge(
                        srcs, lambda gp: (gp - tt) % per_tile)
        return 0

    lax.fori_loop(0, nc // rb, unfold_rows, 0)


def _s5_core(u5, toep, ws, wo, a_re, a_im, layer, bsz, seqlen, gch):
    t, n5 = u5.shape
    groups = n5 // gch
    nc = seqlen // S5_CHUNK
    n_t = n5 // LANES
    const = lambda a: pl.BlockSpec((None,) + a.shape[1:], lambda b: (layer,) + (0,) * (a.ndim - 1),
                                   pipeline_mode=pl.Buffered(1))
    col = lambda j: pl.BlockSpec((seqlen, LANES), lambda b: (b, j))
    return pl.pallas_call(
        functools.partial(_s5_kernel, groups=groups, gch=gch),
        grid=(bsz,),
        in_specs=[col(j) for j in range(n_t)] + [const(toep), const(ws), const(wo), const(a_re), const(a_im)],
        out_specs=[pl.BlockSpec((seqlen, LANES), lambda b: (b, 0))] * n_t,
        out_shape=[jax.ShapeDtypeStruct((t, LANES), F32)] * n_t,
        scratch_shapes=[pltpu.VMEM((groups // 2, nc, 2 * S5_CHUNK * gch), BF16),
                        pltpu.VMEM((groups // 2, nc, 2 * S5_CHUNK * gch), F32)]
        + [pltpu.VMEM((nc, a_re.shape[2]), F32)] * 4,
        compiler_params=_params("parallel"),
        name="s5_core",
    )(*([u5] * n_t), toep, ws, wo, a_re, a_im)


def _s5_weights(lam_re, lam_im, log_dt, b_re, b_im, c_re, c_im):
    lc = S5_CHUNK
    nl, g, p, h = b_re.shape
    step = jnp.exp(log_dt)[..., None]
    mag = jnp.exp(lam_re * step)
    ab_re, ab_im = mag * jnp.cos(lam_im * step), mag * jnp.sin(lam_im * step)
    den = lam_re * lam_re + lam_im * lam_im
    nr = ab_re - 1.0
    coef_re = (nr * lam_re + ab_im * lam_im) / den
    coef_im = (ab_im * lam_re - nr * lam_im) / den
    swap = lambda a: a.transpose(0, 1, 3, 2)
    bt_re, bt_im = swap(b_re), swap(b_im)
    bbt_re = coef_re[:, :, None, :] * bt_re - coef_im[:, :, None, :] * bt_im
    bbt_im = coef_re[:, :, None, :] * bt_im + coef_im[:, :, None, :] * bt_re
    jj = jnp.arange(lc + 1, dtype=F32)
    pmag = jnp.exp((lam_re * step)[..., None] * jj)
    pw_re = pmag * jnp.cos((lam_im * step)[..., None] * jj)
    pw_im = pmag * jnp.sin((lam_im * step)[..., None] * jj)
    ct_re, ct_im = swap(c_re), swap(c_im)
    wide = lambda a: a.reshape(nl, g, p, (lc + 1) * h)
    cpt_re = wide(ct_re[:, :, :, None, :] * pw_re[..., None] - ct_im[:, :, :, None, :] * pw_im[..., None])
    cpt_im = wide(ct_re[:, :, :, None, :] * pw_im[..., None] + ct_im[:, :, :, None, :] * pw_re[..., None])
    w = lc * h
    k2 = (jnp.einsum('lghp,lgpc->lghc', bbt_re, cpt_re[..., :w], precision='highest')
          - jnp.einsum('lghp,lgpc->lghc', bbt_im, cpt_im[..., :w], precision='highest'))
    toep = jnp.stack([jnp.pad(k2, ((0, 0), (0, 0), (0, 0), (ti * h, 0)))[..., :w] for ti in range(lc)], axis=2)
    toep = toep.reshape(nl, g, w, w)
    rev_re = swap(jnp.flip(pw_re[..., :lc], axis=-1))
    rev_im = swap(jnp.flip(pw_im[..., :lc], axis=-1))
    ws_re = (rev_re[:, :, :, None, :] * bbt_re[:, :, None] - rev_im[:, :, :, None, :] * bbt_im[:, :, None]
             ).reshape(nl, g, w, p)
    ws_im = (rev_re[:, :, :, None, :] * bbt_im[:, :, None] + rev_im[:, :, :, None, :] * bbt_re[:, :, None]
             ).reshape(nl, g, w, p)
    wo_re, wo_im = cpt_re[..., h:], -cpt_im[..., h:]

    def pair(a):
        a = a.reshape((nl, g // 2, 2) + a.shape[2:])
        z = jnp.zeros_like(a[:, :, 0])
        return jnp.concatenate([jnp.concatenate([a[:, :, 0], z], axis=-1),
                                jnp.concatenate([z, a[:, :, 1]], axis=-1)], axis=-2).astype(BF16)

    ws = jnp.concatenate([pair(ws_re), pair(ws_im)], axis=-1)
    wo = jnp.concatenate([pair(wo_re), pair(wo_im)], axis=-2)
    return (toep.astype(BF16), ws, wo, pw_re[..., lc].reshape(nl, 1, g * p), pw_im[..., lc].reshape(nl, 1, g * p))


def _outproj_kernel(x_ref, ys_ref, *refs):
    n_t = len(refs) - 17
    y5_refs = refs[:n_t]
    (u5_ref, d5_ref, wg_ref, bg_ref, n5_ref, wa_ref, wb_ref, gate_ref, nw_ref, shift_ref, scale_ref,
     wrh_ref, wrl_ref, br_ref, x1_ref, h2_ref, lg_ref) = refs[n_t:]
    y = jnp.concatenate([r[...] for r in y5_refs], axis=1) + d5_ref[...] * u5_ref[...]
    y = jax.nn.gelu(y)
    y = y * jax.nn.sigmoid(_dot(y.astype(BF16), wg_ref[...]) + bg_ref[...])
    y = y * lax.rsqrt(jnp.mean(y * y, axis=-1, keepdims=True) + EPS) * n5_ref[...]
    m = _dot(ys_ref[...], wa_ref[...]) + _dot(y.astype(BF16), wb_ref[...])
    x1 = x_ref[...] + gate_ref[0] * m
    x1_ref[...] = x1
    h = x1 * lax.rsqrt(jnp.mean(x1 * x1, axis=-1, keepdims=True) + EPS) * nw_ref[...]
    h = h * (1.0 + scale_ref[0]) + shift_ref[0]
    h2_ref[...] = _pack_rows(h)
    h_hi, h_lo = _split2(h)
    wrh = wrh_ref[...]
    lg = _dot(h_hi, wrh) + _dot(h_lo, wrh) + _dot(h_hi, wrl_ref[...]) + br_ref[...]
    lg_ref[...] = lg.T[:ROUTER_ROWS, :]


def _outproj(x2, y_ssd, y5, u5, d5, w_glu, b_glu, n5, w_a, w_b, nw, ada3, wr_hi, wr_lo, br,
             layer, bsz, seqlen):
    t, d = x2.shape
    n_s = y_ssd.shape[1]
    n_5 = u5.shape[1]
    tm = min(512, seqlen)
    per_b = seqlen // tm
    row = lambda i: (i, 0)
    const = lambda a: pl.BlockSpec(a.shape, lambda i: (0, 0))
    ada_blk = lambda k: pl.BlockSpec((1, 1, d), lambda i: (layer * bsz + i // per_b, 0, k))
    return pl.pallas_call(
        _outproj_kernel,
        grid=(t // tm,),
        in_specs=[pl.BlockSpec((tm, d), row), pl.BlockSpec((tm, n_s), row)]
        + [pl.BlockSpec((tm, LANES), row)] * len(y5)
        + [pl.BlockSpec((tm, n_5), row), const(d5), const(w_glu), const(b_glu), const(n5),
                  const(w_a), const(w_b), ada_blk(2), const(nw), ada_blk(3), ada_blk(4),
                  const(wr_hi), const(wr_lo), const(br)],
        out_specs=[pl.BlockSpec((tm, d), row), pl.BlockSpec((tm, d // 2), row),
                   pl.BlockSpec((ROUTER_ROWS, tm), lambda i: (0, i))],
        out_shape=[jax.ShapeDtypeStruct((t, d), F32), jax.ShapeDtypeStruct((t, d // 2), jnp.uint32),
                   jax.ShapeDtypeStruct((ROUTER_ROWS, t), F32)],
        compiler_params=_params("parallel"),
        name="outproj",
    )(x2, y_ssd, *y5, u5, d5, w_glu, b_glu, n5, w_a, w_b, ada3, nw, ada3, ada3, wr_hi, wr_lo, br)


def _route_kernel(lg_ref, upper_ref, dest_ref, gate_ref, cnt_ref, counts, carry, pstart,
                  *, n_experts, per_group, block_rows, sub):
    ph = pl.program_id(0)
    i = pl.program_id(1)
    tr = lg_ref.shape[1]
    lg = lg_ref[...]

    @pl.when((ph == 0) & (i == 0))
    def _():
        counts[...] = jnp.zeros(counts.shape, F32)

    gl = [lg[n_experts + k:n_experts + k + 1, :] for k in range(N_EXPERT_GROUPS)]
    gmax = functools.reduce(jnp.maximum, gl)
    gidx = jnp.full((1, tr), N_EXPERT_GROUPS - 1, jnp.int32)
    for k in range(N_EXPERT_GROUPS - 2, -1, -1):
        gidx = jnp.where(gl[k] == gmax, k, gidx)
    gsum = functools.reduce(lambda a, b: a + b, [jnp.exp(v - gmax) for v in gl])
    g_w = 1.0 / gsum
    el = lg[0:per_group, :]
    for k in range(1, N_EXPERT_GROUPS):
        el = jnp.where(gidx == k, lg[k * per_group:(k + 1) * per_group, :], el)
    ep = jnp.exp(el - jnp.max(el, axis=0, keepdims=True))
    prob = ep / jnp.sum(ep, axis=0, keepdims=True)
    jj = lax.broadcasted_iota(jnp.int32, (per_group, tr), 0).astype(F32)
    p1 = jnp.max(prob, axis=0, keepdims=True)
    i1 = jnp.min(jnp.where(prob == p1, jj, float(per_group)), axis=0, keepdims=True)
    prob2 = jnp.where(jj == i1, -1.0, prob)
    p2 = jnp.max(prob2, axis=0, keepdims=True)
    i2 = jnp.min(jnp.where(prob2 == p2, jj, float(per_group)), axis=0, keepdims=True)
    den = p1 + p2
    gate_ref[0] = jnp.concatenate([g_w * p1 / den, g_w * p2 / den], axis=0)
    e1 = gidx * per_group + i1.astype(jnp.int32)
    e2 = gidx * per_group + i2.astype(jnp.int32)
    rr = lax.broadcasted_iota(jnp.int32, (n_experts, tr), 0)
    oh1 = rr == e1
    oh2 = rr == e2
    member = jnp.where(oh1 | oh2, 1.0, 0.0)

    @pl.when(ph == 0)
    def _():
        counts[...] = counts[...] + jnp.sum(member, axis=1, keepdims=True)
        dest_ref[...] = jnp.zeros(dest_ref.shape, jnp.int32)

    @pl.when(ph == 1)
    def _():
        @pl.when(i == 0)
        def _():
            blocks = (counts[...].astype(jnp.int32) + (block_rows - 1)) >> int(math.log2(block_rows))
            hi = (blocks >> 4).astype(F32).astype(BF16)
            lo = (blocks & 15).astype(F32).astype(BF16)
            er = lax.broadcasted_iota(jnp.int32, (n_experts, n_experts), 0)
            ec = lax.broadcasted_iota(jnp.int32, (n_experts, n_experts), 1)
            lower = jnp.where(ec < er, 1.0, 0.0).astype(BF16)
            pstart[...] = (16.0 * _dot(lower, hi) + _dot(lower, lo)) * float(block_rows)
            carry[...] = jnp.zeros(carry.shape, F32)

        run = carry[...]
        upper = upper_ref[...]
        pieces = []
        for b in range(tr // sub):
            mb = member[:, b * sub:(b + 1) * sub]
            pieces.append(_dot(mb.astype(BF16), upper) + jnp.concatenate([run] * (sub // LANES), axis=1))
            run = run + jnp.sum(mb, axis=1, keepdims=True)
        carry[...] = run
        base = jnp.concatenate(pieces, axis=1) + jnp.concatenate([pstart[...]] * (tr // LANES), axis=1)
        d1 = jnp.sum(jnp.where(oh1, base, 0.0), axis=0, keepdims=True)
        d2 = jnp.sum(jnp.where(oh2, base, 0.0), axis=0, keepdims=True)
        dest_ref[0] = jnp.concatenate([d1, d2], axis=0).astype(jnp.int32)

    cnt_ref[...] = counts[...].astype(jnp.int32)


def _route(logits_t, upper, n_experts, block_rows):
    rows, t = logits_t.shape
    tr = 1024
    sub = upper.shape[0]
    return pl.pallas_call(
        functools.partial(_route_kernel, n_experts=n_experts, per_group=n_experts // N_EXPERT_GROUPS,
                          block_rows=block_rows, sub=sub),
        grid=(2, t // tr),
        in_specs=[pl.BlockSpec((rows, tr), lambda ph, i: (0, i)),
                  pl.BlockSpec(upper.shape, lambda ph, i: (0, 0))],
        out_specs=[pl.BlockSpec((1, TOP_K, tr), lambda ph, i: (ph, 0, i)),
                   pl.BlockSpec((1, TOP_K, tr), lambda ph, i: (ph, 0, i)),
                   pl.BlockSpec((n_experts, LANES), lambda ph, i: (0, 0))],
        out_shape=[jax.ShapeDtypeStruct((2, TOP_K, t), jnp.int32), jax.ShapeDtypeStruct((2, TOP_K, t), F32),
                   jax.ShapeDtypeStruct((n_experts, LANES), jnp.int32)],
        scratch_shapes=[pltpu.VMEM((n_experts, LANES), F32)] * 3,
        compiler_params=_params("arbitrary", "arbitrary"),
        name="route",
    )(logits_t, upper)


DMA_UNROLL = 8


def _dispatch_kernel(dest_ref, h_ref, init_ref, rows_ref, sem, *, tq, t):
    del init_ref
    base = pl.program_id(0) * tq

    def issue(jb, _):
        for u in range(DMA_UNROLL):
            for k in range(TOP_K):
                pltpu.make_async_copy(h_ref.at[jb, pl.ds(u, 1)],
                                      rows_ref.at[pl.ds(dest_ref[k * t + base + jb * DMA_UNROLL + u], 1)],
                                      sem).start(priority=k % 2)
        return 0

    def drain(jb, _):
        for k in range(TOP_K):
            pltpu.make_async_copy(h_ref.at[jb], rows_ref.at[pl.ds(0, DMA_UNROLL)], sem).wait()
        return 0

    lax.fori_loop(0, tq // DMA_UNROLL, issue, 0)
    lax.fori_loop(0, tq // DMA_UNROLL, drain, 0)


def _dispatch(dest_flat, h2, n_pad):
    t, d = h2.shape
    tq = min(1024, t)
    init = jnp.zeros((n_pad, d), h2.dtype)
    return pl.pallas_call(
        functools.partial(_dispatch_kernel, tq=tq, t=t),
        grid_spec=pltpu.PrefetchScalarGridSpec(
            num_scalar_prefetch=1, grid=(t // tq,),
            in_specs=[pl.BlockSpec((tq // DMA_UNROLL, DMA_UNROLL, d), lambda i, dr: (i, 0, 0)),
                      pl.BlockSpec(memory_space=pl.ANY)],
            out_specs=pl.BlockSpec(memory_space=pl.ANY),
            scratch_shapes=[pltpu.SemaphoreType.DMA]),
        out_shape=jax.ShapeDtypeStruct((n_pad, d), h2.dtype),
        input_output_aliases={2: 0},
        compiler_params=_params("arbitrary"),
        name="dispatch",
    )(dest_flat, h2.reshape(t // DMA_UNROLL, DMA_UNROLL, d), init)


def _ffn_kernel(be_ref, first_ref, slot_ref, nxt_ref, hasnext_ref, nused_ref, x_ref, wg_hbm, wu_hbm, wd_hbm,
                y_ref, sg, su, sd, wg16, wu16, wd16, sems, *, layer):
    i = pl.program_id(0)

    def weight_copies(e, s):
        return (pltpu.make_async_copy(wg_hbm.at[layer, e], sg.at[s], sems.at[s, 0]),
                pltpu.make_async_copy(wu_hbm.at[layer, e], su.at[s], sems.at[s, 1]),
                pltpu.make_async_copy(wd_hbm.at[layer, e], sd.at[s], sems.at[s, 2]))

    @pl.when(first_ref[i] == 1)
    def _():
        s = slot_ref[i]
        e = be_ref[i]

        @pl.when(i == 0)
        def _():
            for cp in weight_copies(e, s):
                cp.start()

        for cp in weight_copies(e, s):
            cp.wait()
        wg16[...] = sg[s].astype(BF16)
        wu16[...] = su[s].astype(BF16)
        wd16[...] = sd[s].astype(BF16)

        @pl.when(hasnext_ref[i] == 1)
        def _():
            for cp in weight_copies(nxt_ref[i], 1 - s):
                cp.start()

    @pl.when(i < nused_ref[0])
    def _():
        xb = _unpack_rows(x_ref[...]).astype(BF16)
        a = jax.nn.silu(_dot(xb, wg16[...])) * _dot(xb, wu16[...])
        y_ref[...] = _pack_rows(_dot(a.astype(BF16), wd16[...]))

    @pl.when(i >= nused_ref[0])
    def _():
        y_ref[...] = jnp.zeros(y_ref.shape, y_ref.dtype)


def _ffn(counts, rows, w_eg, w_eu, w_ed, layer, bm):
    n_pad, dw = rows.shape
    _, n_experts, d, ff = w_eg.shape
    n_blocks = n_pad // bm
    ends = jnp.cumsum((counts + bm - 1) // bm).astype(jnp.int32)
    n_used = ends[-1]
    bidx = jnp.arange(n_blocks, dtype=jnp.int32)
    be = jnp.minimum(jnp.sum(ends[None, :] <= bidx[:, None], axis=1), n_experts - 1).astype(jnp.int32)
    be = jnp.where(bidx < n_used, be, be[n_used - 1])
    first = jnp.concatenate([jnp.ones((1,), jnp.int32), (be[1:] != be[:-1]).astype(jnp.int32)])
    slot = ((jnp.cumsum(first) - 1) & 1).astype(jnp.int32)
    seg_end = ends[be]
    has_next = (seg_end < n_used).astype(jnp.int32)
    nxt = be[jnp.minimum(seg_end, n_blocks - 1)]
    row = lambda i, *_: (i, 0)
    return pl.pallas_call(
        functools.partial(_ffn_kernel, layer=layer),
        grid_spec=pltpu.PrefetchScalarGridSpec(
            num_scalar_prefetch=6, grid=(n_blocks,),
            in_specs=[pl.BlockSpec((bm, dw), row)] + [pl.BlockSpec(memory_space=pl.ANY)] * 3,
            out_specs=pl.BlockSpec((bm, dw), row),
            scratch_shapes=[pltpu.VMEM((2, d, ff), F32), pltpu.VMEM((2, d, ff), F32), pltpu.VMEM((2, ff, d), F32),
                            pltpu.VMEM((d, ff), BF16), pltpu.VMEM((d, ff), BF16), pltpu.VMEM((ff, d), BF16),
                            pltpu.SemaphoreType.DMA((2, 3))]),
        out_shape=jax.ShapeDtypeStruct((n_pad, dw), jnp.uint32),
        compiler_params=_params("arbitrary"),
        name="ffn",
    )(be, first, slot, nxt, has_next, n_used[None], rows, w_eg, w_eu, w_ed)


def _combine_kernel(dest_ref, yrows_ref, x_ref, g_ref, gate_ref, fw_ref, o_ref, buf, sems, *, tq, t, final):
    i = pl.program_id(0)
    n = pl.num_programs(0)

    def issue(tile):
        slot = tile % 2

        def body(jb, _):
            for u in range(DMA_UNROLL):
                for k in range(TOP_K):
                    pltpu.make_async_copy(
                        yrows_ref.at[pl.ds(dest_ref[k * t + tile * tq + jb * DMA_UNROLL + u], 1)],
                        buf.at[slot, k, jb, pl.ds(u, 1)], sems.at[slot]).start(priority=k % 2)
            return 0
        lax.fori_loop(0, tq // DMA_UNROLL, body, 0)

    @pl.when(i == 0)
    def _():
        issue(i)

    @pl.when(i + 1 < n)
    def _():
        issue(i + 1)

    def drain(jb, _):
        for k in range(TOP_K):
            pltpu.make_async_copy(yrows_ref.at[pl.ds(0, DMA_UNROLL)], buf.at[i % 2, k, jb], sems.at[i % 2]).wait()
        return 0

    lax.fori_loop(0, tq // DMA_UNROLL, drain, 0)
    g = g_ref[...]
    rows = buf[i % 2].reshape(TOP_K, tq, buf.shape[-1])
    f = g[:, 0:1] * _unpack_rows(rows[0]) + g[:, 1:2] * _unpack_rows(rows[1])
    x2 = x_ref[...] + gate_ref[0] * f
    if final:
        x2 = x2 * lax.rsqrt(jnp.mean(x2 * x2, axis=-1, keepdims=True) + EPS) * fw_ref[...]
    o_ref[...] = x2


def _combine(dest_flat, y_rows, x1, gates_t, ada3, fw, layer, bsz, seqlen, final):
    t, d = x1.shape
    tq = min(1024, seqlen)
    per_b = seqlen // tq
    return pl.pallas_call(
        functools.partial(_combine_kernel, tq=tq, t=t, final=final),
        grid_spec=pltpu.PrefetchScalarGridSpec(
            num_scalar_prefetch=1, grid=(t // tq,),
            in_specs=[pl.BlockSpec(memory_space=pl.ANY),
                      pl.BlockSpec((tq, d), lambda i, dr: (i, 0)),
                      pl.BlockSpec((tq, TOP_K), lambda i, dr: (i, 0)),
                      pl.BlockSpec((1, 1, d), lambda i, dr: (layer * bsz + i // per_b, 0, 5)),
                      pl.BlockSpec((1, d), lambda i, dr: (0, 0))],
            out_specs=pl.BlockSpec((tq, d), lambda i, dr: (i, 0)),
            scratch_shapes=[pltpu.VMEM((2, TOP_K, tq // DMA_UNROLL, DMA_UNROLL, d // 2), jnp.uint32),
                            pltpu.SemaphoreType.DMA((2,))]),
        out_shape=jax.ShapeDtypeStruct((t, d), F32),
        compiler_params=_params("arbitrary"),
        name="combine",
    )(dest_flat, y_rows, x1, gates_t, ada3, fw)


def kernel(x, c, w_ada, b_ada, norm1_w, w_in, conv_w, conv_b, dt_bias, a_log, d_ssd, ssd_norm_w,
           s5_lam_re, s5_lam_im, s5_log_dt, s5_b_re, s5_b_im, s5_c_re, s5_c_im, s5_d, w_glu, b_glu,
           s5_norm_w, w_out, norm2_w, w_rg, b_rg, w_re, b_re, w_eg, w_eu, w_ed, final_norm_w):
    bsz, seqlen, d = x.shape
    t = bsz * seqlen
    depth = w_in.shape[0]
    heads = dt_bias.shape[1]
    inner = ssd_norm_w.shape[1]
    cch = conv_w.shape[2]
    n5 = s5_d.shape[1]
    s5_groups, _, s5_h = s5_b_re.shape[1:]
    n_experts = w_re.shape[2]
    moe_block = 512
    n_rows = t * TOP_K
    n_pad = n_rows + n_experts * moe_block
    n_blocks = n_pad // moe_block

    i0, i1, i2 = inner, inner + cch, inner + cch + heads
    w_zx = w_in[..., :i1].astype(BF16)
    n_du = w_in.shape[2] - i1
    w_du = jnp.pad(w_in[..., i1:].astype(BF16), ((0, 0), (0, 0), (0, -n_du % LANES)))
    lane_pad = lambda a: jnp.pad(a, ((0, 0), (0, LANES - a.shape[1])))[:, None, :]
    dtb, alog = lane_pad(dt_bias), lane_pad(a_log)
    dexp = jnp.repeat(d_ssd, inner // heads, axis=1)[:, None, :]
    li = jnp.arange(SSD_CHUNK)
    tri = (li[None, :] <= li[:, None]).astype(BF16)
    e_mat = (jnp.arange(LANES)[:, None] == (jnp.arange(inner) // (inner // heads))[None, :]).astype(BF16)
    w_out_a = w_out[:, :inner].astype(BF16)
    w_out_b = w_out[:, inner:].astype(BF16)
    w_glu_b = w_glu.astype(BF16)
    wr = jnp.concatenate([w_re, w_rg, jnp.zeros((depth, d, LANES - n_experts - N_EXPERT_GROUPS), F32)], axis=-1)
    wr_hi = wr.astype(BF16)
    wr_lo = (wr - wr_hi.astype(F32)).astype(BF16)
    br = jnp.concatenate([b_re, b_rg, jnp.zeros((depth, LANES - n_experts - N_EXPERT_GROUPS), F32)], axis=-1)[:, None, :]
    sub = 256
    si = jnp.arange(sub)
    upper = (si[:, None] < si[None, :]).astype(BF16)

    ada3 = _ada(c, w_ada, b_ada).reshape(depth * bsz, 1, 6 * d)
    x2 = x.reshape(t, d)
    s5w = _s5_weights(s5_lam_re, s5_lam_im, s5_log_dt, s5_b_re, s5_b_im, s5_c_re, s5_c_im)
    for l in range(depth):
        z, xa, u5, dtp = _inproj(x2, norm1_w[l][None], ada3, w_zx[l], w_du[l], conv_w[l], conv_b[l][None], n5, l,
                                 bsz, seqlen, inner, heads)
        y_ssd = _ssd(z, xa, dtp, dtb[l], alog[l], dexp[l], ssd_norm_w[l][None], tri, e_mat, bsz, seqlen, heads)
        y5 = _s5_core(u5, *s5w, l, bsz, seqlen, s5_h)
        x1, h2, logits_t = _outproj(x2, y_ssd, y5, u5, s5_d[l][None], w_glu_b[l], b_glu[l][None],
                                    s5_norm_w[l][None], w_out_a[l], w_out_b[l], norm2_w[l][None], ada3,
                                    wr_hi[l], wr_lo[l], br[l], l, bsz, seqlen)
        dest, gates, counts = _route(logits_t, upper, n_experts, moe_block)
        dest_flat = dest[1].reshape(-1)
        rows = _dispatch(dest_flat, h2, n_pad)
        y_rows = _ffn(counts[:, 0], rows, w_eg, w_eu, w_ed, l, moe_block)
        x2 = _combine(dest_flat, y_rows, x1, gates[1].T, ada3, final_norm_w[None], l, bsz, seqlen, l == depth - 1)
    return x2.reshape(bsz, seqlen, d)
```

```python
import functools
import math

import jax
import jax.numpy as jnp
from jax import lax
from jax.experimental import pallas as pl
from jax.experimental.pallas import tpu as pltpu

F32 = jnp.float32
BF16 = jnp.bfloat16
EPS = 1e-6

SSD_GROUPS = 2
SSD_STATE = 128
SSD_CHUNK = 128
S5_CHUNK = 16
N_EXPERT_GROUPS = 4
TOP_K = 2
LANES = 128
ROUTER_ROWS = 40
VMEM_LIMIT = 48 * 1024 * 1024


def _dot(a, b):
    return jnp.dot(a, b, preferred_element_type=F32)


def _split2(a):
    hi = a.astype(BF16)
    lo = (a - hi.astype(F32)).astype(BF16)
    return hi, lo


def _split3(a):
    hi = a.astype(BF16)
    r = a - hi.astype(F32)
    mid = r.astype(BF16)
    lo = (r - mid.astype(F32)).astype(BF16)
    return hi, mid, lo


def _pack_rows(x):
    w = x.shape[1] // 2
    bits = lambda v: lax.bitcast_convert_type(v.astype(BF16).astype(F32), jnp.uint32)
    return (bits(x[:, :w]) >> 16) | bits(x[:, w:])


def _unpack_rows(words):
    lo = lax.bitcast_convert_type(words << 16, F32)
    hi = lax.bitcast_convert_type(words & jnp.uint32(0xFFFF0000), F32)
    return jnp.concatenate([lo, hi], axis=1)


def _params(*sem):
    return pltpu.CompilerParams(dimension_semantics=sem, vmem_limit_bytes=VMEM_LIMIT)


def _ada_kernel(c_ref, w_ref, b_ref, o_ref):
    ca = jax.nn.silu(c_ref[...])
    c_hi, c_lo = _split2(ca)
    w_hi, w_lo = _split2(w_ref[0])
    o_ref[0] = _dot(c_hi, w_hi) + _dot(c_lo, w_hi) + _dot(c_hi, w_lo) + b_ref[0]


def _ada(c, w_ada, b_ada):
    n_layers, d, n_out = w_ada.shape
    bsz = c.shape[0]
    tn = 1536
    return pl.pallas_call(
        _ada_kernel,
        grid=(n_layers, n_out // tn),
        in_specs=[pl.BlockSpec((bsz, d), lambda l, j: (0, 0)),
                  pl.BlockSpec((1, d, tn), lambda l, j: (l, 0, j)),
                  pl.BlockSpec((1, 1, tn), lambda l, j: (l, 0, j))],
        out_specs=pl.BlockSpec((1, bsz, tn), lambda l, j: (l, 0, j)),
        out_shape=jax.ShapeDtypeStruct((n_layers, bsz, n_out), F32),
        compiler_params=_params("parallel", "parallel"),
        name="ada",
    )(c, w_ada, b_ada.reshape(n_layers, 1, n_out))


def _inproj_kernel(x_ref, nw_ref, shift_ref, scale_ref, wzx_ref, wdu_ref, cw_ref, cb_ref,
                   z_ref, xa_ref, u_ref, dt_ref, tail_ref, *, n_z, n_dt, per_b):
    tm = x_ref.shape[0]
    tail = tail_ref.shape[0]
    n_conv = cw_ref.shape[0]

    @pl.when(pl.program_id(0) % per_b == 0)
    def _():
        tail_ref[...] = jnp.zeros(tail_ref.shape, F32)

    x = x_ref[...]
    h = x * lax.rsqrt(jnp.mean(x * x, axis=-1, keepdims=True) + EPS) * nw_ref[...]
    h = (h * (1.0 + scale_ref[0]) + shift_ref[0]).astype(BF16)
    u_new = _dot(h, wzx_ref[:, n_z:])
    z_ref[...] = _dot(h, wzx_ref[:, :n_z])
    ext = jnp.concatenate([tail_ref[...], u_new], axis=0)
    conv = cb_ref[...] + cw_ref[n_conv - 1:n_conv, :] * u_new
    for k in range(n_conv - 1):
        conv = conv + cw_ref[k:k + 1, :] * pltpu.roll(ext, n_conv - 1 - k, axis=0)[tail:, :]
    tail_ref[...] = u_new[tm - tail:, :]
    xa_ref[...] = jax.nn.silu(conv)
    q = _dot(h, wdu_ref[...])
    dt_ref[...] = q[:, :LANES]
    u_ref[...] = q[:, n_dt:n_dt + u_ref.shape[1]]


def _inproj(x2, nw, ada3, w_zx, w_du, conv_w, conv_b, n_u, layer, bsz, seqlen, n_z, n_dt):
    t, d = x2.shape
    n_xbc = w_zx.shape[1] - n_z
    tm = min(512, seqlen)
    per_b = seqlen // tm
    row = lambda i: (i, 0)
    const = lambda a: pl.BlockSpec(a.shape, lambda i: (0, 0))
    ada_blk = lambda k: pl.BlockSpec((1, 1, d), lambda i: (layer * bsz + i // per_b, 0, k))
    return pl.pallas_call(
        functools.partial(_inproj_kernel, n_z=n_z, n_dt=n_dt, per_b=per_b),
        grid=(t // tm,),
        in_specs=[pl.BlockSpec((tm, d), row),
                  pl.BlockSpec((1, d), lambda i: (0, 0)),
                  ada_blk(0), ada_blk(1), const(w_zx), const(w_du), const(conv_w), const(conv_b)],
        out_specs=[pl.BlockSpec((tm, n_z), row), pl.BlockSpec((tm, n_xbc), row),
                   pl.BlockSpec((tm, n_u), row), pl.BlockSpec((tm, LANES), row)],
        out_shape=[jax.ShapeDtypeStruct((t, n_z), F32), jax.ShapeDtypeStruct((t, n_xbc), F32),
                   jax.ShapeDtypeStruct((t, n_u), F32), jax.ShapeDtypeStruct((t, LANES), F32)],
        scratch_shapes=[pltpu.VMEM((8, n_xbc), F32)],
        compiler_params=_params("arbitrary"),
        name="inproj",
    )(x2, nw, ada3, ada3, w_zx, w_du, conv_w, conv_b)


def _ssd_kernel(z_ref, xa_ref, dt_ref, dtb_ref, alog_ref, dexp_ref, nw_ref,
                tri_ref, e_ref, y_ref, st_ref, *, inner, heads, chunks):
    @pl.when(pl.program_id(1) == 0)
    def _():
        st_ref[...] = jnp.zeros(st_ref.shape, F32)

    for ci in range(chunks):
        rows = pl.ds(ci * SSD_CHUNK, SSD_CHUNK)
        _ssd_chunk(z_ref[rows, :], xa_ref[rows, :], dt_ref[rows, :], dtb_ref, alog_ref, dexp_ref, nw_ref,
                   tri_ref, e_ref, y_ref.at[rows, :], st_ref, inner=inner, heads=heads)


def _ssd_chunk(z, xa, dt_raw, dtb_ref, alog_ref, dexp_ref, nw_ref, tri_ref, e_ref, y_ref, st_ref, *, inner, heads):
    L = SSD_CHUNK
    N = SSD_STATE
    hd = inner // heads
    hpg = heads // SSD_GROUPS
    gw = inner // SSD_GROUPS
    xs = xa[:, :inner]
    bm = xa[:, inner:inner + SSD_GROUPS * N]
    cm = xa[:, inner + SSD_GROUPS * N:]

    pre = dt_raw + dtb_ref[...]
    dt = jnp.maximum(pre, 0.0) + jnp.log1p(jnp.exp(-jnp.abs(pre)))
    ad = dt * (-jnp.exp(alog_ref[...]))
    tri = tri_ref[...]
    a1, a2, a3 = _split3(ad)
    acs = _dot(tri, a1) + _dot(tri, a2) + _dot(tri, a3)
    acs_t = acs.T
    dt_t = dt.T
    w = dt * jnp.exp(acs[L - 1:L, :] - acs)
    eacs = jnp.exp(acs)
    e_mat = e_ref[...]
    w1, w2 = _split2(w)
    w_exp = _dot(w1, e_mat) + _dot(w2, e_mat)
    q1, q2 = _split2(eacs)
    eacs_exp = _dot(q1, e_mat) + _dot(q2, e_mat)

    row = lax.broadcasted_iota(jnp.int32, (L, L), 0)
    col = lax.broadcasted_iota(jnp.int32, (L, L), 1)
    causal = row >= col
    lane = lax.broadcasted_iota(jnp.int32, (L, 2 * hd), 1)
    h_prev = st_ref[...]

    y_diag, y_off, st_new = [], [], []
    for g in range(SSD_GROUPS):
        bg = bm[:, g * N:(g + 1) * N]
        cg = cm[:, g * N:(g + 1) * N].astype(BF16)
        cb = lax.dot_general(cg, bg.astype(BF16), (((1,), (1,)), ((), ())), preferred_element_type=F32)
        for j in range(hpg // 2):
            h0 = g * hpg + 2 * j
            lms = []
            for h in (h0, h0 + 1):
                seg = jnp.where(causal, acs[:, h:h + 1] - acs_t[h:h + 1, :], -1e30)
                lms.append((cb * jnp.exp(seg) * dt_t[h:h + 1, :]).astype(BF16))
            pair = xs[:, h0 * hd:(h0 + 2) * hd]
            w_bd = jnp.concatenate([jnp.where(lane < hd, pair, 0.0), jnp.where(lane >= hd, pair, 0.0)],
                                   axis=0).astype(BF16)
            y_diag.append(_dot(jnp.concatenate(lms, axis=1), w_bd))
        sl = slice(g * gw, (g + 1) * gw)
        y_off.append(_dot(cg, h_prev[:, sl].astype(BF16)))
        st_new.append(_dot(bg.T.astype(BF16), (xs[:, sl] * w_exp[:, sl]).astype(BF16)))

    y = jnp.concatenate(y_diag, axis=1) + jnp.concatenate(y_off, axis=1) * eacs_exp + dexp_ref[...] * xs
    st_ref[...] = h_prev * eacs_exp[L - 1:L, :] + jnp.concatenate(st_new, axis=1)
    y = y * jax.nn.silu(z)
    outs = []
    for g in range(SSD_GROUPS):
        yg = y[:, g * gw:(g + 1) * gw]
        outs.append(yg * lax.rsqrt(jnp.mean(yg * yg, axis=-1, keepdims=True) + EPS))
    y_ref[...] = (jnp.concatenate(outs, axis=1) * nw_ref[...]).astype(y_ref.dtype)


def _ssd(z, xa, dtp, dtb, alog, dexp, nw, tri, e_mat, bsz, seqlen, heads):
    t, inner = z.shape
    cch = xa.shape[1]
    chunks = 4
    rows = chunks * SSD_CHUNK
    nb = seqlen // rows
    row = lambda b, c: (b * nb + c, 0)
    const = lambda shape: pl.BlockSpec(shape, lambda b, c: (0, 0))
    return pl.pallas_call(
        functools.partial(_ssd_kernel, inner=inner, heads=heads, chunks=chunks),
        grid=(bsz, nb),
        in_specs=[pl.BlockSpec((rows, inner), row), pl.BlockSpec((rows, cch), row), pl.BlockSpec((rows, LANES), row),
                  const((1, LANES)), const((1, LANES)), const((1, inner)), const((1, inner)),
                  const((SSD_CHUNK, SSD_CHUNK)), const((LANES, inner))],
        out_specs=pl.BlockSpec((rows, inner), row),
        out_shape=jax.ShapeDtypeStruct((t, inner), BF16),
        scratch_shapes=[pltpu.VMEM((SSD_STATE, inner), F32)],
        compiler_params=_params("parallel", "arbitrary"),
        name="ssd",
    )(z, xa, dtp, dtb, alog, dexp, nw, tri, e_mat)


def _s5_kernel(*refs, groups, gch):
    n_t = groups * gch // LANES
    u_refs = refs[:n_t]
    toep_ref, ws_ref, wo_ref, ar_ref, ai_ref = refs[n_t:n_t + 5]
    y_refs = refs[n_t + 5:2 * n_t + 5]
    fold, yfold, sre, sim, pre, pim = refs[2 * n_t + 5:]
    lc = S5_CHUNK
    nc = u_refs[0].shape[0] // lc
    per_tile = LANES // gch
    n_tiles = groups // per_tile
    halves = lc * gch // LANES
    w2 = halves * LANES
    rb = 16
    piece = lax.broadcasted_iota(jnp.int32, (rb, LANES), 1) // gch

    def merge(srcs, shift_of):
        acc = None
        for q, src in enumerate(srcs):
            sh = shift_of(q) * gch
            r = pltpu.roll(src, sh, axis=1) if sh else src
            acc = r if acc is None else jnp.where(piece == q, r, acc)
        return acc

    def fold_rows(b, _):
        r0 = pl.multiple_of(b * rb, rb)
        for j in range(n_tiles):
            tiles = [u_refs[j][pl.ds(b * (rb * lc) + t, rb, stride=lc), :] for t in range(lc)]
            for gp in range(per_tile):
                g = j * per_tile + gp
                parts = [merge(tiles[hv * per_tile:(hv + 1) * per_tile], lambda tt: (tt - gp) % per_tile)
                         for hv in range(halves)]
                fold[g // 2, pl.ds(r0, rb), (g % 2) * w2:(g % 2 + 1) * w2] = (
                    jnp.concatenate(parts, axis=1).astype(BF16))
        return 0

    lax.fori_loop(0, nc // rb, fold_rows, 0)

    n_pairs = groups // 2
    for k in range(n_pairs):
        inc = _dot(fold[k], ws_ref[k])
        sre[:, k * LANES:(k + 1) * LANES] = inc[:, :LANES]
        sim[:, k * LANES:(k + 1) * LANES] = inc[:, LANES:]

    ar = ar_ref[...]
    ai = ai_ref[...]

    def body(c, carry):
        r, i = carry
        pre[pl.ds(c, 1), :] = r
        pim[pl.ds(c, 1), :] = i
        return ar * r - ai * i + sre[pl.ds(c, 1), :], ar * i + ai * r + sim[pl.ds(c, 1), :]

    zero = jnp.zeros((1, sre.shape[1]), F32)
    lax.fori_loop(0, nc, body, (zero, zero))

    for k in range(n_pairs):
        a = fold[k]
        intra = jnp.concatenate([_dot(a[:, :w2], toep_ref[2 * k]), _dot(a[:, w2:], toep_ref[2 * k + 1])], axis=1)
        state = jnp.concatenate([pre[:, k * LANES:(k + 1) * LANES], pim[:, k * LANES:(k + 1) * LANES]], axis=1)
        yfold[k] = intra + _dot(state.astype(BF16), wo_ref[k])

    def unfold_rows(b, _):
        r0 = pl.multiple_of(b * rb, rb)
        for j in range(n_tiles):
            for hv in range(halves):
                srcs = []
                for gp in range(per_tile):
                    g = j * per_tile + gp
                    srcs.append(yfold[g // 2, pl.ds(r0, rb), pl.ds((g % 2) * w2 + hv * LANES, LANES)])
                for tt in range(per_tile):
                    y_refs[j][pl.ds(b * (rb * lc) + hv * per_tile + tt, rb, stride=lc), :] = merge(
                        srcs, lambda gp: (gp - tt) % per_tile)
        return 0

    lax.fori_loop(0, nc // rb, unfold_rows, 0)


def _s5_core(u5, toep, ws, wo, a_re, a_im, layer, bsz, seqlen, gch):
    t, n5 = u5.shape
    groups = n5 // gch
    nc = seqlen // S5_CHUNK
    n_t = n5 // LANES
    const = lambda a: pl.BlockSpec((None,) + a.shape[1:], lambda b: (layer,) + (0,) * (a.ndim - 1),
                                   pipeline_mode=pl.Buffered(1))
    col = lambda j: pl.BlockSpec((seqlen, LANES), lambda b: (b, j))
    return pl.pallas_call(
        functools.partial(_s5_kernel, groups=groups, gch=gch),
        grid=(bsz,),
        in_specs=[col(j) for j in range(n_t)] + [const(toep), const(ws), const(wo), const(a_re), const(a_im)],
        out_specs=[pl.BlockSpec((seqlen, LANES), lambda b: (b, 0))] * n_t,
        out_shape=[jax.ShapeDtypeStruct((t, LANES), F32)] * n_t,
        scratch_shapes=[pltpu.VMEM((groups // 2, nc, 2 * S5_CHUNK * gch), BF16),
                        pltpu.VMEM((groups // 2, nc, 2 * S5_CHUNK * gch), F32)]
        + [pltpu.VMEM((nc, a_re.shape[2]), F32)] * 4,
        compiler_params=_params("parallel"),
        name="s5_core",
    )(*([u5] * n_t), toep, ws, wo, a_re, a_im)


def _s5_weights(lam_re, lam_im, log_dt, b_re, b_im, c_re, c_im):
    lc = S5_CHUNK
    nl, g, p, h = b_re.shape
    step = jnp.exp(log_dt)[..., None]
    mag = jnp.exp(lam_re * step)
    ab_re, ab_im = mag * jnp.cos(lam_im * step), mag * jnp.sin(lam_im * step)
    den = lam_re * lam_re + lam_im * lam_im
    nr = ab_re - 1.0
    coef_re = (nr * lam_re + ab_im * lam_im) / den
    coef_im = (ab_im * lam_re - nr * lam_im) / den
    swap = lambda a: a.transpose(0, 1, 3, 2)
    bt_re, bt_im = swap(b_re), swap(b_im)
    bbt_re = coef_re[:, :, None, :] * bt_re - coef_im[:, :, None, :] * bt_im
    bbt_im = coef_re[:, :, None, :] * bt_im + coef_im[:, :, None, :] * bt_re
    jj = jnp.arange(lc + 1, dtype=F32)
    pmag = jnp.exp((lam_re * step)[..., None] * jj)
    pw_re = pmag * jnp.cos((lam_im * step)[..., None] * jj)
    pw_im = pmag * jnp.sin((lam_im * step)[..., None] * jj)
    ct_re, ct_im = swap(c_re), swap(c_im)
    wide = lambda a: a.reshape(nl, g, p, (lc + 1) * h)
    cpt_re = wide(ct_re[:, :, :, None, :] * pw_re[..., None] - ct_im[:, :, :, None, :] * pw_im[..., None])
    cpt_im = wide(ct_re[:, :, :, None, :] * pw_im[..., None] + ct_im[:, :, :, None, :] * pw_re[..., None])
    w = lc * h
    k2 = (jnp.einsum('lghp,lgpc->lghc', bbt_re, cpt_re[..., :w], precision='highest')
          - jnp.einsum('lghp,lgpc->lghc', bbt_im, cpt_im[..., :w], precision='highest'))
    toep = jnp.stack([jnp.pad(k2, ((0, 0), (0, 0), (0, 0), (ti * h, 0)))[..., :w] for ti in range(lc)], axis=2)
    toep = toep.reshape(nl, g, w, w)
    rev_re = swap(jnp.flip(pw_re[..., :lc], axis=-1))
    rev_im = swap(jnp.flip(pw_im[..., :lc], axis=-1))
    ws_re = (rev_re[:, :, :, None, :] * bbt_re[:, :, None] - rev_im[:, :, :, None, :] * bbt_im[:, :, None]
             ).reshape(nl, g, w, p)
    ws_im = (rev_re[:, :, :, None, :] * bbt_im[:, :, None] + rev_im[:, :, :, None, :] * bbt_re[:, :, None]
             ).reshape(nl, g, w, p)
    wo_re, wo_im = cpt_re[..., h:], -cpt_im[..., h:]

    def pair(a):
        a = a.reshape((nl, g // 2, 2) + a.shape[2:])
        z = jnp.zeros_like(a[:, :, 0])
        return jnp.concatenate([jnp.concatenate([a[:, :, 0], z], axis=-1),
                                jnp.concatenate([z, a[:, :, 1]], axis=-1)], axis=-2).astype(BF16)

    ws = jnp.concatenate([pair(ws_re), pair(ws_im)], axis=-1)
    wo = jnp.concatenate([pair(wo_re), pair(wo_im)], axis=-2)
    return (toep.astype(BF16), ws, wo, pw_re[..., lc].reshape(nl, 1, g * p), pw_im[..., lc].reshape(nl, 1, g * p))


def _outproj_kernel(x_ref, ys_ref, *refs):
    n_t = len(refs) - 17
    y5_refs = refs[:n_t]
    (u5_ref, d5_ref, wg_ref, bg_ref, n5_ref, wa_ref, wb_ref, gate_ref, nw_ref, shift_ref, scale_ref,
     wrh_ref, wrl_ref, br_ref, x1_ref, h2_ref, lg_ref) = refs[n_t:]
    y = jnp.concatenate([r[...] for r in y5_refs], axis=1) + d5_ref[...] * u5_ref[...]
    y = jax.nn.gelu(y)
    y = y * jax.nn.sigmoid(_dot(y.astype(BF16), wg_ref[...]) + bg_ref[...])
    y = y * lax.rsqrt(jnp.mean(y * y, axis=-1, keepdims=True) + EPS) * n5_ref[...]
    m = _dot(ys_ref[...], wa_ref[...]) + _dot(y.astype(BF16), wb_ref[...])
    x1 = x_ref[...] + gate_ref[0] * m
    x1_ref[...] = x1
    h = x1 * lax.rsqrt(jnp.mean(x1 * x1, axis=-1, keepdims=True) + EPS) * nw_ref[...]
    h = h * (1.0 + scale_ref[0]) + shift_ref[0]
    h2_ref[...] = _pack_rows(h)
    h_hi, h_lo = _split2(h)
    wrh = wrh_ref[...]
    lg = _dot(h_hi, wrh) + _dot(h_lo, wrh) + _dot(h_hi, wrl_ref[...]) + br_ref[...]
    lg_ref[...] = lg.T[:ROUTER_ROWS, :]


def _outproj(x2, y_ssd, y5, u5, d5, w_glu, b_glu, n5, w_a, w_b, nw, ada3, wr_hi, wr_lo, br,
             layer, bsz, seqlen):
    t, d = x2.shape
    n_s = y_ssd.shape[1]
    n_5 = u5.shape[1]
    tm = min(512, seqlen)
    per_b = seqlen // tm
    row = lambda i: (i, 0)
    const = lambda a: pl.BlockSpec(a.shape, lambda i: (0, 0))
    ada_blk = lambda k: pl.BlockSpec((1, 1, d), lambda i: (layer * bsz + i // per_b, 0, k))
    return pl.pallas_call(
        _outproj_kernel,
        grid=(t // tm,),
        in_specs=[pl.BlockSpec((tm, d), row), pl.BlockSpec((tm, n_s), row)]
        + [pl.BlockSpec((tm, LANES), row)] * len(y5)
        + [pl.BlockSpec((tm, n_5), row), const(d5), const(w_glu), const(b_glu), const(n5),
                  const(w_a), const(w_b), ada_blk(2), const(nw), ada_blk(3), ada_blk(4),
                  const(wr_hi), const(wr_lo), const(br)],
        out_specs=[pl.BlockSpec((tm, d), row), pl.BlockSpec((tm, d // 2), row),
                   pl.BlockSpec((ROUTER_ROWS, tm), lambda i: (0, i))],
        out_shape=[jax.ShapeDtypeStruct((t, d), F32), jax.ShapeDtypeStruct((t, d // 2), jnp.uint32),
                   jax.ShapeDtypeStruct((ROUTER_ROWS, t), F32)],
        compiler_params=_params("parallel"),
        name="outproj",
    )(x2, y_ssd, *y5, u5, d5, w_glu, b_glu, n5, w_a, w_b, ada3, nw, ada3, ada3, wr_hi, wr_lo, br)


def _route_kernel(lg_ref, upper_ref, dest_ref, gate_ref, cnt_ref, counts, carry, pstart,
                  *, n_experts, per_group, block_rows, sub):
    ph = pl.program_id(0)
    i = pl.program_id(1)
    tr = lg_ref.shape[1]
    lg = lg_ref[...]

    @pl.when((ph == 0) & (i == 0))
    def _():
        counts[...] = jnp.zeros(counts.shape, F32)

    gl = [lg[n_experts + k:n_experts + k + 1, :] for k in range(N_EXPERT_GROUPS)]
    gmax = functools.reduce(jnp.maximum, gl)
    gidx = jnp.full((1, tr), N_EXPERT_GROUPS - 1, jnp.int32)
    for k in range(N_EXPERT_GROUPS - 2, -1, -1):
        gidx = jnp.where(gl[k] == gmax, k, gidx)
    gsum = functools.reduce(lambda a, b: a + b, [jnp.exp(v - gmax) for v in gl])
    g_w = 1.0 / gsum
    el = lg[0:per_group, :]
    for k in range(1, N_EXPERT_GROUPS):
        el = jnp.where(gidx == k, lg[k * per_group:(k + 1) * per_group, :], el)
    ep = jnp.exp(el - jnp.max(el, axis=0, keepdims=True))
    prob = ep / jnp.sum(ep, axis=0, keepdims=True)
    jj = lax.broadcasted_iota(jnp.int32, (per_group, tr), 0).astype(F32)
    p1 = jnp.max(prob, axis=0, keepdims=True)
    i1 = jnp.min(jnp.where(prob == p1, jj, float(per_group)), axis=0, keepdims=True)
    prob2 = jnp.where(jj == i1, -1.0, prob)
    p2 = jnp.max(prob2, axis=0, keepdims=True)
    i2 = jnp.min(jnp.where(prob2 == p2, jj, float(per_group)), axis=0, keepdims=True)
    den = p1 + p2
    gate_ref[0] = jnp.concatenate([g_w * p1 / den, g_w * p2 / den], axis=0)
    e1 = gidx * per_group + i1.astype(jnp.int32)
    e2 = gidx * per_group + i2.astype(jnp.int32)
    rr = lax.broadcasted_iota(jnp.int32, (n_experts, tr), 0)
    oh1 = rr == e1
    oh2 = rr == e2
    member = jnp.where(oh1 | oh2, 1.0, 0.0)

    @pl.when(ph == 0)
    def _():
        counts[...] = counts[...] + jnp.sum(member, axis=1, keepdims=True)
        dest_ref[...] = jnp.zeros(dest_ref.shape, jnp.int32)

    @pl.when(ph == 1)
    def _():
        @pl.when(i == 0)
        def _():
            blocks = (counts[...].astype(jnp.int32) + (block_rows - 1)) >> int(math.log2(block_rows))
            hi = (blocks >> 4).astype(F32).astype(BF16)
            lo = (blocks & 15).astype(F32).astype(BF16)
            er = lax.broadcasted_iota(jnp.int32, (n_experts, n_experts), 0)
            ec = lax.broadcasted_iota(jnp.int32, (n_experts, n_experts), 1)
            lower = jnp.where(ec < er, 1.0, 0.0).astype(BF16)
            pstart[...] = (16.0 * _dot(lower, hi) + _dot(lower, lo)) * float(block_rows)
            carry[...] = jnp.zeros(carry.shape, F32)

        run = carry[...]
        upper = upper_ref[...]
        pieces = []
        for b in range(tr // sub):
            mb = member[:, b * sub:(b + 1) * sub]
            pieces.append(_dot(mb.astype(BF16), upper) + jnp.concatenate([run] * (sub // LANES), axis=1))
            run = run + jnp.sum(mb, axis=1, keepdims=True)
        carry[...] = run
        base = jnp.concatenate(pieces, axis=1) + jnp.concatenate([pstart[...]] * (tr // LANES), axis=1)
        d1 = jnp.sum(jnp.where(oh1, base, 0.0), axis=0, keepdims=True)
        d2 = jnp.sum(jnp.where(oh2, base, 0.0), axis=0, keepdims=True)
        dest_ref[0] = jnp.concatenate([d1, d2], axis=0).astype(jnp.int32)

    cnt_ref[...] = counts[...].astype(jnp.int32)


def _route(logits_t, upper, n_experts, block_rows):
    rows, t = logits_t.shape
    tr = 1024
    sub = upper.shape[0]
    return pl.pallas_call(
        functools.partial(_route_kernel, n_experts=n_experts, per_group=n_experts // N_EXPERT_GROUPS,
                          block_rows=block_rows, sub=sub),
        grid=(2, t // tr),
        in_specs=[pl.BlockSpec((rows, tr), lambda ph, i: (0, i)),
                  pl.BlockSpec(upper.shape, lambda ph, i: (0, 0))],
        out_specs=[pl.BlockSpec((1, TOP_K, tr), lambda ph, i: (ph, 0, i)),
                   pl.BlockSpec((1, TOP_K, tr), lambda ph, i: (ph, 0, i)),
                   pl.BlockSpec((n_experts, LANES), lambda ph, i: (0, 0))],
        out_shape=[jax.ShapeDtypeStruct((2, TOP_K, t), jnp.int32), jax.ShapeDtypeStruct((2, TOP_K, t), F32),
                   jax.ShapeDtypeStruct((n_experts, LANES), jnp.int32)],
        scratch_shapes=[pltpu.VMEM((n_experts, LANES), F32)] * 3,
        compiler_params=_params("arbitrary", "arbitrary"),
        name="route",
    )(logits_t, upper)


DMA_UNROLL = 8


def _dispatch_kernel(dest_ref, h_ref, init_ref, rows_ref, sem, *, tq, t):
    del init_ref
    base = pl.program_id(0) * tq

    def issue(jb, _):
        for u in range(DMA_UNROLL):
            for k in range(TOP_K):
                pltpu.make_async_copy(h_ref.at[jb, pl.ds(u, 1)],
                                      rows_ref.at[pl.ds(dest_ref[k * t + base + jb * DMA_UNROLL + u], 1)],
                                      sem).start(priority=k % 2)
        return 0

    def drain(jb, _):
        for k in range(TOP_K):
            pltpu.make_async_copy(h_ref.at[jb], rows_ref.at[pl.ds(0, DMA_UNROLL)], sem).wait()
        return 0

    lax.fori_loop(0, tq // DMA_UNROLL, issue, 0)
    lax.fori_loop(0, tq // DMA_UNROLL, drain, 0)


def _dispatch(dest_flat, h2, n_pad):
    t, d = h2.shape
    tq = min(1024, t)
    init = jnp.zeros((n_pad, d), h2.dtype)
    return pl.pallas_call(
        functools.partial(_dispatch_kernel, tq=tq, t=t),
        grid_spec=pltpu.PrefetchScalarGridSpec(
            num_scalar_prefetch=1, grid=(t // tq,),
            in_specs=[pl.BlockSpec((tq // DMA_UNROLL, DMA_UNROLL, d), lambda i, dr: (i, 0, 0)),
                      pl.BlockSpec(memory_space=pl.ANY)],
            out_specs=pl.BlockSpec(memory_space=pl.ANY),
            scratch_shapes=[pltpu.SemaphoreType.DMA]),
        out_shape=jax.ShapeDtypeStruct((n_pad, d), h2.dtype),
        input_output_aliases={2: 0},
        compiler_params=_params("arbitrary"),
        name="dispatch",
    )(dest_flat, h2.reshape(t // DMA_UNROLL, DMA_UNROLL, d), init)


def _ffn_kernel(be_ref, first_ref, slot_ref, nxt_ref, hasnext_ref, nused_ref, x_hbm, wg_hbm, wu_hbm, wd_hbm,
                y_ref, sg, su, sd, wg16, wu16, wd16, sems, xbuf, xsems, *, layer):
    i = pl.program_id(0)
    n_used = nused_ref[0]
    bm = xbuf.shape[1]

    def row_copy(b):
        return pltpu.make_async_copy(x_hbm.at[pl.ds(b * bm, bm)], xbuf.at[b % 3], xsems.at[b % 3])

    @pl.when(i == 0)
    def _():
        row_copy(0).start()

        @pl.when(n_used > 1)
        def _():
            row_copy(1).start()

    @pl.when(i + 2 < n_used)
    def _():
        row_copy(i + 2).start()

    def weight_copies(e, s):
        return (pltpu.make_async_copy(wg_hbm.at[layer, e], sg.at[s], sems.at[s, 0]),
                pltpu.make_async_copy(wu_hbm.at[layer, e], su.at[s], sems.at[s, 1]),
                pltpu.make_async_copy(wd_hbm.at[layer, e], sd.at[s], sems.at[s, 2]))

    @pl.when(first_ref[i] == 1)
    def _():
        s = slot_ref[i]
        e = be_ref[i]

        @pl.when(i == 0)
        def _():
            for cp in weight_copies(e, s):
                cp.start()

        for cp in weight_copies(e, s):
            cp.wait()
        wg16[...] = sg[s].astype(BF16)
        wu16[...] = su[s].astype(BF16)
        wd16[...] = sd[s].astype(BF16)

        @pl.when(hasnext_ref[i] == 1)
        def _():
            for cp in weight_copies(nxt_ref[i], 1 - s):
                cp.start()

    @pl.when(i < nused_ref[0])
    def _():
        row_copy(i).wait()
        xb = _unpack_rows(xbuf[i % 3]).astype(BF16)
        a = jax.nn.silu(_dot(xb, wg16[...])) * _dot(xb, wu16[...])
        y_ref[...] = _pack_rows(_dot(a.astype(BF16), wd16[...]))

    @pl.when(i >= nused_ref[0])
    def _():
        y_ref[...] = jnp.zeros(y_ref.shape, y_ref.dtype)


def _ffn(counts, rows, w_eg, w_eu, w_ed, layer, bm):
    n_pad, dw = rows.shape
    _, n_experts, d, ff = w_eg.shape
    n_blocks = n_pad // bm
    ends = jnp.cumsum((counts + bm - 1) // bm).astype(jnp.int32)
    n_used = ends[-1]
    bidx = jnp.arange(n_blocks, dtype=jnp.int32)
    be = jnp.minimum(jnp.sum(ends[None, :] <= bidx[:, None], axis=1), n_experts - 1).astype(jnp.int32)
    be = jnp.where(bidx < n_used, be, be[n_used - 1])
    first = jnp.concatenate([jnp.ones((1,), jnp.int32), (be[1:] != be[:-1]).astype(jnp.int32)])
    slot = ((jnp.cumsum(first) - 1) & 1).astype(jnp.int32)
    seg_end = ends[be]
    has_next = (seg_end < n_used).astype(jnp.int32)
    nxt = be[jnp.minimum(seg_end, n_blocks - 1)]
    row = lambda i, *_: (i, 0)
    return pl.pallas_call(
        functools.partial(_ffn_kernel, layer=layer),
        grid_spec=pltpu.PrefetchScalarGridSpec(
            num_scalar_prefetch=6, grid=(n_blocks,),
            in_specs=[pl.BlockSpec(memory_space=pl.ANY)] * 4,
            out_specs=pl.BlockSpec((bm, dw), row),
            scratch_shapes=[pltpu.VMEM((2, d, ff), F32), pltpu.VMEM((2, d, ff), F32), pltpu.VMEM((2, ff, d), F32),
                            pltpu.VMEM((d, ff), BF16), pltpu.VMEM((d, ff), BF16), pltpu.VMEM((ff, d), BF16),
                            pltpu.SemaphoreType.DMA((2, 3)),
                            pltpu.VMEM((3, bm, dw), jnp.uint32), pltpu.SemaphoreType.DMA((3,))]),
        out_shape=jax.ShapeDtypeStruct((n_pad, dw), jnp.uint32),
        compiler_params=_params("arbitrary"),
        name="ffn",
    )(be, first, slot, nxt, has_next, n_used[None], rows, w_eg, w_eu, w_ed)


def _combine_kernel(dest_ref, yrows_ref, x_ref, g_ref, gate_ref, fw_ref, o_ref, buf, sems, *, tq, t, final):
    i = pl.program_id(0)
    n = pl.num_programs(0)

    def issue(tile):
        slot = tile % 2

        def body(jb, _):
            for u in range(DMA_UNROLL):
                for k in range(TOP_K):
                    pltpu.make_async_copy(
                        yrows_ref.at[pl.ds(dest_ref[k * t + tile * tq + jb * DMA_UNROLL + u], 1)],
                        buf.at[slot, k, jb, pl.ds(u, 1)], sems.at[slot]).start(priority=k % 2)
            return 0
        lax.fori_loop(0, tq // DMA_UNROLL, body, 0)

    @pl.when(i == 0)
    def _():
        issue(i)

    @pl.when(i + 1 < n)
    def _():
        issue(i + 1)

    def drain(jb, _):
        for k in range(TOP_K):
            pltpu.make_async_copy(yrows_ref.at[pl.ds(0, DMA_UNROLL)], buf.at[i % 2, k, jb], sems.at[i % 2]).wait()
        return 0

    lax.fori_loop(0, tq // DMA_UNROLL, drain, 0)
    g = g_ref[...]
    rows = buf[i % 2].reshape(TOP_K, tq, buf.shape[-1])
    f = g[:, 0:1] * _unpack_rows(rows[0]) + g[:, 1:2] * _unpack_rows(rows[1])
    x2 = x_ref[...] + gate_ref[0] * f
    if final:
        x2 = x2 * lax.rsqrt(jnp.mean(x2 * x2, axis=-1, keepdims=True) + EPS) * fw_ref[...]
    o_ref[...] = x2


def _combine(dest_flat, y_rows, x1, gates_t, ada3, fw, layer, bsz, seqlen, final):
    t, d = x1.shape
    tq = min(1024, seqlen)
    per_b = seqlen // tq
    return pl.pallas_call(
        functools.partial(_combine_kernel, tq=tq, t=t, final=final),
        grid_spec=pltpu.PrefetchScalarGridSpec(
            num_scalar_prefetch=1, grid=(t // tq,),
            in_specs=[pl.BlockSpec(memory_space=pl.ANY),
                      pl.BlockSpec((tq, d), lambda i, dr: (i, 0)),
                      pl.BlockSpec((tq, TOP_K), lambda i, dr: (i, 0)),
                      pl.BlockSpec((1, 1, d), lambda i, dr: (layer * bsz + i // per_b, 0, 5)),
                      pl.BlockSpec((1, d), lambda i, dr: (0, 0))],
            out_specs=pl.BlockSpec((tq, d), lambda i, dr: (i, 0)),
            scratch_shapes=[pltpu.VMEM((2, TOP_K, tq // DMA_UNROLL, DMA_UNROLL, d // 2), jnp.uint32),
                            pltpu.SemaphoreType.DMA((2,))]),
        out_shape=jax.ShapeDtypeStruct((t, d), F32),
        compiler_params=_params("arbitrary"),
        name="combine",
    )(dest_flat, y_rows, x1, gates_t, ada3, fw)


def kernel(x, c, w_ada, b_ada, norm1_w, w_in, conv_w, conv_b, dt_bias, a_log, d_ssd, ssd_norm_w,
           s5_lam_re, s5_lam_im, s5_log_dt, s5_b_re, s5_b_im, s5_c_re, s5_c_im, s5_d, w_glu, b_glu,
           s5_norm_w, w_out, norm2_w, w_rg, b_rg, w_re, b_re, w_eg, w_eu, w_ed, final_norm_w):
    bsz, seqlen, d = x.shape
    t = bsz * seqlen
    depth = w_in.shape[0]
    heads = dt_bias.shape[1]
    inner = ssd_norm_w.shape[1]
    cch = conv_w.shape[2]
    n5 = s5_d.shape[1]
    s5_groups, _, s5_h = s5_b_re.shape[1:]
    n_experts = w_re.shape[2]
    moe_block = 512
    n_rows = t * TOP_K
    n_pad = n_rows + n_experts * moe_block
    n_blocks = n_pad // moe_block

    i0, i1, i2 = inner, inner + cch, inner + cch + heads
    w_zx = w_in[..., :i1].astype(BF16)
    n_du = w_in.shape[2] - i1
    w_du = jnp.pad(w_in[..., i1:].astype(BF16), ((0, 0), (0, 0), (0, -n_du % LANES)))
    lane_pad = lambda a: jnp.pad(a, ((0, 0), (0, LANES - a.shape[1])))[:, None, :]
    dtb, alog = lane_pad(dt_bias), lane_pad(a_log)
    dexp = jnp.repeat(d_ssd, inner // heads, axis=1)[:, None, :]
    li = jnp.arange(SSD_CHUNK)
    tri = (li[None, :] <= li[:, None]).astype(BF16)
    e_mat = (jnp.arange(LANES)[:, None] == (jnp.arange(inner) // (inner // heads))[None, :]).astype(BF16)
    w_out_a = w_out[:, :inner].astype(BF16)
    w_out_b = w_out[:, inner:].astype(BF16)
    w_glu_b = w_glu.astype(BF16)
    wr = jnp.concatenate([w_re, w_rg, jnp.zeros((depth, d, LANES - n_experts - N_EXPERT_GROUPS), F32)], axis=-1)
    wr_hi = wr.astype(BF16)
    wr_lo = (wr - wr_hi.astype(F32)).astype(BF16)
    br = jnp.concatenate([b_re, b_rg, jnp.zeros((depth, LANES - n_experts - N_EXPERT_GROUPS), F32)], axis=-1)[:, None, :]
    sub = 256
    si = jnp.arange(sub)
    upper = (si[:, None] < si[None, :]).astype(BF16)

    ada3 = _ada(c, w_ada, b_ada).reshape(depth * bsz, 1, 6 * d)
    x2 = x.reshape(t, d)
    s5w = _s5_weights(s5_lam_re, s5_lam_im, s5_log_dt, s5_b_re, s5_b_im, s5_c_re, s5_c_im)
    for l in range(depth):
        z, xa, u5, dtp = _inproj(x2, norm1_w[l][None], ada3, w_zx[l], w_du[l], conv_w[l], conv_b[l][None], n5, l,
                                 bsz, seqlen, inner, heads)
        y_ssd = _ssd(z, xa, dtp, dtb[l], alog[l], dexp[l], ssd_norm_w[l][None], tri, e_mat, bsz, seqlen, heads)
        y5 = _s5_core(u5, *s5w, l, bsz, seqlen, s5_h)
        x1, h2, logits_t = _outproj(x2, y_ssd, y5, u5, s5_d[l][None], w_glu_b[l], b_glu[l][None],
                                    s5_norm_w[l][None], w_out_a[l], w_out_b[l], norm2_w[l][None], ada3,
                                    wr_hi[l], wr_lo[l], br[l], l, bsz, seqlen)
        dest, gates, counts = _route(logits_t, upper, n_experts, moe_block)
        dest_flat = dest[1].reshape(-1)
        rows = _dispatch(dest_flat, h2, n_pad)
        y_rows = _ffn(counts[:, 0], rows, w_eg, w_eu, w_ed, l, moe_block)
        x2 = _combine(dest_flat, y_rows, x1, gates[1].T, ada3, final_norm_w[None], l, bsz, seqlen, l == depth - 1)
    return x2.reshape(bsz, seqlen, d)
```
